```python
import jax, jax.numpy as jnp
from jax import lax
import numpy as np

D_MODEL = 2048
BATCH = 4
SEQ = 2048
DEPTH = 2
DEC_BATCH = 128
DEC_SEQ = 1
PAST_LEN = 8192
PAGE_SIZE = 128

M_HEADS = 4
M_DQK = 256
M_DV = 512
M_CHUNK = 64
F_BIAS_INIT = 3.0
N_HEADS = 32
KV_HEADS = 4
HEAD_DIM = 64
GROUP = N_HEADS // KV_HEADS
ROT_DIM = HEAD_DIM // 4
ROPE_THETA = 500000.0
WINDOW = 128
ATT_BLOCK = 128
D_FF = 5632
N_EXPERTS = 8
TOP_K = 2
E_FF = 7168
N_DENSE = (DEPTH + 1) // 2
N_MOE = DEPTH // 2
EPS = 1e-6
IN_SIZES = (M_HEADS * M_DQK, M_HEADS * M_DQK, M_HEADS * M_DV, M_HEADS * M_DV, M_HEADS, M_HEADS,
            N_HEADS * HEAD_DIM, KV_HEADS * HEAD_DIM, KV_HEADS * HEAD_DIM, D_MODEL, D_MODEL)
N_IN = 2 * M_HEADS * M_DQK + 2 * M_HEADS * M_DV + 2 * M_HEADS + N_HEADS * HEAD_DIM + 2 * KV_HEADS * HEAD_DIM + 2 * D_MODEL

kernel_name = "hybrid_mlstm_swa_sinks_moe_decoder_step"


def rmsnorm(x, g):
    xf = x.astype(jnp.float32)
    y = xf * lax.rsqrt(jnp.mean(xf * xf, axis=-1, keepdims=True) + EPS)
    return (y * g.astype(jnp.float32)).astype(x.dtype)


def rope_partial(x, pos):
    half = ROT_DIM // 2
    inv = jnp.power(ROPE_THETA, -jnp.arange(half, dtype=jnp.float32) / half)
    ang = pos.astype(jnp.float32)[:, None] * inv[None, :]
    cos = jnp.cos(ang)[:, None, :]
    sin = jnp.sin(ang)[:, None, :]
    xr = x[..., :ROT_DIM].astype(jnp.float32)
    x1, x2 = xr[..., :half], xr[..., half:]
    rot = jnp.concatenate([x1 * cos - x2 * sin, x2 * cos + x1 * sin], axis=-1).astype(x.dtype)
    return jnp.concatenate([rot, x[..., ROT_DIM:]], axis=-1)


def mlstm_chunkwise(q, k, v, i_pre, log_f, C0, n0, m0):
    B, T, H, _ = q.shape
    L = M_CHUNK if T % M_CHUNK == 0 else T
    NC = T // L
    f32 = jnp.float32

    def chunks(a):
        a = a.astype(f32).reshape((B, NC, L) + a.shape[2:])
        return jnp.swapaxes(jnp.moveaxis(a, 1, 0), 2, 3)

    causal = jnp.tril(jnp.ones((L, L), dtype=bool))

    def step(carry, xs):
        C, n, m = carry
        qc, kc, vc, ic, fc = xs
        b = jnp.cumsum(fc, axis=-1)
        a = b + m[..., None]
        D = b[..., :, None] - b[..., None, :] + ic[..., None, :]
        D = jnp.where(causal, D, -jnp.inf)
        m_t = jnp.maximum(a, jnp.max(D, axis=-1))
        w_intra = jnp.exp(D - m_t[..., None])
        w_inter = jnp.exp(a - m_t)
        s = jnp.einsum('bhtd,bhsd->bhts', qc, kc) * w_intra
        num = w_inter[..., None] * jnp.einsum('bhtd,bhde->bhte', qc, C) + jnp.einsum('bhts,bhse->bhte', s, vc)
        den = w_inter * jnp.einsum('bhtd,bhd->bht', qc, n) + jnp.sum(s, axis=-1)
        h = num / jnp.maximum(jnp.abs(den), jnp.exp(-m_t))[..., None]
        m_new = m_t[..., -1]
        w_state = jnp.exp(b[..., -1:] - b + ic - m_new[..., None])
        decay = jnp.exp(b[..., -1] + m - m_new)
        C_new = decay[..., None, None] * C + jnp.einsum('bhs,bhsd,bhse->bhde', w_state, kc, vc)
        n_new = decay[..., None] * n + jnp.einsum('bhs,bhsd->bhd', w_state, kc)
        return (C_new, n_new, m_new), h

    xs = (chunks(q), chunks(k), chunks(v), chunks(i_pre), chunks(log_f))
    (C, n, m), h = lax.scan(step, (C0.astype(f32), n0.astype(f32), m0.astype(f32)), xs)
    h = jnp.swapaxes(jnp.moveaxis(h, 0, 1), 2, 3).reshape(B, T, H, v.shape[-1])
    return h.astype(v.dtype), C.astype(C0.dtype), n.astype(n0.dtype), m.astype(m0.dtype)


def window_attend(q, k, v, q_pos, k_pos, sinks):
    B, NB, TQ = q.shape[:3]
    qg = q.reshape(B, NB, TQ, KV_HEADS, GROUP, HEAD_DIM)
    s = jnp.einsum('bnqkgd,bnskd->bnkgqs', qg, k, preferred_element_type=jnp.float32) * (HEAD_DIM ** -0.5)
    rel = q_pos[:, :, None] - k_pos[:, None, :]
    mask = (rel >= 0) & (rel <= WINDOW) & (k_pos[:, None, :] >= 0)
    s = jnp.where(mask[None, :, None, None], s, -jnp.inf)
    sink = sinks.astype(jnp.float32).reshape(1, 1, KV_HEADS, GROUP, 1, 1)
    mx = jnp.maximum(jnp.max(s, axis=-1, keepdims=True), sink)
    p = jnp.exp(s - mx)
    denom = jnp.sum(p, axis=-1, keepdims=True) + jnp.exp(sink - mx)
    o = jnp.einsum('bnkgqs,bnskd->bnqkgd', (p / denom).astype(v.dtype), v)
    return o.reshape(B, NB * TQ, N_HEADS * HEAD_DIM)


def swa_prompt(q, k, v, sinks):
    B, T = q.shape[:2]
    NB = T // ATT_BLOCK
    qb = q.reshape(B, NB, ATT_BLOCK, N_HEADS, HEAD_DIM)
    kb = k.reshape(B, NB, ATT_BLOCK, KV_HEADS, HEAD_DIM)
    vb = v.reshape(B, NB, ATT_BLOCK, KV_HEADS, HEAD_DIM)
    pad = ((0, 0), (1, 0), (0, 0), (0, 0), (0, 0))
    kk = jnp.concatenate([jnp.pad(kb, pad)[:, :-1], kb], axis=2)
    vv = jnp.concatenate([jnp.pad(vb, pad)[:, :-1], vb], axis=2)
    pos = jnp.arange(T, dtype=jnp.int32).reshape(NB, ATT_BLOCK)
    k_pos = jnp.concatenate([pos - ATT_BLOCK, pos], axis=1)
    return window_attend(qb, kk, vv, pos, k_pos, sinks)


def swa_decode(q, k_new, v_new, k_buf, v_buf, sinks):
    T = q.shape[1]
    kk = jnp.concatenate([k_buf.astype(k_new.dtype), k_new], axis=1)
    vv = jnp.concatenate([v_buf.astype(v_new.dtype), v_new], axis=1)
    q_pos = (PAST_LEN + jnp.arange(T, dtype=jnp.int32))[None]
    k_pos = (PAST_LEN - WINDOW + jnp.arange(WINDOW + T, dtype=jnp.int32))[None]
    o = window_attend(q[:, None], kk[:, None], vv[:, None], q_pos, k_pos, sinks)
    return o, kk[:, -WINDOW:], vv[:, -WINDOW:]


def token_mixer(h, pos, C0, n0, m0, k_buf, v_buf, w_in, b_i, b_f, g_mnorm, sinks, w_bm, w_ba, w_out):
    B, T = h.shape[:2]
    split_points = [int(p) for p in np.cumsum(IN_SIZES)[:-1]]
    z = h @ w_in
    q_m, k_m, v_m, o_m, i_m, f_m, q_a, k_a, v_a, g_m, g_a = jnp.split(z, split_points, axis=-1)
    q_m = q_m.reshape(B, T, M_HEADS, M_DQK) * (M_DQK ** -0.5)
    k_m = k_m.reshape(B, T, M_HEADS, M_DQK)
    v_m = v_m.reshape(B, T, M_HEADS, M_DV)
    i_pre = i_m.astype(jnp.float32) + b_i.astype(jnp.float32)
    log_f = jax.nn.log_sigmoid(f_m.astype(jnp.float32) + b_f.astype(jnp.float32))
    hm, C, n, m = mlstm_chunkwise(q_m, k_m, v_m, i_pre, log_f, C0, n0, m0)
    hm = rmsnorm(hm, g_mnorm.reshape(M_HEADS, M_DV)).reshape(B, T, M_HEADS * M_DV) * jax.nn.sigmoid(o_m)
    q_a = rope_partial(q_a.reshape(B, T, N_HEADS, HEAD_DIM), pos)
    k_a = rope_partial(k_a.reshape(B, T, KV_HEADS, HEAD_DIM), pos)
    v_a = v_a.reshape(B, T, KV_HEADS, HEAD_DIM)
    if k_buf is None:
        ha = swa_prompt(q_a, k_a, v_a, sinks)
        k_keep, v_keep = k_a[:, -WINDOW:], v_a[:, -WINDOW:]
    else:
        ha, k_keep, v_keep = swa_decode(q_a, k_a, v_a, k_buf, v_buf, sinks)
    merged = jax.nn.sigmoid(g_m) * (hm @ w_bm) + jax.nn.sigmoid(g_a) * (ha @ w_ba)
    return merged @ w_out, (C, n, m, k_keep, v_keep)


def swiglu(h, wg, wu, wd):
    return (jax.nn.silu(h @ wg) * (h @ wu)) @ wd


def moe_swiglu(h, w_router, wg, wu, wd):
    logits = (h @ w_router).astype(jnp.float32)
    top_val, top_idx = lax.top_k(logits, TOP_K)
    gates = jax.nn.softmax(top_val, axis=-1)
    dense_gate = jnp.sum(jax.nn.one_hot(top_idx, N_EXPERTS, dtype=jnp.float32) * gates[..., None], axis=-2)
    dense_gate = dense_gate.astype(h.dtype)
    out = jnp.zeros(h.shape[:-1] + (wd.shape[-1],), h.dtype)
    for e in range(N_EXPERTS):
        out = out + dense_gate[..., e:e + 1] * swiglu(h, wg[e], wu[e], wd[e])
    return out


def channel_mixer(h, layer, w_gate_dense, w_up_dense, w_down_dense, w_router, w_gate_moe, w_up_moe, w_down_moe):
    j = layer // 2
    if layer % 2 == 0:
        return swiglu(h, w_gate_dense[j], w_up_dense[j], w_down_dense[j])
    return moe_swiglu(h, w_router[j], w_gate_moe[j], w_up_moe[j], w_down_moe[j])


def setup_inputs(seed: int = 0) -> dict:
    key = jax.random.key(seed)
    ks = jax.random.split(key, 32)
    f32 = jnp.float32

    def nrm(k, shape, scale):
        return scale * jax.random.normal(k, shape, f32)

    return {
        'x_prompt': nrm(ks[0], (BATCH, SEQ, D_MODEL), 1.0),
        'x_sample': nrm(ks[1], (DEC_BATCH, DEC_SEQ, D_MODEL), 1.0),
        'state_mlstm_C': nrm(ks[2], (DEPTH, DEC_BATCH, M_HEADS, M_DQK, M_DV), 0.5),
        'state_mlstm_n': nrm(ks[3], (DEPTH, DEC_BATCH, M_HEADS, M_DQK), 1.0),
        'state_mlstm_m': nrm(ks[4], (DEPTH, DEC_BATCH, M_HEADS), 1.0),
        'cache_swa_k': nrm(ks[5], (DEPTH, DEC_BATCH, WINDOW, KV_HEADS, HEAD_DIM), 1.0),
        'cache_swa_v': nrm(ks[6], (DEPTH, DEC_BATCH, WINDOW, KV_HEADS, HEAD_DIM), 1.0),
        'norm_mix_g': 1.0 + nrm(ks[7], (DEPTH, D_MODEL), 0.02),
        'w_in': nrm(ks[8], (DEPTH, D_MODEL, N_IN), D_MODEL ** -0.5),
        'b_igate': nrm(ks[9], (DEPTH, M_HEADS), 0.1),
        'b_fgate': F_BIAS_INIT + nrm(ks[10], (DEPTH, M_HEADS), 0.5),
        'mlstm_norm_g': 1.0 + nrm(ks[11], (DEPTH, M_HEADS * M_DV), 0.02),
        'attn_sinks': nrm(ks[12], (DEPTH, N_HEADS), 1.0),
        'w_branch_m': nrm(ks[13], (DEPTH, M_HEADS * M_DV, D_MODEL), (M_HEADS * M_DV) ** -0.5),
        'w_branch_a': nrm(ks[14], (DEPTH, N_HEADS * HEAD_DIM, D_MODEL), (N_HEADS * HEAD_DIM) ** -0.5),
        'w_out': nrm(ks[15], (DEPTH, D_MODEL, D_MODEL), D_MODEL ** -0.5),
        'norm_ffn_g': 1.0 + nrm(ks[16], (DEPTH, D_MODEL), 0.02),
        'w_gate_dense': nrm(ks[17], (N_DENSE, D_MODEL, D_FF), D_MODEL ** -0.5),
        'w_up_dense': nrm(ks[18], (N_DENSE, D_MODEL, D_FF), D_MODEL ** -0.5),
        'w_down_dense': nrm(ks[19], (N_DENSE, D_FF, D_MODEL), D_FF ** -0.5),
        'w_router': nrm(ks[20], (N_MOE, D_MODEL, N_EXPERTS), D_MODEL ** -0.5),
        'w_gate_moe': nrm(ks[21], (N_MOE, N_EXPERTS, D_MODEL, E_FF), D_MODEL ** -0.5),
        'w_up_moe': nrm(ks[22], (N_MOE, N_EXPERTS, D_MODEL, E_FF), D_MODEL ** -0.5),
        'w_down_moe': nrm(ks[23], (N_MOE, N_EXPERTS, E_FF, D_MODEL), E_FF ** -0.5),
        'norm_final_g': 1.0 + nrm(ks[24], (D_MODEL,), 0.02),
    }


def reference(x_prompt, x_sample, state_mlstm_C, state_mlstm_n, state_mlstm_m, cache_swa_k, cache_swa_v,
              norm_mix_g, w_in, b_igate, b_fgate, mlstm_norm_g, attn_sinks, w_branch_m, w_branch_a, w_out,
              norm_ffn_g, w_gate_dense, w_up_dense, w_down_dense, w_router, w_gate_moe, w_up_moe, w_down_moe,
              norm_final_g):
    B_p, T_p = x_prompt.shape[:2]
    T_s = x_sample.shape[1]
    pos_p = jnp.arange(T_p, dtype=jnp.int32)
    pos_s = PAST_LEN + jnp.arange(T_s, dtype=jnp.int32)
    zC = jnp.zeros((B_p, M_HEADS, M_DQK, M_DV), x_prompt.dtype)
    zn = jnp.zeros((B_p, M_HEADS, M_DQK), x_prompt.dtype)
    zm = jnp.zeros((B_p, M_HEADS), x_prompt.dtype)
    xp, xs = x_prompt, x_sample
    st_prompt = [[], [], [], [], []]
    st_sample = [[], [], [], [], []]
    for l in range(DEPTH):
        mix_w = (w_in[l], b_igate[l], b_fgate[l], mlstm_norm_g[l], attn_sinks[l], w_branch_m[l], w_branch_a[l], w_out[l])
        yp, sp = token_mixer(rmsnorm(xp, norm_mix_g[l]), pos_p, zC, zn, zm, None, None, *mix_w)
        ys, ss = token_mixer(rmsnorm(xs, norm_mix_g[l]), pos_s, state_mlstm_C[l], state_mlstm_n[l],
                             state_mlstm_m[l], cache_swa_k[l], cache_swa_v[l], *mix_w)
        xp = xp + yp
        xs = xs + ys
        ffn_w = (w_gate_dense, w_up_dense, w_down_dense, w_router, w_gate_moe, w_up_moe, w_down_moe)
        xp = xp + channel_mixer(rmsnorm(xp, norm_ffn_g[l]), l, *ffn_w)
        xs = xs + channel_mixer(rmsnorm(xs, norm_ffn_g[l]), l, *ffn_w)
        for lst, a in zip(st_prompt, sp):
            lst.append(a)
        for lst, a in zip(st_sample, ss):
            lst.append(a)
    y_prompt = rmsnorm(xp, norm_final_g)
    y_sample = rmsnorm(xs, norm_final_g)
    new_C_prompt = jnp.stack(st_prompt[0])
    new_n_prompt = jnp.stack(st_prompt[1])
    new_m_prompt = jnp.stack(st_prompt[2])
    new_k_prompt = jnp.stack(st_prompt[3])
    new_v_prompt = jnp.stack(st_prompt[4])
    new_C_sample = jnp.stack(st_sample[0])
    new_n_sample = jnp.stack(st_sample[1])
    new_m_sample = jnp.stack(st_sample[2])
    new_k_sample = jnp.stack(st_sample[3])
    new_v_sample = jnp.stack(st_sample[4])
    return (y_prompt, y_sample, new_C_prompt, new_n_prompt, new_m_prompt, new_k_prompt, new_v_prompt,
            new_C_sample, new_n_sample, new_m_sample, new_k_sample, new_v_sample)
```

```python
import functools

import jax
import jax.numpy as jnp
import numpy as np
from jax import lax
from jax.experimental import pallas as pl
from jax.experimental.pallas import tpu as pltpu

F32 = jnp.float32
BF16 = jnp.bfloat16

D_MODEL = 2048
BATCH = 4
SEQ = 2048
DEPTH = 2
DEC_BATCH = 128
PAST_LEN = 8192
M_HEADS = 4
M_DQK = 256
M_DV = 512
N_HEADS = 32
KV_HEADS = 4
HEAD_DIM = 64
GROUP = N_HEADS // KV_HEADS
ROT_DIM = HEAD_DIM // 4
ROPE_THETA = 500000.0
WINDOW = 128
ATT_BLOCK = 128
D_FF = 5632
N_EXPERTS = 8
TOP_K = 2
E_FF = 7168
EPS = 1e-6

MP = BATCH * SEQ
MS = DEC_BATCH
M = MP + MS

C_QM = 0
C_KM = C_QM + M_HEADS * M_DQK
C_VM = C_KM + M_HEADS * M_DQK
C_OM = C_VM + M_HEADS * M_DV
C_QA = C_OM + M_HEADS * M_DV
C_KA = C_QA + N_HEADS * HEAD_DIM
C_VA = C_KA + KV_HEADS * HEAD_DIM
C_GM = C_VA + KV_HEADS * HEAD_DIM
C_GA = C_GM + D_MODEL
N_Z = C_GA + D_MODEL
R_GATES = 2 * M_HEADS * M_DQK + 2 * M_HEADS * M_DV
R_QA = R_GATES + 2 * M_HEADS

LANES = 128
TM = 1040
TN = 512
ML = 256
TB = 8
MOE_TM = 256
MOE_TF = 512
MOE_TN = 512
N_SLOT_TILES = (TOP_K * M + N_EXPERTS * (MOE_TM - 1)) // MOE_TM + 1
N_SLOTS = N_SLOT_TILES * MOE_TM
GATHER_CHUNK = 256
VMEM_LIMIT = 56 * 1024 * 1024


def _cparams(sem, vmem=VMEM_LIMIT):
    return pltpu.CompilerParams(dimension_semantics=sem, vmem_limit_bytes=vmem)


def _rms(x, g):
    ms = jnp.mean(x * x, axis=-1, keepdims=True)
    return (x * lax.rsqrt(ms + EPS)) * g


def _sigmoid(x):
    return 1.0 / (1.0 + jnp.exp(-x))


def _log_sigmoid(x):
    return jnp.minimum(x, 0.0) - jnp.log(1.0 + jnp.exp(-jnp.abs(x)))


J_QA0 = C_QA // TN
J_KV = C_KA // TN
assert C_QA % TN == 0 and C_KA % TN == 0 and (C_VA - C_KA) * 2 == TN and C_GM == C_KA + TN


def _rope(acc, cs, sa, sb):
    n = acc.shape[1]
    return acc * cs + pltpu.roll(acc, n - ROT_DIM // 2, 1) * sa + pltpu.roll(acc, ROT_DIM // 2, 1) * sb


def _inproj_kernel(x_ref, g_ref, w_ref, wgate_ref, cs_ref, sa_ref, sb_ref, z_ref, kv_ref, gt_ref, h_scr):
    j = pl.program_id(1)

    @pl.when(j == 0)
    def _():
        h = _rms(x_ref[...], g_ref[...]).astype(BF16)
        h_scr[...] = h
        gt_ref[...] = jnp.dot(h, wgate_ref[...], preferred_element_type=F32)

    acc = jnp.dot(h_scr[...], w_ref[...], preferred_element_type=F32)
    reps = TN // LANES
    is_q = jnp.logical_and(j >= J_QA0, j < J_KV)
    is_kv = j == J_KV

    @pl.when(is_q)
    def _():
        cs = jnp.tile(cs_ref[...], (1, reps))
        sa = jnp.tile(sa_ref[...], (1, reps))
        sb = jnp.tile(sb_ref[...], (1, reps))
        z_ref[...] = _rope(acc, cs, sa, sb).astype(BF16)

    @pl.when(is_kv)
    def _():
        is_k = lax.broadcasted_iota(jnp.int32, acc.shape, 1) < (C_VA - C_KA)
        cs = jnp.where(is_k, jnp.tile(cs_ref[...], (1, reps)), 1.0)
        sa = jnp.where(is_k, jnp.tile(sa_ref[...], (1, reps)), 0.0)
        sb = jnp.where(is_k, jnp.tile(sb_ref[...], (1, reps)), 0.0)
        r = _rope(acc, cs, sa, sb)
        kv_ref[...] = r
        z_ref[...] = r.astype(BF16)

    @pl.when(jnp.logical_not(jnp.logical_or(is_q, is_kv)))
    def _():
        z_ref[...] = acc.astype(BF16)


def _inproj(x, g, w_al, w_gate, cs, sa, sb):
    return pl.pallas_call(
        _inproj_kernel,
        grid=(M // TM, N_Z // TN),
        in_specs=[
            pl.BlockSpec((TM, D_MODEL), lambda i, j: (i, 0)),
            pl.BlockSpec((1, D_MODEL), lambda i, j: (0, 0)),
            pl.BlockSpec((D_MODEL, TN), lambda i, j: (0, j)),
            pl.BlockSpec((D_MODEL, LANES), lambda i, j: (0, 0)),
            pl.BlockSpec((TM, LANES), lambda i, j: (i, 0)),
            pl.BlockSpec((TM, LANES), lambda i, j: (i, 0)),
            pl.BlockSpec((TM, LANES), lambda i, j: (i, 0)),
        ],
        out_specs=[
            pl.BlockSpec((TM, TN), lambda i, j: (i, j)),
            pl.BlockSpec((TM, TN), lambda i, j: (i, 0)),
            pl.BlockSpec((TM, LANES), lambda i, j: (i, 0)),
        ],
        out_shape=[
            jax.ShapeDtypeStruct((M, N_Z), BF16),
            jax.ShapeDtypeStruct((M, TN), F32),
            jax.ShapeDtypeStruct((M, LANES), F32),
        ],
        scratch_shapes=[pltpu.VMEM((TM, D_MODEL), BF16)],
        compiler_params=_cparams(("parallel", "arbitrary")),
        name="inproj",
    )(x, g, w_al, w_gate, cs, sa, sb)


NC = SEQ // ML


def _mlstm_prompt_kernel(q_ref, k_ref, v_ref, o_ref, gt_ref, bias_ref, gn_ref,
                         hm_ref, c_ref, n_ref, m_ref, c_scr, n_scr, m_scr):
    h = pl.program_id(1)
    c = pl.program_id(2)

    @pl.when(c == 0)
    def _():
        c_scr[...] = jnp.zeros_like(c_scr)
        n_scr[...] = jnp.zeros_like(n_scr)
        m_scr[...] = jnp.zeros_like(m_scr)

    gates = gt_ref[...] + bias_ref[...]
    lane = lax.broadcasted_iota(jnp.int32, gates.shape, 1)
    i_msk = jnp.where(lane == h, gates, 0.0)
    lf_msk = jnp.where(lane == h + M_HEADS, _log_sigmoid(gates), 0.0)
    i_col = jnp.sum(i_msk, axis=1, keepdims=True)
    row = lax.broadcasted_iota(jnp.int32, (ML, ML), 0)
    col = lax.broadcasted_iota(jnp.int32, (ML, ML), 1)
    causal = col <= row
    b_full = jnp.dot(causal.astype(F32), lf_msk, preferred_element_type=F32,
                     precision=lax.Precision.HIGHEST)
    b_col = jnp.sum(b_full, axis=1, keepdims=True)
    both = jnp.where(lane == 0, b_col, jnp.where(lane == 1, i_col, 0.0))
    both_t = both.T
    b_row = both_t[0:1, :]
    i_row = both_t[1:2, :]

    m_prev = m_scr[...]
    a_col = b_col + m_prev
    dmat = jnp.where(causal, b_col - b_row + i_row, -jnp.inf)
    m_t = jnp.maximum(a_col, jnp.max(dmat, axis=1, keepdims=True))
    w_intra = jnp.exp(dmat - m_t)
    w_inter = jnp.exp(a_col - m_t)

    q = q_ref[...]
    k = k_ref[...]
    v = v_ref[...]
    qscale = M_DQK ** -0.5
    s = lax.dot_general(q, k, (((1,), (1,)), ((), ())), preferred_element_type=F32) * qscale * w_intra
    qf = q.astype(F32) * qscale
    inter = jnp.dot(q, c_scr[...].astype(BF16), preferred_element_type=F32) * qscale
    num = w_inter * inter + jnp.dot(s.astype(BF16), v, preferred_element_type=F32)
    den = w_inter * jnp.sum(qf * n_scr[...], axis=1, keepdims=True) + jnp.sum(s, axis=1, keepdims=True)
    hh = num / jnp.maximum(jnp.abs(den), jnp.exp(-m_t))
    y = _rms(hh, gn_ref[...]) * _sigmoid(o_ref[...].astype(F32))
    hm_ref[...] = y.astype(BF16)

    m_new = m_t[ML - 1:ML, :]
    b_last = b_col[ML - 1:ML, :]
    w_state = jnp.exp(b_last - b_col + i_col - m_new)
    decay = jnp.exp(b_last + m_prev - m_new)
    kw = k.astype(F32) * w_state
    c_new = decay * c_scr[...] + lax.dot_general(kw.astype(BF16), v, (((0,), (0,)), ((), ())),
                                                 preferred_element_type=F32)
    n_new = decay * n_scr[...] + jnp.sum(kw, axis=0, keepdims=True)
    c_scr[...] = c_new
    n_scr[...] = n_new
    m_scr[...] = m_new

    @pl.when(c == NC - 1)
    def _():
        c_ref[...] = c_new
        n_ref[...] = n_new
        m_ref[...] = jnp.broadcast_to(m_new, m_ref.shape)


def _mlstm_prompt(z, gates, bias, gn):
    qb, vb = M_DQK, M_DV
    return pl.pallas_call(
        _mlstm_prompt_kernel,
        grid=(BATCH, M_HEADS, NC),
        in_specs=[
            pl.BlockSpec((ML, qb), lambda b, h, c: (b * NC + c, C_QM // qb + h)),
            pl.BlockSpec((ML, qb), lambda b, h, c: (b * NC + c, C_KM // qb + h)),
            pl.BlockSpec((ML, vb), lambda b, h, c: (b * NC + c, C_VM // vb + h)),
            pl.BlockSpec((ML, vb), lambda b, h, c: (b * NC + c, C_OM // vb + h)),
            pl.BlockSpec((ML, LANES), lambda b, h, c: (b * NC + c, 0)),
            pl.BlockSpec((1, LANES), lambda b, h, c: (0, 0)),
            pl.BlockSpec((1, vb), lambda b, h, c: (0, h)),
        ],
        out_specs=[
            pl.BlockSpec((ML, vb), lambda b, h, c: (b * NC + c, h)),
            pl.BlockSpec((None, None, M_DQK, M_DV), lambda b, h, c: (b, h, 0, 0)),
            pl.BlockSpec((None, None, 1, M_DQK), lambda b, h, c: (b, h, 0, 0)),
            pl.BlockSpec((None, None, 1, LANES), lambda b, h, c: (b, h, 0, 0)),
        ],
        out_shape=[
            jax.ShapeDtypeStruct((MP, M_HEADS * M_DV), BF16),
            jax.ShapeDtypeStruct((BATCH, M_HEADS, M_DQK, M_DV), F32),
            jax.ShapeDtypeStruct((BATCH, M_HEADS, 1, M_DQK), F32),
            jax.ShapeDtypeStruct((BATCH, M_HEADS, 1, LANES), F32),
        ],
        scratch_shapes=[pltpu.VMEM((M_DQK, M_DV), F32), pltpu.VMEM((1, M_DQK), F32), pltpu.VMEM((1, 1), F32)],
        compiler_params=_cparams(("parallel", "parallel", "arbitrary")),
        name="mlstm_prompt",
    )(z, z, z, z, gates, bias, gn)


def _mlstm_decode_kernel(q_ref, k_ref, v_ref, o_ref, qt_ref, kt_ref, gt_ref, bias_ref, m0_ref, gn_ref,
                         c0_ref, n0_ref, hm_ref, c_ref, n_ref, m_ref):
    i = pl.program_id(0)
    h = pl.program_id(1)
    gates = gt_ref[...] + bias_ref[...]
    lane = lax.broadcasted_iota(jnp.int32, gates.shape, 1)
    log_f = pltpu.roll(_log_sigmoid(gates), LANES - M_HEADS, 1)
    a = log_f + m0_ref[...]
    m_t = jnp.maximum(a, gates)
    w_intra_all = jnp.exp(gates - m_t)
    w_inter_all = jnp.exp(a - m_t)
    floor_all = jnp.exp(-m_t)

    @pl.when(h == 0)
    def _():
        m_ref[...] = m_t

    def pick(arr):
        return jnp.sum(jnp.where(lane == h, arr, 0.0), axis=1, keepdims=True)

    wi = pick(w_intra_all)
    we = pick(w_inter_all)
    fl = pick(floor_all)
    qscale = M_DQK ** -0.5
    q = q_ref[...] * qscale
    k = k_ref[...]
    v = v_ref[...]
    n0 = n0_ref[...]
    s = jnp.sum(q * k, axis=1, keepdims=True) * wi
    den = we * jnp.sum(q * n0, axis=1, keepdims=True) + s
    dd = jnp.maximum(jnp.abs(den), fl)
    n_ref[...] = we * n0 + wi * k

    shift = lax.rem(LANES - i * TB, LANES)
    qt = pltpu.roll(qt_ref[...], shift, 1) * qscale
    kt = pltpu.roll(kt_ref[...], shift, 1)
    gn = gn_ref[...]
    sig_o = _sigmoid(o_ref[...])
    for j in range(TB):
        qc = qt[:, j:j + 1]
        kc = kt[:, j:j + 1]
        c0 = c0_ref[j]
        vj = v[j:j + 1, :]
        qc0 = jnp.sum(qc * c0, axis=0, keepdims=True)
        hrow = (we[j:j + 1, :] * qc0 + s[j:j + 1, :] * vj) / dd[j:j + 1, :]
        hm_ref[j:j + 1, :] = _rms(hrow, gn) * sig_o[j:j + 1, :]
        c_ref[j] = we[j:j + 1, :] * c0 + (wi[j:j + 1, :] * kc) * vj


def _mlstm_decode(layer, zs, qt, kt, gates_s, bias, m0p, gn, state_c, state_n):
    qb, vb = M_DQK, M_DV
    return pl.pallas_call(
        _mlstm_decode_kernel,
        grid=(MS // TB, M_HEADS),
        in_specs=[
            pl.BlockSpec((TB, qb), lambda i, h: (i, C_QM // qb + h)),
            pl.BlockSpec((TB, qb), lambda i, h: (i, C_KM // qb + h)),
            pl.BlockSpec((TB, vb), lambda i, h: (i, C_VM // vb + h)),
            pl.BlockSpec((TB, vb), lambda i, h: (i, C_OM // vb + h)),
            pl.BlockSpec((None, M_DQK, MS), lambda i, h: (h, 0, 0)),
            pl.BlockSpec((None, M_DQK, MS), lambda i, h: (h, 0, 0)),
            pl.BlockSpec((TB, LANES), lambda i, h: (i, 0)),
            pl.BlockSpec((1, LANES), lambda i, h: (0, 0)),
            pl.BlockSpec((TB, LANES), lambda i, h: (i, 0)),
            pl.BlockSpec((1, vb), lambda i, h: (0, h)),
            pl.BlockSpec((None, TB, None, M_DQK, M_DV), lambda i, h: (layer, i, h, 0, 0)),
            pl.BlockSpec((None, TB, M_DQK), lambda i, h: (layer, i, h)),
        ],
        out_specs=[
            pl.BlockSpec((TB, vb), lambda i, h: (i, h)),
            pl.BlockSpec((TB, None, M_DQK, M_DV), lambda i, h: (i, h, 0, 0)),
            pl.BlockSpec((TB, M_DQK), lambda i, h: (i, h)),
            pl.BlockSpec((TB, LANES), lambda i, h: (i, 0)),
        ],
        out_shape=[
            jax.ShapeDtypeStruct((MS, M_HEADS * M_DV), F32),
            jax.ShapeDtypeStruct((MS, M_HEADS, M_DQK, M_DV), F32),
            jax.ShapeDtypeStruct((MS, M_HEADS * M_DQK), F32),
            jax.ShapeDtypeStruct((MS, LANES), F32),
        ],
        compiler_params=_cparams(("parallel", "arbitrary")),
        name="mlstm_decode",
    )(zs, zs, zs, zs, qt, kt, gates_s, bias, m0p, gn, state_c, state_n)


NB = SEQ // ATT_BLOCK
KVW = KV_HEADS * HEAD_DIM


def _swa_prompt_kernel(q_ref, kc_ref, kp_ref, vc_ref, vp_ref, sink_ref, o_ref):
    nb = pl.program_id(1)
    rows = GROUP * ATT_BLOCK
    t = lax.broadcasted_iota(jnp.int32, (rows, 2 * ATT_BLOCK), 0) & (ATT_BLOCK - 1)
    sidx = lax.broadcasted_iota(jnp.int32, (rows, 2 * ATT_BLOCK), 1)
    rel = t + ATT_BLOCK - sidx
    visible = jnp.logical_and(jnp.logical_and(rel >= 0, rel <= WINDOW),
                              jnp.logical_or(sidx >= ATT_BLOCK, nb > 0))
    q = q_ref[...]
    kk = jnp.concatenate([kp_ref[...], kc_ref[...]], axis=0).astype(BF16)
    vv = jnp.concatenate([vp_ref[...], vc_ref[...]], axis=0).astype(BF16)
    scale = HEAD_DIM ** -0.5
    for g in range(KV_HEADS):
        qg = jnp.concatenate(
            [q[:, (g * GROUP + hh) * HEAD_DIM:(g * GROUP + hh + 1) * HEAD_DIM] for hh in range(GROUP)], axis=0)
        kg = kk[:, g * HEAD_DIM:(g + 1) * HEAD_DIM]
        vg = vv[:, g * HEAD_DIM:(g + 1) * HEAD_DIM]
        sink = jnp.concatenate(
            [jnp.broadcast_to(sink_ref[g * GROUP + hh:g * GROUP + hh + 1, :], (ATT_BLOCK, LANES))
             for hh in range(GROUP)], axis=0)[:, 0:1]
        s = lax.dot_general(qg, kg, (((1,), (1,)), ((), ())), preferred_element_type=F32) * scale
        s = jnp.where(visible, s, -jnp.inf)
        mx = jnp.maximum(jnp.max(s, axis=1, keepdims=True), sink)
        p = jnp.exp(s - mx)
        denom = jnp.sum(p, axis=1, keepdims=True) + jnp.exp(sink - mx)
        o = jnp.dot((p / denom).astype(BF16), vg, preferred_element_type=F32)
        og = jnp.concatenate([o[hh * ATT_BLOCK:(hh + 1) * ATT_BLOCK, :] for hh in range(GROUP)], axis=1)
        o_ref[:, g * GROUP * HEAD_DIM:(g + 1) * GROUP * HEAD_DIM] = og.astype(BF16)


def _swa_prompt(z, kv, sinks_b):
    qw = N_HEADS * HEAD_DIM
    return pl.pallas_call(
        _swa_prompt_kernel,
        grid=(BATCH, NB),
        in_specs=[
            pl.BlockSpec((ATT_BLOCK, qw), lambda b, n: (b * NB + n, C_QA // qw)),
            pl.BlockSpec((ATT_BLOCK, KVW), lambda b, n: (b * NB + n, 0)),
            pl.BlockSpec((ATT_BLOCK, KVW), lambda b, n: (b * NB + jnp.maximum(n - 1, 0), 0)),
            pl.BlockSpec((ATT_BLOCK, KVW), lambda b, n: (b * NB + n, 1)),
            pl.BlockSpec((ATT_BLOCK, KVW), lambda b, n: (b * NB + jnp.maximum(n - 1, 0), 1)),
            pl.BlockSpec((N_HEADS, LANES), lambda b, n: (0, 0)),
        ],
        out_specs=pl.BlockSpec((ATT_BLOCK, qw), lambda b, n: (b * NB + n, 0)),
        out_shape=jax.ShapeDtypeStruct((MP, qw), BF16),
        compiler_params=_cparams(("parallel", "arbitrary")),
        name="swa_prompt",
    )(z, kv, kv, kv, kv, sinks_b)


def _swa_decode_kernel(q_ref, kn_ref, vn_ref, knf_ref, vnf_ref, kb_ref, vb_ref, sink_ref,
                       o_ref, kc_ref, vc_ref):
    scale = HEAD_DIM ** -0.5
    for j in range(TB):
        kbuf = kb_ref[j]
        vbuf = vb_ref[j]
        kc_ref[j, 0:WINDOW - 1, :] = kbuf[1:WINDOW, :]
        kc_ref[j, WINDOW - 1:WINDOW, :] = knf_ref[j:j + 1, :]
        vc_ref[j, 0:WINDOW - 1, :] = vbuf[1:WINDOW, :]
        vc_ref[j, WINDOW - 1:WINDOW, :] = vnf_ref[j:j + 1, :]
        for g in range(KV_HEADS):
            qg = q_ref[j, g * GROUP:(g + 1) * GROUP, :]
            kg = kbuf[:, g * HEAD_DIM:(g + 1) * HEAD_DIM].astype(BF16)
            vg = vbuf[:, g * HEAD_DIM:(g + 1) * HEAD_DIM].astype(BF16)
            s_c = lax.dot_general(qg.astype(BF16), kg, (((1,), (1,)), ((), ())),
                                  preferred_element_type=F32) * scale
            s_n = jnp.sum(qg * kn_ref[j, g:g + 1, :], axis=1, keepdims=True) * scale
            sink = sink_ref[g * GROUP:(g + 1) * GROUP, 0:1]
            mx = jnp.maximum(jnp.maximum(jnp.max(s_c, axis=1, keepdims=True), s_n), sink)
            p_c = jnp.exp(s_c - mx)
            p_n = jnp.exp(s_n - mx)
            denom = jnp.sum(p_c, axis=1, keepdims=True) + p_n + jnp.exp(sink - mx)
            o = jnp.dot((p_c / denom).astype(BF16), vg, preferred_element_type=F32)
            o_ref[j, g * GROUP:(g + 1) * GROUP, :] = o + (p_n / denom) * vn_ref[j, g:g + 1, :]


def _swa_decode(layer, q3, kn3, vn3, knf, vnf, kbuf, vbuf, sinks_b):
    return pl.pallas_call(
        _swa_decode_kernel,
        grid=(MS // TB,),
        in_specs=[
            pl.BlockSpec((TB, N_HEADS, HEAD_DIM), lambda i: (i, 0, 0)),
            pl.BlockSpec((TB, KV_HEADS, HEAD_DIM), lambda i: (i, 0, 0)),
            pl.BlockSpec((TB, KV_HEADS, HEAD_DIM), lambda i: (i, 0, 0)),
            pl.BlockSpec((TB, KVW), lambda i: (i, 0)),
            pl.BlockSpec((TB, KVW), lambda i: (i, 0)),
            pl.BlockSpec((None, TB, WINDOW, KVW), lambda i: (layer, i, 0, 0)),
            pl.BlockSpec((None, TB, WINDOW, KVW), lambda i: (layer, i, 0, 0)),
            pl.BlockSpec((N_HEADS, LANES), lambda i: (0, 0)),
        ],
        out_specs=[
            pl.BlockSpec((TB, N_HEADS, HEAD_DIM), lambda i: (i, 0, 0)),
            pl.BlockSpec((TB, WINDOW, KVW), lambda i: (i, 0, 0)),
            pl.BlockSpec((TB, WINDOW, KVW), lambda i: (i, 0, 0)),
        ],
        out_shape=[
            jax.ShapeDtypeStruct((MS, N_HEADS, HEAD_DIM), F32),
            jax.ShapeDtypeStruct((MS, WINDOW, KVW), F32),
            jax.ShapeDtypeStruct((MS, WINDOW, KVW), F32),
        ],
        compiler_params=_cparams(("parallel",)),
        name="swa_decode",
    )(q3, kn3, vn3, knf, vnf, kbuf, vbuf, sinks_b)


def _merge_kernel(hm_ref, ha_ref, wbm_ref, wba_ref, gm_ref, ga_ref, o_ref):
    a = jnp.dot(hm_ref[...], wbm_ref[...], preferred_element_type=F32)
    b = jnp.dot(ha_ref[...], wba_ref[...], preferred_element_type=F32)
    o_ref[...] = (_sigmoid(gm_ref[...].astype(F32)) * a + _sigmoid(ga_ref[...].astype(F32)) * b).astype(BF16)


def _merge(hm, ha, wbm, wba, z):
    return pl.pallas_call(
        _merge_kernel,
        grid=(M // TM, D_MODEL // TN),
        in_specs=[
            pl.BlockSpec((TM, D_MODEL), lambda i, j: (i, 0)),
            pl.BlockSpec((TM, D_MODEL), lambda i, j: (i, 0)),
            pl.BlockSpec((D_MODEL, TN), lambda i, j: (0, j)),
            pl.BlockSpec((D_MODEL, TN), lambda i, j: (0, j)),
            pl.BlockSpec((TM, TN), lambda i, j: (i, C_GM // TN + j)),
            pl.BlockSpec((TM, TN), lambda i, j: (i, C_GA // TN + j)),
        ],
        out_specs=pl.BlockSpec((TM, TN), lambda i, j: (i, j)),
        out_shape=jax.ShapeDtypeStruct((M, D_MODEL), BF16),
        compiler_params=_cparams(("parallel", "arbitrary")),
        name="merge",
    )(hm, ha, wbm, wba, z, z)


def _mm_res_kernel(a_ref, w_ref, r_ref, o_ref):
    o_ref[...] = r_ref[...] + jnp.dot(a_ref[...], w_ref[...], preferred_element_type=F32)


def _mm_res(a, w, res, tm, tn, name):
    kdim = a.shape[1]
    n = w.shape[1]
    return pl.pallas_call(
        _mm_res_kernel,
        grid=(M // tm, n // tn),
        in_specs=[
            pl.BlockSpec((tm, kdim), lambda i, j: (i, 0)),
            pl.BlockSpec((kdim, tn), lambda i, j: (0, j)),
            pl.BlockSpec((tm, tn), lambda i, j: (i, j)),
        ],
        out_specs=pl.BlockSpec((tm, tn), lambda i, j: (i, j)),
        out_shape=jax.ShapeDtypeStruct((M, n), F32),
        compiler_params=_cparams(("parallel", "arbitrary")),
        name=name,
    )(a, w, res)


def _ffn_up_kernel(x_ref, g_ref, wg_ref, wu_ref, act_ref, h_scr):
    @pl.when(pl.program_id(1) == 0)
    def _():
        h_scr[...] = _rms(x_ref[...], g_ref[...]).astype(BF16)

    h = h_scr[...]
    a = jnp.dot(h, wg_ref[...], preferred_element_type=F32)
    u = jnp.dot(h, wu_ref[...], preferred_element_type=F32)
    act_ref[...] = ((a * _sigmoid(a)) * u).astype(BF16)


def _ffn_up(x, g, wg, wu):
    return pl.pallas_call(
        _ffn_up_kernel,
        grid=(M // TM, D_FF // TN),
        in_specs=[
            pl.BlockSpec((TM, D_MODEL), lambda i, j: (i, 0)),
            pl.BlockSpec((1, D_MODEL), lambda i, j: (0, 0)),
            pl.BlockSpec((D_MODEL, TN), lambda i, j: (0, j)),
            pl.BlockSpec((D_MODEL, TN), lambda i, j: (0, j)),
        ],
        out_specs=pl.BlockSpec((TM, TN), lambda i, j: (i, j)),
        out_shape=jax.ShapeDtypeStruct((M, D_FF), BF16),
        scratch_shapes=[pltpu.VMEM((TM, D_MODEL), BF16)],
        compiler_params=_cparams(("parallel", "arbitrary")),
        name="ffn_up",
    )(x, g, wg, wu)


def _router_kernel(x_ref, g_ref, wr_ref, h_ref, ids_ref, gates_ref):
    h = _rms(x_ref[...], g_ref[...])
    h_ref[...] = h.astype(BF16)
    logits = jnp.dot(h, wr_ref[...], preferred_element_type=F32, precision=lax.Precision.HIGHEST)
    lane = lax.broadcasted_iota(jnp.int32, logits.shape, 1)
    lg = jnp.where(lane < N_EXPERTS, logits, -jnp.inf)
    v1 = jnp.max(lg, axis=1, keepdims=True)
    i1 = jnp.min(jnp.where(lg == v1, lane, LANES), axis=1, keepdims=True)
    lg2 = jnp.where(lane == i1, -jnp.inf, lg)
    v2 = jnp.max(lg2, axis=1, keepdims=True)
    i2 = jnp.min(jnp.where(lg2 == v2, lane, LANES), axis=1, keepdims=True)
    e2 = jnp.exp(v2 - v1)
    g1 = 1.0 / (1.0 + e2)
    g2 = e2 / (1.0 + e2)
    ids_ref[...] = jnp.where(lane == 0, i1, jnp.where(lane == 1, i2, 0))
    gates_ref[...] = jnp.where(lane == 0, g1, jnp.where(lane == 1, g2, 0.0))


def _router(x, g, wr_pad):
    tm = TM
    return pl.pallas_call(
        _router_kernel,
        grid=(M // tm,),
        in_specs=[
            pl.BlockSpec((tm, D_MODEL), lambda i: (i, 0)),
            pl.BlockSpec((1, D_MODEL), lambda i: (0, 0)),
            pl.BlockSpec((D_MODEL, LANES), lambda i: (0, 0)),
        ],
        out_specs=[
            pl.BlockSpec((tm, D_MODEL), lambda i: (i, 0)),
            pl.BlockSpec((tm, LANES), lambda i: (i, 0)),
            pl.BlockSpec((tm, LANES), lambda i: (i, 0)),
        ],
        out_shape=[
            jax.ShapeDtypeStruct((M, D_MODEL), BF16),
            jax.ShapeDtypeStruct((M, LANES), jnp.int32),
            jax.ShapeDtypeStruct((M, LANES), F32),
        ],
        compiler_params=_cparams(("parallel",)),
        name="router",
    )(x, g, wr_pad)


def _row_gather_kernel(idx_ref, src_ref, dst_ref, sems, *, n_rows):
    n_chunks = n_rows // GATHER_CHUNK

    def copy(r, slot):
        return pltpu.make_async_copy(src_ref.at[idx_ref[r]], dst_ref.at[r], sems.at[slot])

    def start_chunk(c, slot):
        def body(r, carry):
            copy(c * GATHER_CHUNK + r, slot).start()
            return carry
        lax.fori_loop(0, GATHER_CHUNK, body, 0)

    def wait_chunk(c, slot):
        def body(r, carry):
            copy(c * GATHER_CHUNK + r, slot).wait()
            return carry
        lax.fori_loop(0, GATHER_CHUNK, body, 0)

    start_chunk(0, 0)

    def outer(c, carry):
        slot = lax.rem(c, 2)

        @pl.when(c + 1 < n_chunks)
        def _():
            start_chunk(c + 1, 1 - slot)

        wait_chunk(c, slot)
        return carry

    lax.fori_loop(0, n_chunks, outer, 0)


def _row_gather(idx, src3, n_rows, name):
    assert n_rows % GATHER_CHUNK == 0
    return pl.pallas_call(
        functools.partial(_row_gather_kernel, n_rows=n_rows),
        grid_spec=pltpu.PrefetchScalarGridSpec(
            num_scalar_prefetch=1,
            grid=(1,),
            in_specs=[pl.BlockSpec(memory_space=pl.ANY)],
            out_specs=pl.BlockSpec(memory_space=pl.ANY),
            scratch_shapes=[pltpu.SemaphoreType.DMA((2,))],
        ),
        out_shape=jax.ShapeDtypeStruct((n_rows,) + src3.shape[1:], src3.dtype),
        compiler_params=pltpu.CompilerParams(dimension_semantics=("arbitrary",)),
        name=name,
    )(idx, src3)


def _moe_up_kernel(te_ref, nu_ref, x_ref, wg_ref, wu_ref, act_ref, wg_scr, wu_scr):
    m = pl.program_id(1)

    @pl.when(m < nu_ref[0])
    def _():
        prev = te_ref[jnp.maximum(m - 1, 0)]

        @pl.when(jnp.logical_or(m == 0, te_ref[m] != prev))
        def _():
            wg_scr[...] = wg_ref[...].astype(BF16)
            wu_scr[...] = wu_ref[...].astype(BF16)

        x = x_ref[...]
        a = jnp.dot(x, wg_scr[...], preferred_element_type=F32)
        u = jnp.dot(x, wu_scr[...], preferred_element_type=F32)
        act_ref[...] = ((a * _sigmoid(a)) * u).astype(BF16)

    @pl.when(m >= nu_ref[0])
    def _():
        act_ref[...] = jnp.zeros_like(act_ref)


def _moe_up(tile_expert, n_used, xs, wg, wu):
    return pl.pallas_call(
        _moe_up_kernel,
        grid_spec=pltpu.PrefetchScalarGridSpec(
            num_scalar_prefetch=2,
            grid=(E_FF // MOE_TF, N_SLOT_TILES),
            in_specs=[
                pl.BlockSpec((MOE_TM, D_MODEL), lambda j, m, te, nu: (m, 0)),
                pl.BlockSpec((None, D_MODEL, MOE_TF), lambda j, m, te, nu: (te[m], 0, j)),
                pl.BlockSpec((None, D_MODEL, MOE_TF), lambda j, m, te, nu: (te[m], 0, j)),
            ],
            out_specs=pl.BlockSpec((MOE_TM, MOE_TF), lambda j, m, te, nu: (m, j)),
            scratch_shapes=[pltpu.VMEM((D_MODEL, MOE_TF), BF16), pltpu.VMEM((D_MODEL, MOE_TF), BF16)],
        ),
        out_shape=jax.ShapeDtypeStruct((N_SLOTS, E_FF), BF16),
        compiler_params=_cparams(("arbitrary", "arbitrary")),
        name="moe_up",
    )(tile_expert, n_used, xs, wg, wu)


def _moe_down_kernel(te_ref, nu_ref, a_ref, wd_ref, sg_ref, y_ref, wd_scr):
    m = pl.program_id(1)

    @pl.when(m < nu_ref[0])
    def _():
        prev = te_ref[jnp.maximum(m - 1, 0)]

        @pl.when(jnp.logical_or(m == 0, te_ref[m] != prev))
        def _():
            wd_scr[...] = wd_ref[...].astype(BF16)

        y_ref[...] = sg_ref[...] * jnp.dot(a_ref[...], wd_scr[...], preferred_element_type=F32)

    @pl.when(m >= nu_ref[0])
    def _():
        y_ref[...] = jnp.zeros_like(y_ref)


def _moe_down(tile_expert, n_used, act, wd, slot_gate):
    return pl.pallas_call(
        _moe_down_kernel,
        grid_spec=pltpu.PrefetchScalarGridSpec(
            num_scalar_prefetch=2,
            grid=(D_MODEL // MOE_TN, N_SLOT_TILES),
            in_specs=[
                pl.BlockSpec((MOE_TM, E_FF), lambda j, m, te, nu: (m, 0)),
                pl.BlockSpec((None, E_FF, MOE_TN), lambda j, m, te, nu: (te[m], 0, j)),
                pl.BlockSpec((MOE_TM, 1), lambda j, m, te, nu: (m, 0)),
            ],
            out_specs=pl.BlockSpec((MOE_TM, MOE_TN), lambda j, m, te, nu: (m, j)),
            scratch_shapes=[pltpu.VMEM((E_FF, MOE_TN), BF16)],
        ),
        out_shape=jax.ShapeDtypeStruct((N_SLOTS, D_MODEL), F32),
        compiler_params=_cparams(("arbitrary", "arbitrary")),
        name="moe_down",
    )(tile_expert, n_used, act, wd, slot_gate)


def _final_norm_kernel(x_ref, y1_ref, y2_ref, g_ref, o_ref):
    o_ref[...] = _rms(x_ref[...] + y1_ref[...] + y2_ref[...], g_ref[...])


def _combine_norm(x, y1, y2, g):
    tm = TM // 2
    spec = pl.BlockSpec((tm, D_MODEL), lambda i: (i, 0))
    return pl.pallas_call(
        _final_norm_kernel,
        grid=(M // tm,),
        in_specs=[spec, spec, spec, pl.BlockSpec((1, D_MODEL), lambda i: (0, 0))],
        out_specs=spec,
        out_shape=jax.ShapeDtypeStruct((M, D_MODEL), F32),
        compiler_params=_cparams(("parallel",)),
        name="combine_norm",
    )(x, y1, y2, g)


def _route(ids, gates):
    e_flat = jnp.concatenate([ids[:, 0], ids[:, 1]])
    g_flat = jnp.concatenate([gates[:, 0], gates[:, 1]])
    onehot = (e_flat[:, None] == jnp.arange(N_EXPERTS, dtype=jnp.int32)[None, :]).astype(jnp.int32)
    counts = jnp.sum(onehot, axis=0)
    rank = jnp.sum((jnp.cumsum(onehot, axis=0) - onehot) * onehot, axis=1)
    padded = ((counts + MOE_TM - 1) // MOE_TM) * MOE_TM
    ends = jnp.cumsum(padded)
    starts = ends - padded
    pos = starts[e_flat] + rank
    tok = jnp.tile(jnp.arange(M, dtype=jnp.int32), TOP_K)
    slot_token = jnp.zeros((N_SLOTS,), jnp.int32).at[pos].set(tok)
    slot_gate = jnp.zeros((N_SLOTS,), F32).at[pos].set(g_flat)
    n_used = (ends[-1] // MOE_TM).astype(jnp.int32)
    tile_start = jnp.arange(N_SLOT_TILES, dtype=jnp.int32) * MOE_TM
    tile_expert = jnp.sum((tile_start[:, None] >= ends[None, :]).astype(jnp.int32), axis=1)
    last = jnp.sum((jnp.maximum(n_used - 1, 0) * MOE_TM >= ends).astype(jnp.int32))
    tile_expert = jnp.where(jnp.arange(N_SLOT_TILES) < n_used, tile_expert, last).astype(jnp.int32)
    return slot_token, slot_gate[:, None], tile_expert, n_used.reshape(1), pos[:M], pos[M:]


def _rope_tables():
    half = ROT_DIM // 2
    pos = jnp.concatenate([jnp.tile(jnp.arange(SEQ, dtype=jnp.int32), BATCH),
                           jnp.full((MS,), PAST_LEN, jnp.int32)])
    inv = jnp.power(ROPE_THETA, -jnp.arange(half, dtype=F32) / half)
    ang = pos.astype(F32)[:, None] * inv[None, :]
    cos, sin = jnp.cos(ang), jnp.sin(ang)
    pad = HEAD_DIM - ROT_DIM
    one = jnp.ones((M, pad), F32)
    zero = jnp.zeros((M, pad), F32)
    zh = jnp.zeros((M, half), F32)
    cs = jnp.concatenate([cos, cos, one], axis=1)
    sa = jnp.concatenate([-sin, zh, zero], axis=1)
    sb = jnp.concatenate([zh, sin, zero], axis=1)
    rep = LANES // HEAD_DIM
    return jnp.tile(cs, (1, rep)), jnp.tile(sa, (1, rep)), jnp.tile(sb, (1, rep))


def _pad_lanes(a):
    return jnp.pad(a, ((0, 0), (0, LANES - a.shape[1])))


def kernel(x_prompt, x_sample, state_mlstm_C, state_mlstm_n, state_mlstm_m, cache_swa_k, cache_swa_v, norm_mix_g, w_in, b_igate, b_fgate, mlstm_norm_g, attn_sinks, w_branch_m, w_branch_a, w_out, norm_ffn_g, w_gate_dense, w_up_dense, w_down_dense, w_router, w_gate_moe, w_up_moe, w_down_moe, norm_final_g):
    x = jnp.concatenate([x_prompt.reshape(MP, D_MODEL), x_sample.reshape(MS, D_MODEL)], axis=0)
    cs, sa, sb = _rope_tables()
    state_n = state_mlstm_n.reshape(DEPTH, MS, M_HEADS * M_DQK)
    kbuf = cache_swa_k.reshape(DEPTH, MS, WINDOW, KVW)
    vbuf = cache_swa_v.reshape(DEPTH, MS, WINDOW, KVW)
    outs = {name: [] for name in ("Cp", "np", "mp", "kp", "vp", "Cs", "ns", "ms", "ks", "vs")}
    y_final = None

    for l in range(DEPTH):
        w_l = w_in[l]
        w_al = jnp.concatenate([w_l[:, :R_GATES], w_l[:, R_QA:]], axis=1).astype(BF16)
        w_gate = _pad_lanes(w_l[:, R_GATES:R_QA]).astype(BF16)
        bias = _pad_lanes(jnp.concatenate([b_igate[l], b_fgate[l]])[None, :])
        gn = mlstm_norm_g[l][None, :]
        sinks_b = jnp.broadcast_to(attn_sinks[l][:, None], (N_HEADS, LANES))

        z, kv, gates = _inproj(x, norm_mix_g[l][None, :], w_al, w_gate, cs, sa, sb)

        hm_p, c_p, n_p, m_p = _mlstm_prompt(z, gates, bias, gn)
        ha_p = _swa_prompt(z, kv, sinks_b)
        kv_p = kv[:MP].reshape(BATCH, SEQ, 2, KV_HEADS, HEAD_DIM)[:, SEQ - WINDOW:]
        outs["Cp"].append(c_p)
        outs["np"].append(n_p.reshape(BATCH, M_HEADS, M_DQK))
        outs["mp"].append(m_p[:, :, 0, 0])
        outs["kp"].append(kv_p[:, :, 0])
        outs["vp"].append(kv_p[:, :, 1])

        zs = z[MP:].astype(F32)
        kv_s = kv[MP:]
        qt = zs[:, C_QM:C_KM].T.reshape(M_HEADS, M_DQK, MS)
        kt = zs[:, C_KM:C_VM].T.reshape(M_HEADS, M_DQK, MS)
        m0p = _pad_lanes(state_mlstm_m[l])
        hm_s, c_s, n_s, m_s = _mlstm_decode(l, zs, qt, kt, gates[MP:], bias, m0p, gn, state_mlstm_C, state_n)
        kn = kv_s[:, :KVW]
        vn = kv_s[:, KVW:]
        ha_s, k_s, v_s = _swa_decode(
            l, zs[:, C_QA:C_KA].reshape(MS, N_HEADS, HEAD_DIM), kn.reshape(MS, KV_HEADS, HEAD_DIM),
            vn.reshape(MS, KV_HEADS, HEAD_DIM), kn, vn, kbuf, vbuf, sinks_b)
        outs["Cs"].append(c_s)
        outs["ns"].append(n_s.reshape(MS, M_HEADS, M_DQK))
        outs["ms"].append(m_s[:, :M_HEADS])
        outs["ks"].append(k_s.reshape(MS, WINDOW, KV_HEADS, HEAD_DIM))
        outs["vs"].append(v_s.reshape(MS, WINDOW, KV_HEADS, HEAD_DIM))

        hm = jnp.concatenate([hm_p, hm_s.astype(BF16)], axis=0)
        ha = jnp.concatenate([ha_p, ha_s.reshape(MS, N_HEADS * HEAD_DIM).astype(BF16)], axis=0)
        merged = _merge(hm, ha, w_branch_m[l].astype(BF16), w_branch_a[l].astype(BF16), z)
        x = _mm_res(merged, w_out[l].astype(BF16), x, TM, TN, "out_proj")

        if l % 2 == 0:
            jd = l // 2
            act = _ffn_up(x, norm_ffn_g[l][None, :], w_gate_dense[jd].astype(BF16), w_up_dense[jd].astype(BF16))
            x = _mm_res(act, w_down_dense[jd].astype(BF16), x, TM, TN, "ffn_down")
        else:
            jm = l // 2
            h, ids, gts = _router(x, norm_ffn_g[l][None, :], _pad_lanes(w_router[jm]))
            slot_token, slot_gate, tile_expert, n_used, pos1, pos2 = _route(ids, gts)
            xs = _row_gather(slot_token, h.reshape(M, D_MODEL // LANES, LANES), N_SLOTS, "slot_gather")
            act = _moe_up(tile_expert, n_used, xs.reshape(N_SLOTS, D_MODEL), w_gate_moe[jm], w_up_moe[jm])
            y = _moe_down(tile_expert, n_used, act, w_down_moe[jm], slot_gate)
            y3 = y.reshape(N_SLOTS, D_MODEL // LANES, LANES)
            m_pad = ((M + GATHER_CHUNK - 1) // GATHER_CHUNK) * GATHER_CHUNK
            pad = lambda p: jnp.pad(p, (0, m_pad - M))
            y1 = _row_gather(pad(pos1), y3, m_pad, "combine_gather_1").reshape(m_pad, D_MODEL)
            y2 = _row_gather(pad(pos2), y3, m_pad, "combine_gather_2").reshape(m_pad, D_MODEL)
            if l == DEPTH - 1:
                y_final = _combine_norm(x, y1, y2, norm_final_g[None, :])
            else:
                x = x + y1[:M] + y2[:M]

    if y_final is None:
        y_final = _combine_norm(x, jnp.zeros_like(x), jnp.zeros_like(x), norm_final_g[None, :])
    y_prompt = y_final[:MP].reshape(BATCH, SEQ, D_MODEL)
    y_sample = y_final[MP:].reshape(MS, 1, D_MODEL)
    st = lambda name: jnp.stack(outs[name])
    return (y_prompt, y_sample, st("Cp"), st("np"), st("mp"), st("kp"), st("vp"),
            st("Cs"), st("ns"), st("ms"), st("ks"), st("vs"))
```

```python
import functools

import jax
import jax.numpy as jnp
import numpy as np
from jax import lax
from jax.experimental import pallas as pl
from jax.experimental.pallas import tpu as pltpu

F32 = jnp.float32
BF16 = jnp.bfloat16

D_MODEL = 2048
BATCH = 4
SEQ = 2048
DEPTH = 2
DEC_BATCH = 128
PAST_LEN = 8192
M_HEADS = 4
M_DQK = 256
M_DV = 512
N_HEADS = 32
KV_HEADS = 4
HEAD_DIM = 64
GROUP = N_HEADS // KV_HEADS
ROT_DIM = HEAD_DIM // 4
ROPE_THETA = 500000.0
WINDOW = 128
ATT_BLOCK = 128
D_FF = 5632
N_EXPERTS = 8
TOP_K = 2
E_FF = 7168
EPS = 1e-6

MP = BATCH * SEQ
MS = DEC_BATCH
M = MP + MS

C_QM = 0
C_KM = C_QM + M_HEADS * M_DQK
C_VM = C_KM + M_HEADS * M_DQK
C_OM = C_VM + M_HEADS * M_DV
C_QA = C_OM + M_HEADS * M_DV
C_KA = C_QA + N_HEADS * HEAD_DIM
C_VA = C_KA + KV_HEADS * HEAD_DIM
C_GM = C_VA + KV_HEADS * HEAD_DIM
C_GA = C_GM + D_MODEL
N_Z = C_GA + D_MODEL
R_GATES = 2 * M_HEADS * M_DQK + 2 * M_HEADS * M_DV
R_QA = R_GATES + 2 * M_HEADS

LANES = 128
TM = 1040
TN = 512
ML = 256
TB = 8
MOE_TM = 256
MOE_TF = 512
MOE_TN = 512
GATHER_ROWS = 2 * MOE_TM
COMBINE_ROWS = TM // 2
N_SLOT_TILES = -(-(TOP_K * M + N_EXPERTS * (MOE_TM - 1)) // GATHER_ROWS) * (GATHER_ROWS // MOE_TM)
N_SLOTS = N_SLOT_TILES * MOE_TM
VMEM_LIMIT = 56 * 1024 * 1024


def _cparams(sem, vmem=VMEM_LIMIT):
    return pltpu.CompilerParams(dimension_semantics=sem, vmem_limit_bytes=vmem)


def _rms(x, g):
    ms = jnp.mean(x * x, axis=-1, keepdims=True)
    return (x * lax.rsqrt(ms + EPS)) * g


def _sigmoid(x):
    return 1.0 / (1.0 + jnp.exp(-x))


def _log_sigmoid(x):
    return jnp.minimum(x, 0.0) - jnp.log(1.0 + jnp.exp(-jnp.abs(x)))


J_QA0 = C_QA // TN
J_KV = C_KA // TN
assert C_QA % TN == 0 and C_KA % TN == 0 and (C_VA - C_KA) * 2 == TN and C_GM == C_KA + TN


def _rope(acc, cs, sa, sb):
    n = acc.shape[1]
    return acc * cs + pltpu.roll(acc, n - ROT_DIM // 2, 1) * sa + pltpu.roll(acc, ROT_DIM // 2, 1) * sb


def _inproj_kernel(x_ref, g_ref, w_ref, wgate_ref, cs_ref, sa_ref, sb_ref, z_ref, kv_ref, gt_ref, h_scr):
    j = pl.program_id(1)

    @pl.when(j == 0)
    def _():
        h = _rms(x_ref[...], g_ref[...]).astype(BF16)
        h_scr[...] = h
        gt_ref[...] = jnp.dot(h, wgate_ref[...], preferred_element_type=F32)

    acc = jnp.dot(h_scr[...], w_ref[...], preferred_element_type=F32)
    reps = TN // LANES
    is_q = jnp.logical_and(j >= J_QA0, j < J_KV)
    is_kv = j == J_KV

    @pl.when(is_q)
    def _():
        cs = jnp.tile(cs_ref[...], (1, reps))
        sa = jnp.tile(sa_ref[...], (1, reps))
        sb = jnp.tile(sb_ref[...], (1, reps))
        z_ref[...] = _rope(acc, cs, sa, sb).astype(BF16)

    @pl.when(is_kv)
    def _():
        is_k = lax.broadcasted_iota(jnp.int32, acc.shape, 1) < (C_VA - C_KA)
        cs = jnp.where(is_k, jnp.tile(cs_ref[...], (1, reps)), 1.0)
        sa = jnp.where(is_k, jnp.tile(sa_ref[...], (1, reps)), 0.0)
        sb = jnp.where(is_k, jnp.tile(sb_ref[...], (1, reps)), 0.0)
        r = _rope(acc, cs, sa, sb)
        kv_ref[...] = r
        z_ref[...] = r.astype(BF16)

    @pl.when(jnp.logical_not(jnp.logical_or(is_q, is_kv)))
    def _():
        z_ref[...] = acc.astype(BF16)


def _inproj(x, g, w_al, w_gate, cs, sa, sb):
    return pl.pallas_call(
        _inproj_kernel,
        grid=(M // TM, N_Z // TN),
        in_specs=[
            pl.BlockSpec((TM, D_MODEL), lambda i, j: (i, 0)),
            pl.BlockSpec((1, D_MODEL), lambda i, j: (0, 0)),
            pl.BlockSpec((D_MODEL, TN), lambda i, j: (0, j)),
            pl.BlockSpec((D_MODEL, LANES), lambda i, j: (0, 0)),
            pl.BlockSpec((TM, LANES), lambda i, j: (i, 0)),
            pl.BlockSpec((TM, LANES), lambda i, j: (i, 0)),
            pl.BlockSpec((TM, LANES), lambda i, j: (i, 0)),
        ],
        out_specs=[
            pl.BlockSpec((TM, TN), lambda i, j: (i, j)),
            pl.BlockSpec((TM, TN), lambda i, j: (i, 0)),
            pl.BlockSpec((TM, LANES), lambda i, j: (i, 0)),
        ],
        out_shape=[
            jax.ShapeDtypeStruct((M, N_Z), BF16),
            jax.ShapeDtypeStruct((M, TN), F32),
            jax.ShapeDtypeStruct((M, LANES), F32),
        ],
        scratch_shapes=[pltpu.VMEM((TM, D_MODEL), BF16)],
        compiler_params=_cparams(("parallel", "arbitrary")),
        name="inproj",
    )(x, g, w_al, w_gate, cs, sa, sb)


NC = SEQ // ML


def _mlstm_prompt_kernel(q_ref, k_ref, v_ref, o_ref, gt_ref, bias_ref, gn_ref,
                         hm_ref, c_ref, n_ref, m_ref, c_scr, n_scr, m_scr):
    h = pl.program_id(1)
    c = pl.program_id(2)

    @pl.when(c == 0)
    def _():
        c_scr[...] = jnp.zeros_like(c_scr)
        n_scr[...] = jnp.zeros_like(n_scr)
        m_scr[...] = jnp.zeros_like(m_scr)

    gates = gt_ref[...] + bias_ref[...]
    lane = lax.broadcasted_iota(jnp.int32, gates.shape, 1)
    i_msk = jnp.where(lane == h, gates, 0.0)
    lf_msk = jnp.where(lane == h + M_HEADS, _log_sigmoid(gates), 0.0)
    i_col = jnp.sum(i_msk, axis=1, keepdims=True)
    row = lax.broadcasted_iota(jnp.int32, (ML, ML), 0)
    col = lax.broadcasted_iota(jnp.int32, (ML, ML), 1)
    causal = col <= row
    b_full = jnp.dot(causal.astype(F32), lf_msk, preferred_element_type=F32,
                     precision=lax.Precision.HIGHEST)
    b_col = jnp.sum(b_full, axis=1, keepdims=True)
    both = jnp.where(lane == 0, b_col, jnp.where(lane == 1, i_col, 0.0))
    both_t = both.T
    b_row = both_t[0:1, :]
    i_row = both_t[1:2, :]

    m_prev = m_scr[...]
    a_col = b_col + m_prev
    dmat = jnp.where(causal, b_col - b_row + i_row, -jnp.inf)
    m_t = jnp.maximum(a_col, jnp.max(dmat, axis=1, keepdims=True))
    w_intra = jnp.exp(dmat - m_t)
    w_inter = jnp.exp(a_col - m_t)

    q = q_ref[...]
    k = k_ref[...]
    v = v_ref[...]
    qscale = M_DQK ** -0.5
    s = lax.dot_general(q, k, (((1,), (1,)), ((), ())), preferred_element_type=F32) * qscale * w_intra
    qf = q.astype(F32) * qscale
    inter = jnp.dot(q, c_scr[...].astype(BF16), preferred_element_type=F32) * qscale
    num = w_inter * inter + jnp.dot(s.astype(BF16), v, preferred_element_type=F32)
    den = w_inter * jnp.sum(qf * n_scr[...], axis=1, keepdims=True) + jnp.sum(s, axis=1, keepdims=True)
    hh = num / jnp.maximum(jnp.abs(den), jnp.exp(-m_t))
    y = _rms(hh, gn_ref[...]) * _sigmoid(o_ref[...].astype(F32))
    hm_ref[...] = y.astype(BF16)

    m_new = m_t[ML - 1:ML, :]
    b_last = b_col[ML - 1:ML, :]
    w_state = jnp.exp(b_last - b_col + i_col - m_new)
    decay = jnp.exp(b_last + m_prev - m_new)
    kw = k.astype(F32) * w_state
    c_new = decay * c_scr[...] + lax.dot_general(kw.astype(BF16), v, (((0,), (0,)), ((), ())),
                                                 preferred_element_type=F32)
    n_new = decay * n_scr[...] + jnp.sum(kw, axis=0, keepdims=True)
    c_scr[...] = c_new
    n_scr[...] = n_new
    m_scr[...] = m_new

    @pl.when(c == NC - 1)
    def _():
        c_ref[...] = c_new
        n_ref[...] = n_new
        m_ref[...] = jnp.broadcast_to(m_new, m_ref.shape)


def _mlstm_prompt(z, gates, bias, gn):
    qb, vb = M_DQK, M_DV
    return pl.pallas_call(
        _mlstm_prompt_kernel,
        grid=(BATCH, M_HEADS, NC),
        in_specs=[
            pl.BlockSpec((ML, qb), lambda b, h, c: (b * NC + c, C_QM // qb + h)),
            pl.BlockSpec((ML, qb), lambda b, h, c: (b * NC + c, C_KM // qb + h)),
            pl.BlockSpec((ML, vb), lambda b, h, c: (b * NC + c, C_VM // vb + h)),
            pl.BlockSpec((ML, vb), lambda b, h, c: (b * NC + c, C_OM // vb + h)),
            pl.BlockSpec((ML, LANES), lambda b, h, c: (b * NC + c, 0)),
            pl.BlockSpec((1, LANES), lambda b, h, c: (0, 0)),
            pl.BlockSpec((1, vb), lambda b, h, c: (0, h)),
        ],
        out_specs=[
            pl.BlockSpec((ML, vb), lambda b, h, c: (b * NC + c, h)),
            pl.BlockSpec((None, None, M_DQK, M_DV), lambda b, h, c: (b, h, 0, 0)),
            pl.BlockSpec((None, None, 1, M_DQK), lambda b, h, c: (b, h, 0, 0)),
            pl.BlockSpec((None, None, 1, LANES), lambda b, h, c: (b, h, 0, 0)),
        ],
        out_shape=[
            jax.ShapeDtypeStruct((MP, M_HEADS * M_DV), BF16),
            jax.ShapeDtypeStruct((BATCH, M_HEADS, M_DQK, M_DV), F32),
            jax.ShapeDtypeStruct((BATCH, M_HEADS, 1, M_DQK), F32),
            jax.ShapeDtypeStruct((BATCH, M_HEADS, 1, LANES), F32),
        ],
        scratch_shapes=[pltpu.VMEM((M_DQK, M_DV), F32), pltpu.VMEM((1, M_DQK), F32), pltpu.VMEM((1, 1), F32)],
        compiler_params=_cparams(("parallel", "parallel", "arbitrary")),
        name="mlstm_prompt",
    )(z, z, z, z, gates, bias, gn)


def _mlstm_decode_kernel(q_ref, k_ref, v_ref, o_ref, qt_ref, kt_ref, gt_ref, bias_ref, m0_ref, gn_ref,
                         c0_ref, n0_ref, hm_ref, c_ref, n_ref, m_ref):
    i = pl.program_id(0)
    h = pl.program_id(1)
    gates = gt_ref[...] + bias_ref[...]
    lane = lax.broadcasted_iota(jnp.int32, gates.shape, 1)
    log_f = pltpu.roll(_log_sigmoid(gates), LANES - M_HEADS, 1)
    a = log_f + m0_ref[...]
    m_t = jnp.maximum(a, gates)
    w_intra_all = jnp.exp(gates - m_t)
    w_inter_all = jnp.exp(a - m_t)
    floor_all = jnp.exp(-m_t)

    @pl.when(h == 0)
    def _():
        m_ref[...] = m_t

    def pick(arr):
        return jnp.sum(jnp.where(lane == h, arr, 0.0), axis=1, keepdims=True)

    wi = pick(w_intra_all)
    we = pick(w_inter_all)
    fl = pick(floor_all)
    qscale = M_DQK ** -0.5
    q = q_ref[...] * qscale
    k = k_ref[...]
    v = v_ref[...]
    n0 = n0_ref[...]
    s = jnp.sum(q * k, axis=1, keepdims=True) * wi
    den = we * jnp.sum(q * n0, axis=1, keepdims=True) + s
    dd = jnp.maximum(jnp.abs(den), fl)
    n_ref[...] = we * n0 + wi * k

    shift = lax.rem(LANES - i * TB, LANES)
    qt = pltpu.roll(qt_ref[...], shift, 1) * qscale
    kt = pltpu.roll(kt_ref[...], shift, 1)
    gn = gn_ref[...]
    sig_o = _sigmoid(o_ref[...])
    for j in range(TB):
        qc = qt[:, j:j + 1]
        kc = kt[:, j:j + 1]
        c0 = c0_ref[j]
        vj = v[j:j + 1, :]
        qc0 = jnp.sum(qc * c0, axis=0, keepdims=True)
        hrow = (we[j:j + 1, :] * qc0 + s[j:j + 1, :] * vj) / dd[j:j + 1, :]
        hm_ref[j:j + 1, :] = _rms(hrow, gn) * sig_o[j:j + 1, :]
        c_ref[j] = we[j:j + 1, :] * c0 + (wi[j:j + 1, :] * kc) * vj


def _mlstm_decode(layer, zs, qt, kt, gates_s, bias, m0p, gn, state_c, state_n):
    qb, vb = M_DQK, M_DV
    return pl.pallas_call(
        _mlstm_decode_kernel,
        grid=(MS // TB, M_HEADS),
        in_specs=[
            pl.BlockSpec((TB, qb), lambda i, h: (i, C_QM // qb + h)),
            pl.BlockSpec((TB, qb), lambda i, h: (i, C_KM // qb + h)),
            pl.BlockSpec((TB, vb), lambda i, h: (i, C_VM // vb + h)),
            pl.BlockSpec((TB, vb), lambda i, h: (i, C_OM // vb + h)),
            pl.BlockSpec((None, M_DQK, MS), lambda i, h: (h, 0, 0)),
            pl.BlockSpec((None, M_DQK, MS), lambda i, h: (h, 0, 0)),
            pl.BlockSpec((TB, LANES), lambda i, h: (i, 0)),
            pl.BlockSpec((1, LANES), lambda i, h: (0, 0)),
            pl.BlockSpec((TB, LANES), lambda i, h: (i, 0)),
            pl.BlockSpec((1, vb), lambda i, h: (0, h)),
            pl.BlockSpec((None, TB, None, M_DQK, M_DV), lambda i, h: (layer, i, h, 0, 0)),
            pl.BlockSpec((None, TB, M_DQK), lambda i, h: (layer, i, h)),
        ],
        out_specs=[
            pl.BlockSpec((TB, vb), lambda i, h: (i, h)),
            pl.BlockSpec((TB, None, M_DQK, M_DV), lambda i, h: (i, h, 0, 0)),
            pl.BlockSpec((TB, M_DQK), lambda i, h: (i, h)),
            pl.BlockSpec((TB, LANES), lambda i, h: (i, 0)),
        ],
        out_shape=[
            jax.ShapeDtypeStruct((MS, M_HEADS * M_DV), F32),
            jax.ShapeDtypeStruct((MS, M_HEADS, M_DQK, M_DV), F32),
            jax.ShapeDtypeStruct((MS, M_HEADS * M_DQK), F32),
            jax.ShapeDtypeStruct((MS, LANES), F32),
        ],
        compiler_params=_cparams(("parallel", "arbitrary")),
        name="mlstm_decode",
    )(zs, zs, zs, zs, qt, kt, gates_s, bias, m0p, gn, state_c, state_n)


NB = SEQ // ATT_BLOCK
KVW = KV_HEADS * HEAD_DIM


def _swa_prompt_kernel(sink_ref, q_ref, kc_ref, kp_ref, vc_ref, vp_ref, o_ref):
    nb = pl.program_id(1)
    cols = GROUP * ATT_BLOCK
    sidx = lax.broadcasted_iota(jnp.int32, (2 * ATT_BLOCK, cols), 0)
    t = lax.broadcasted_iota(jnp.int32, (2 * ATT_BLOCK, cols), 1) & (ATT_BLOCK - 1)
    rel = t + ATT_BLOCK - sidx
    visible = jnp.logical_and(jnp.logical_and(rel >= 0, rel <= WINDOW),
                              jnp.logical_or(sidx >= ATT_BLOCK, nb > 0))
    bias = jnp.where(visible, 0.0, -jnp.inf)
    q = q_ref[...] * (HEAD_DIM ** -0.5)
    kk = jnp.concatenate([kp_ref[...], kc_ref[...]], axis=0).astype(BF16)
    vv_t = jnp.concatenate([vp_ref[...], vc_ref[...]], axis=0).T.astype(BF16)
    for g in range(KV_HEADS):
        qg = jnp.concatenate(
            [q[:, (g * GROUP + hh) * HEAD_DIM:(g * GROUP + hh + 1) * HEAD_DIM] for hh in range(GROUP)], axis=0)
        kg = kk[:, g * HEAD_DIM:(g + 1) * HEAD_DIM]
        vg_t = vv_t[g * HEAD_DIM:(g + 1) * HEAD_DIM, :]
        sink = jnp.concatenate(
            [jnp.full((1, ATT_BLOCK), sink_ref[g * GROUP + hh], F32) for hh in range(GROUP)], axis=1)
        s = lax.dot_general(kg, qg, (((1,), (1,)), ((), ())), preferred_element_type=F32) + bias
        mx = jnp.maximum(jnp.max(s, axis=0, keepdims=True), sink)
        p = jnp.exp(s - mx)
        denom = jnp.sum(p, axis=0, keepdims=True) + jnp.exp(sink - mx)
        o_t = jnp.dot(vg_t, p.astype(BF16), preferred_element_type=F32) / denom
        og = jnp.concatenate([o_t[:, hh * ATT_BLOCK:(hh + 1) * ATT_BLOCK].T for hh in range(GROUP)], axis=1)
        o_ref[:, g * GROUP * HEAD_DIM:(g + 1) * GROUP * HEAD_DIM] = og.astype(BF16)


def _swa_prompt(z, kv, sinks):
    qw = N_HEADS * HEAD_DIM
    return pl.pallas_call(
        _swa_prompt_kernel,
        grid=(BATCH, NB),
        in_specs=[
            pl.BlockSpec(memory_space=pltpu.SMEM),
            pl.BlockSpec((ATT_BLOCK, qw), lambda b, n: (b * NB + n, C_QA // qw)),
            pl.BlockSpec((ATT_BLOCK, KVW), lambda b, n: (b * NB + n, 0)),
            pl.BlockSpec((ATT_BLOCK, KVW), lambda b, n: (b * NB + jnp.maximum(n - 1, 0), 0)),
            pl.BlockSpec((ATT_BLOCK, KVW), lambda b, n: (b * NB + n, 1)),
            pl.BlockSpec((ATT_BLOCK, KVW), lambda b, n: (b * NB + jnp.maximum(n - 1, 0), 1)),
        ],
        out_specs=pl.BlockSpec((ATT_BLOCK, qw), lambda b, n: (b * NB + n, 0)),
        out_shape=jax.ShapeDtypeStruct((MP, qw), BF16),
        compiler_params=_cparams(("parallel", "arbitrary")),
        name="swa_prompt",
    )(sinks, z, kv, kv, kv, kv)


def _swa_decode_kernel(q_ref, kn_ref, vn_ref, knf_ref, vnf_ref, kb_ref, vb_ref, sink_ref,
                       o_ref, kc_ref, vc_ref):
    scale = HEAD_DIM ** -0.5
    for j in range(TB):
        kbuf = kb_ref[j]
        vbuf = vb_ref[j]
        kc_ref[j, 0:WINDOW - 1, :] = kbuf[1:WINDOW, :]
        kc_ref[j, WINDOW - 1:WINDOW, :] = knf_ref[j:j + 1, :]
        vc_ref[j, 0:WINDOW - 1, :] = vbuf[1:WINDOW, :]
        vc_ref[j, WINDOW - 1:WINDOW, :] = vnf_ref[j:j + 1, :]
        for g in range(KV_HEADS):
            qg = q_ref[j, g * GROUP:(g + 1) * GROUP, :]
            kg = kbuf[:, g * HEAD_DIM:(g + 1) * HEAD_DIM].astype(BF16)
            vg = vbuf[:, g * HEAD_DIM:(g + 1) * HEAD_DIM].astype(BF16)
            s_c = lax.dot_general(qg.astype(BF16), kg, (((1,), (1,)), ((), ())),
                                  preferred_element_type=F32) * scale
            s_n = jnp.sum(qg * kn_ref[j, g:g + 1, :], axis=1, keepdims=True) * scale
            sink = sink_ref[g * GROUP:(g + 1) * GROUP, 0:1]
            mx = jnp.maximum(jnp.maximum(jnp.max(s_c, axis=1, keepdims=True), s_n), sink)
            p_c = jnp.exp(s_c - mx)
            p_n = jnp.exp(s_n - mx)
            denom = jnp.sum(p_c, axis=1, keepdims=True) + p_n + jnp.exp(sink - mx)
            o = jnp.dot((p_c / denom).astype(BF16), vg, preferred_element_type=F32)
            o_ref[j, g * GROUP:(g + 1) * GROUP, :] = o + (p_n / denom) * vn_ref[j, g:g + 1, :]


def _swa_decode(layer, q3, kn3, vn3, knf, vnf, kbuf, vbuf, sinks_b):
    return pl.pallas_call(
        _swa_decode_kernel,
        grid=(MS // TB,),
        in_specs=[
            pl.BlockSpec((TB, N_HEADS, HEAD_DIM), lambda i: (i, 0, 0)),
            pl.BlockSpec((TB, KV_HEADS, HEAD_DIM), lambda i: (i, 0, 0)),
            pl.BlockSpec((TB, KV_HEADS, HEAD_DIM), lambda i: (i, 0, 0)),
            pl.BlockSpec((TB, KVW), lambda i: (i, 0)),
            pl.BlockSpec((TB, KVW), lambda i: (i, 0)),
            pl.BlockSpec((None, TB, WINDOW, KVW), lambda i: (layer, i, 0, 0)),
            pl.BlockSpec((None, TB, WINDOW, KVW), lambda i: (layer, i, 0, 0)),
            pl.BlockSpec((N_HEADS, LANES), lambda i: (0, 0)),
        ],
        out_specs=[
            pl.BlockSpec((TB, N_HEADS, HEAD_DIM), lambda i: (i, 0, 0)),
            pl.BlockSpec((TB, WINDOW, KVW), lambda i: (i, 0, 0)),
            pl.BlockSpec((TB, WINDOW, KVW), lambda i: (i, 0, 0)),
        ],
        out_shape=[
            jax.ShapeDtypeStruct((MS, N_HEADS, HEAD_DIM), F32),
            jax.ShapeDtypeStruct((MS, WINDOW, KVW), F32),
            jax.ShapeDtypeStruct((MS, WINDOW, KVW), F32),
        ],
        compiler_params=_cparams(("parallel",)),
        name="swa_decode",
    )(q3, kn3, vn3, knf, vnf, kbuf, vbuf, sinks_b)


def _merge_kernel(hm_ref, ha_ref, wbm_ref, wba_ref, gm_ref, ga_ref, o_ref):
    a = jnp.dot(hm_ref[...], wbm_ref[...], preferred_element_type=F32)
    b = jnp.dot(ha_ref[...], wba_ref[...], preferred_element_type=F32)
    o_ref[...] = (_sigmoid(gm_ref[...].astype(F32)) * a + _sigmoid(ga_ref[...].astype(F32)) * b).astype(BF16)


def _merge(hm, ha, wbm, wba, z):
    return pl.pallas_call(
        _merge_kernel,
        grid=(M // TM, D_MODEL // TN),
        in_specs=[
            pl.BlockSpec((TM, D_MODEL), lambda i, j: (i, 0)),
            pl.BlockSpec((TM, D_MODEL), lambda i, j: (i, 0)),
            pl.BlockSpec((D_MODEL, TN), lambda i, j: (0, j)),
            pl.BlockSpec((D_MODEL, TN), lambda i, j: (0, j)),
            pl.BlockSpec((TM, TN), lambda i, j: (i, C_GM // TN + j)),
            pl.BlockSpec((TM, TN), lambda i, j: (i, C_GA // TN + j)),
        ],
        out_specs=pl.BlockSpec((TM, TN), lambda i, j: (i, j)),
        out_shape=jax.ShapeDtypeStruct((M, D_MODEL), BF16),
        compiler_params=_cparams(("parallel", "arbitrary")),
        name="merge",
    )(hm, ha, wbm, wba, z, z)


def _mm_res_kernel(a_ref, w_ref, r_ref, o_ref):
    o_ref[...] = r_ref[...] + jnp.dot(a_ref[...], w_ref[...], preferred_element_type=F32)


def _mm_res(a, w, res, tm, tn, name):
    kdim = a.shape[1]
    n = w.shape[1]
    return pl.pallas_call(
        _mm_res_kernel,
        grid=(M // tm, n // tn),
        in_specs=[
            pl.BlockSpec((tm, kdim), lambda i, j: (i, 0)),
            pl.BlockSpec((kdim, tn), lambda i, j: (0, j)),
            pl.BlockSpec((tm, tn), lambda i, j: (i, j)),
        ],
        out_specs=pl.BlockSpec((tm, tn), lambda i, j: (i, j)),
        out_shape=jax.ShapeDtypeStruct((M, n), F32),
        compiler_params=_cparams(("parallel", "arbitrary")),
        name=name,
    )(a, w, res)


def _ffn_up_kernel(x_ref, g_ref, wg_ref, wu_ref, act_ref, h_scr):
    @pl.when(pl.program_id(1) == 0)
    def _():
        h_scr[...] = _rms(x_ref[...], g_ref[...]).astype(BF16)

    h = h_scr[...]
    a = jnp.dot(h, wg_ref[...], preferred_element_type=F32)
    u = jnp.dot(h, wu_ref[...], preferred_element_type=F32)
    act_ref[...] = ((a * _sigmoid(a)) * u).astype(BF16)


def _ffn_up(x, g, wg, wu):
    return pl.pallas_call(
        _ffn_up_kernel,
        grid=(M // TM, D_FF // TN),
        in_specs=[
            pl.BlockSpec((TM, D_MODEL), lambda i, j: (i, 0)),
            pl.BlockSpec((1, D_MODEL), lambda i, j: (0, 0)),
            pl.BlockSpec((D_MODEL, TN), lambda i, j: (0, j)),
            pl.BlockSpec((D_MODEL, TN), lambda i, j: (0, j)),
        ],
        out_specs=pl.BlockSpec((TM, TN), lambda i, j: (i, j)),
        out_shape=jax.ShapeDtypeStruct((M, D_FF), BF16),
        scratch_shapes=[pltpu.VMEM((TM, D_MODEL), BF16)],
        compiler_params=_cparams(("parallel", "arbitrary")),
        name="ffn_up",
    )(x, g, wg, wu)


def _router_kernel(x_ref, g_ref, wr_ref, ids_ref, gates_ref, cnt_ref, run_scr):
    @pl.when(pl.program_id(0) == 0)
    def _():
        run_scr[...] = jnp.zeros_like(run_scr)

    h = _rms(x_ref[...], g_ref[...])
    logits = jnp.dot(h, wr_ref[...], preferred_element_type=F32, precision=lax.Precision.HIGHEST)
    lane = lax.broadcasted_iota(jnp.int32, logits.shape, 1)
    lg = jnp.where(lane < N_EXPERTS, logits, -jnp.inf)
    v1 = jnp.max(lg, axis=1, keepdims=True)
    i1 = jnp.min(jnp.where(lg == v1, lane, LANES), axis=1, keepdims=True)
    lg2 = jnp.where(lane == i1, -jnp.inf, lg)
    v2 = jnp.max(lg2, axis=1, keepdims=True)
    i2 = jnp.min(jnp.where(lg2 == v2, lane, LANES), axis=1, keepdims=True)
    e2 = jnp.exp(v2 - v1)
    g1 = 1.0 / (1.0 + e2)
    g2 = e2 / (1.0 + e2)
    gates_ref[...] = jnp.where(lane == 0, g1, jnp.where(lane == 1, g2, 0.0))

    picks = jnp.where(jnp.logical_or(lane == i1, lane == i2), 1.0, 0.0)
    tm = picks.shape[0]
    earlier = lax.broadcasted_iota(jnp.int32, (tm, tm), 1) < lax.broadcasted_iota(jnp.int32, (tm, tm), 0)
    before = jnp.dot(jnp.where(earlier, 1.0, 0.0).astype(BF16), picks.astype(BF16),
                     preferred_element_type=F32) + run_scr[...]
    r1 = jnp.sum(jnp.where(lane == i1, before, 0.0), axis=1, keepdims=True).astype(jnp.int32)
    r2 = jnp.sum(jnp.where(lane == i2, before, 0.0), axis=1, keepdims=True).astype(jnp.int32)
    ids_ref[...] = jnp.where(lane == 0, i1, jnp.where(lane == 1, i2, jnp.where(lane == 2, r1, jnp.where(lane == 3, r2, 0))))
    total = run_scr[...] + jnp.sum(picks, axis=0, keepdims=True)
    run_scr[...] = total
    cnt_ref[...] = total


def _router(x, g, wr_pad):
    return pl.pallas_call(
        _router_kernel,
        grid=(M // TM,),
        in_specs=[
            pl.BlockSpec((TM, D_MODEL), lambda i: (i, 0)),
            pl.BlockSpec((1, D_MODEL), lambda i: (0, 0)),
            pl.BlockSpec((D_MODEL, LANES), lambda i: (0, 0)),
        ],
        out_specs=[
            pl.BlockSpec((TM, LANES), lambda i: (i, 0)),
            pl.BlockSpec((TM, LANES), lambda i: (i, 0)),
            pl.BlockSpec((1, LANES), lambda i: (0, 0)),
        ],
        out_shape=[
            jax.ShapeDtypeStruct((M, LANES), jnp.int32),
            jax.ShapeDtypeStruct((M, LANES), F32),
            jax.ShapeDtypeStruct((1, LANES), F32),
        ],
        scratch_shapes=[pltpu.VMEM((1, LANES), F32)],
        compiler_params=_cparams(("arbitrary",)),
        name="router",
    )(x, g, wr_pad)


def _row_copy(src_hbm, row, buf, r, sem):
    return pltpu.make_async_copy(src_hbm.at[pl.ds(row, 1), :], buf.at[pl.ds(r, 1), :], sem)


def _slot_gather_kernel(tok_ref, nu_ref, x_hbm, g_ref, xs_ref, buf, sem):
    base = pl.program_id(0) * GATHER_ROWS

    @pl.when(base < nu_ref[0] * MOE_TM)
    def _():
        def issue(r, carry):
            _row_copy(x_hbm, tok_ref[base + r], buf, r, sem).start()
            return carry
        lax.fori_loop(0, GATHER_ROWS, issue, 0, unroll=8)

        def wait(r, carry):
            _row_copy(x_hbm, 0, buf, r, sem).wait()
            return carry
        lax.fori_loop(0, GATHER_ROWS, wait, 0, unroll=8)
        xs_ref[...] = _rms(buf[...], g_ref[...]).astype(BF16)

    @pl.when(base >= nu_ref[0] * MOE_TM)
    def _():
        xs_ref[...] = jnp.zeros_like(xs_ref)


def _slot_gather(slot_token, n_used, x, g):
    return pl.pallas_call(
        _slot_gather_kernel,
        grid_spec=pltpu.PrefetchScalarGridSpec(
            num_scalar_prefetch=2,
            grid=(N_SLOTS // GATHER_ROWS,),
            in_specs=[pl.BlockSpec(memory_space=pl.ANY),
                      pl.BlockSpec((1, D_MODEL), lambda m, tok, nu: (0, 0))],
            out_specs=pl.BlockSpec((GATHER_ROWS, D_MODEL), lambda m, tok, nu: (m, 0)),
            scratch_shapes=[pltpu.VMEM((GATHER_ROWS, D_MODEL), F32), pltpu.SemaphoreType.DMA(())],
        ),
        out_shape=jax.ShapeDtypeStruct((N_SLOTS, D_MODEL), BF16),
        compiler_params=_cparams(("arbitrary",)),
        name="slot_gather",
    )(slot_token, n_used, x, g)


def _combine_kernel(p1_ref, p2_ref, x_ref, y_hbm, g_ref, o_ref, buf1, buf2, sem, *, final_norm):
    base = pl.program_id(0) * COMBINE_ROWS

    def issue(r, carry):
        _row_copy(y_hbm, p1_ref[base + r], buf1, r, sem).start()
        _row_copy(y_hbm, p2_ref[base + r], buf2, r, sem).start()
        return carry
    lax.fori_loop(0, COMBINE_ROWS, issue, 0, unroll=8)

    def wait(r, carry):
        _row_copy(y_hbm, 0, buf1, r, sem).wait()
        _row_copy(y_hbm, 0, buf2, r, sem).wait()
        return carry
    lax.fori_loop(0, COMBINE_ROWS, wait, 0, unroll=8)
    out = x_ref[...] + buf1[...] + buf2[...]
    o_ref[...] = _rms(out, g_ref[...]) if final_norm else out


def _combine(pos1, pos2, x, y, g_final):
    final_norm = g_final is not None
    g = g_final if final_norm else jnp.ones((1, D_MODEL), F32)
    return pl.pallas_call(
        functools.partial(_combine_kernel, final_norm=final_norm),
        grid_spec=pltpu.PrefetchScalarGridSpec(
            num_scalar_prefetch=2,
            grid=(M // COMBINE_ROWS,),
            in_specs=[pl.BlockSpec((COMBINE_ROWS, D_MODEL), lambda i, p1, p2: (i, 0)),
                      pl.BlockSpec(memory_space=pl.ANY),
                      pl.BlockSpec((1, D_MODEL), lambda i, p1, p2: (0, 0))],
            out_specs=pl.BlockSpec((COMBINE_ROWS, D_MODEL), lambda i, p1, p2: (i, 0)),
            scratch_shapes=[pltpu.VMEM((COMBINE_ROWS, D_MODEL), F32), pltpu.VMEM((COMBINE_ROWS, D_MODEL), F32),
                            pltpu.SemaphoreType.DMA(())],
        ),
        out_shape=jax.ShapeDtypeStruct((M, D_MODEL), F32),
        compiler_params=_cparams(("arbitrary",)),
        name="combine",
    )(pos1, pos2, x, y, g)


def _moe_up_kernel(te_ref, nu_ref, x_ref, wg_ref, wu_ref, act_ref, wg_scr, wu_scr):
    m = pl.program_id(1)

    @pl.when(m < nu_ref[0])
    def _():
        prev = te_ref[jnp.maximum(m - 1, 0)]

        @pl.when(jnp.logical_or(m == 0, te_ref[m] != prev))
        def _():
            wg_scr[...] = wg_ref[...].astype(BF16)
            wu_scr[...] = wu_ref[...].astype(BF16)

        x = x_ref[...]
        a = jnp.dot(x, wg_scr[...], preferred_element_type=F32)
        u = jnp.dot(x, wu_scr[...], preferred_element_type=F32)
        act_ref[...] = ((a * _sigmoid(a)) * u).astype(BF16)

    @pl.when(m >= nu_ref[0])
    def _():
        act_ref[...] = jnp.zeros_like(act_ref)


def _moe_up(tile_expert, n_used, xs, wg, wu):
    return pl.pallas_call(
        _moe_up_kernel,
        grid_spec=pltpu.PrefetchScalarGridSpec(
            num_scalar_prefetch=2,
            grid=(E_FF // MOE_TF, N_SLOT_TILES),
            in_specs=[
                pl.BlockSpec((MOE_TM, D_MODEL), lambda j, m, te, nu: (m, 0)),
                pl.BlockSpec((None, D_MODEL, MOE_TF), lambda j, m, te, nu: (te[m], 0, j)),
                pl.BlockSpec((None, D_MODEL, MOE_TF), lambda j, m, te, nu: (te[m], 0, j)),
            ],
            out_specs=pl.BlockSpec((MOE_TM, MOE_TF), lambda j, m, te, nu: (m, j)),
            scratch_shapes=[pltpu.VMEM((D_MODEL, MOE_TF), BF16), pltpu.VMEM((D_MODEL, MOE_TF), BF16)],
        ),
        out_shape=jax.ShapeDtypeStruct((N_SLOTS, E_FF), BF16),
        compiler_params=_cparams(("arbitrary", "arbitrary")),
        name="moe_up",
    )(tile_expert, n_used, xs, wg, wu)


def _moe_down_kernel(te_ref, nu_ref, a_ref, wd_ref, sg_ref, y_ref, wd_scr):
    m = pl.program_id(1)

    @pl.when(m < nu_ref[0])
    def _():
        prev = te_ref[jnp.maximum(m - 1, 0)]

        @pl.when(jnp.logical_or(m == 0, te_ref[m] != prev))
        def _():
            wd_scr[...] = wd_ref[...].astype(BF16)

        y_ref[...] = sg_ref[...] * jnp.dot(a_ref[...], wd_scr[...], preferred_element_type=F32)

    @pl.when(m >= nu_ref[0])
    def _():
        y_ref[...] = jnp.zeros_like(y_ref)


def _moe_down(tile_expert, n_used, act, wd, slot_gate):
    return pl.pallas_call(
        _moe_down_kernel,
        grid_spec=pltpu.PrefetchScalarGridSpec(
            num_scalar_prefetch=2,
            grid=(D_MODEL // MOE_TN, N_SLOT_TILES),
            in_specs=[
                pl.BlockSpec((MOE_TM, E_FF), lambda j, m, te, nu: (m, 0)),
                pl.BlockSpec((None, E_FF, MOE_TN), lambda j, m, te, nu: (te[m], 0, j)),
                pl.BlockSpec((MOE_TM, 1), lambda j, m, te, nu: (m, 0)),
            ],
            out_specs=pl.BlockSpec((MOE_TM, MOE_TN), lambda j, m, te, nu: (m, j)),
            scratch_shapes=[pltpu.VMEM((E_FF, MOE_TN), BF16)],
        ),
        out_shape=jax.ShapeDtypeStruct((N_SLOTS, D_MODEL), F32),
        compiler_params=_cparams(("arbitrary", "arbitrary")),
        name="moe_down",
    )(tile_expert, n_used, act, wd, slot_gate)


def _final_norm_kernel(x_ref, g_ref, o_ref):
    o_ref[...] = _rms(x_ref[...], g_ref[...])


def _final_norm(x, g):
    spec = pl.BlockSpec((COMBINE_ROWS, D_MODEL), lambda i: (i, 0))
    return pl.pallas_call(
        _final_norm_kernel,
        grid=(M // COMBINE_ROWS,),
        in_specs=[spec, pl.BlockSpec((1, D_MODEL), lambda i: (0, 0))],
        out_specs=spec,
        out_shape=jax.ShapeDtypeStruct((M, D_MODEL), F32),
        compiler_params=_cparams(("parallel",)),
        name="final_norm",
    )(x, g)


def _route(ids, gates, counts):
    counts = counts[0, :N_EXPERTS].astype(jnp.int32)
    padded = ((counts + MOE_TM - 1) // MOE_TM) * MOE_TM
    ends = jnp.cumsum(padded)
    starts = ends - padded
    pos = jnp.concatenate([starts[ids[:, 0]] + ids[:, 2], starts[ids[:, 1]] + ids[:, 3]])
    g_flat = jnp.concatenate([gates[:, 0], gates[:, 1]])
    tok = jnp.tile(jnp.arange(M, dtype=jnp.int32), TOP_K)
    slot_token = jnp.zeros((N_SLOTS,), jnp.int32).at[pos].set(tok)
    slot_gate = jnp.zeros((N_SLOTS,), F32).at[pos].set(g_flat)
    n_used = (ends[-1] // MOE_TM).astype(jnp.int32)
    tile_start = jnp.arange(N_SLOT_TILES, dtype=jnp.int32) * MOE_TM
    tile_expert = jnp.sum((tile_start[:, None] >= ends[None, :]).astype(jnp.int32), axis=1)
    last = jnp.sum((jnp.maximum(n_used - 1, 0) * MOE_TM >= ends).astype(jnp.int32))
    tile_expert = jnp.where(jnp.arange(N_SLOT_TILES) < n_used, tile_expert, last).astype(jnp.int32)
    return slot_token, slot_gate[:, None], tile_expert, n_used.reshape(1), pos[:M], pos[M:]


def _rope_tables():
    half = ROT_DIM // 2
    pos = jnp.concatenate([jnp.tile(jnp.arange(SEQ, dtype=jnp.int32), BATCH),
                           jnp.full((MS,), PAST_LEN, jnp.int32)])
    inv = jnp.power(ROPE_THETA, -jnp.arange(half, dtype=F32) / half)
    ang = pos.astype(F32)[:, None] * inv[None, :]
    cos, sin = jnp.cos(ang), jnp.sin(ang)
    pad = HEAD_DIM - ROT_DIM
    one = jnp.ones((M, pad), F32)
    zero = jnp.zeros((M, pad), F32)
    zh = jnp.zeros((M, half), F32)
    cs = jnp.concatenate([cos, cos, one], axis=1)
    sa = jnp.concatenate([-sin, zh, zero], axis=1)
    sb = jnp.concatenate([zh, sin, zero], axis=1)
    rep = LANES // HEAD_DIM
    return jnp.tile(cs, (1, rep)), jnp.tile(sa, (1, rep)), jnp.tile(sb, (1, rep))


def _pad_lanes(a):
    return jnp.pad(a, ((0, 0), (0, LANES - a.shape[1])))


def kernel(x_prompt, x_sample, state_mlstm_C, state_mlstm_n, state_mlstm_m, cache_swa_k, cache_swa_v, norm_mix_g, w_in, b_igate, b_fgate, mlstm_norm_g, attn_sinks, w_branch_m, w_branch_a, w_out, norm_ffn_g, w_gate_dense, w_up_dense, w_down_dense, w_router, w_gate_moe, w_up_moe, w_down_moe, norm_final_g):
    x = jnp.concatenate([x_prompt.reshape(MP, D_MODEL), x_sample.reshape(MS, D_MODEL)], axis=0)
    cs, sa, sb = _rope_tables()
    state_n = state_mlstm_n.reshape(DEPTH, MS, M_HEADS * M_DQK)
    kbuf = cache_swa_k.reshape(DEPTH, MS, WINDOW, KVW)
    vbuf = cache_swa_v.reshape(DEPTH, MS, WINDOW, KVW)
    outs = {name: [] for name in ("Cp", "np", "mp", "kp", "vp", "Cs", "ns", "ms", "ks", "vs")}
    y_final = None

    for l in range(DEPTH):
        w_l = w_in[l]
        w_al = jnp.concatenate([w_l[:, :R_GATES], w_l[:, R_QA:]], axis=1).astype(BF16)
        w_gate = _pad_lanes(w_l[:, R_GATES:R_QA]).astype(BF16)
        bias = _pad_lanes(jnp.concatenate([b_igate[l], b_fgate[l]])[None, :])
        gn = mlstm_norm_g[l][None, :]
        sinks_b = jnp.broadcast_to(attn_sinks[l][:, None], (N_HEADS, LANES))

        z, kv, gates = _inproj(x, norm_mix_g[l][None, :], w_al, w_gate, cs, sa, sb)

        hm_p, c_p, n_p, m_p = _mlstm_prompt(z, gates, bias, gn)
        ha_p = _swa_prompt(z, kv, attn_sinks[l])
        kv_p = kv[:MP].reshape(BATCH, SEQ, 2, KV_HEADS, HEAD_DIM)[:, SEQ - WINDOW:]
        outs["Cp"].append(c_p)
        outs["np"].append(n_p.reshape(BATCH, M_HEADS, M_DQK))
        outs["mp"].append(m_p[:, :, 0, 0])
        outs["kp"].append(kv_p[:, :, 0])
        outs["vp"].append(kv_p[:, :, 1])

        zs = z[MP:].astype(F32)
        kv_s = kv[MP:]
        qt = zs[:, C_QM:C_KM].T.reshape(M_HEADS, M_DQK, MS)
        kt = zs[:, C_KM:C_VM].T.reshape(M_HEADS, M_DQK, MS)
        m0p = _pad_lanes(state_mlstm_m[l])
        hm_s, c_s, n_s, m_s = _mlstm_decode(l, zs, qt, kt, gates[MP:], bias, m0p, gn, state_mlstm_C, state_n)
        kn = kv_s[:, :KVW]
        vn = kv_s[:, KVW:]
        ha_s, k_s, v_s = _swa_decode(
            l, zs[:, C_QA:C_KA].reshape(MS, N_HEADS, HEAD_DIM), kn.reshape(MS, KV_HEADS, HEAD_DIM),
            vn.reshape(MS, KV_HEADS, HEAD_DIM), kn, vn, kbuf, vbuf, sinks_b)
        outs["Cs"].append(c_s)
        outs["ns"].append(n_s.reshape(MS, M_HEADS, M_DQK))
        outs["ms"].append(m_s[:, :M_HEADS])
        outs["ks"].append(k_s.reshape(MS, WINDOW, KV_HEADS, HEAD_DIM))
        outs["vs"].append(v_s.reshape(MS, WINDOW, KV_HEADS, HEAD_DIM))

        hm = jnp.concatenate([hm_p, hm_s.astype(BF16)], axis=0)
        ha = jnp.concatenate([ha_p, ha_s.reshape(MS, N_HEADS * HEAD_DIM).astype(BF16)], axis=0)
        merged = _merge(hm, ha, w_branch_m[l].astype(BF16), w_branch_a[l].astype(BF16), z)
        x = _mm_res(merged, w_out[l].astype(BF16), x, TM, TN, "out_proj")

        if l % 2 == 0:
            jd = l // 2
            act = _ffn_up(x, norm_ffn_g[l][None, :], w_gate_dense[jd].astype(BF16), w_up_dense[jd].astype(BF16))
            x = _mm_res(act, w_down_dense[jd].astype(BF16), x, TM, TN, "ffn_down")
        else:
            jm = l // 2
            g_ffn = norm_ffn_g[l][None, :]
            ids, gts, counts = _router(x, g_ffn, _pad_lanes(w_router[jm]))
            slot_token, slot_gate, tile_expert, n_used, pos1, pos2 = _route(ids, gts, counts)
            xs = _slot_gather(slot_token, n_used, x, g_ffn)
            act = _moe_up(tile_expert, n_used, xs, w_gate_moe[jm], w_up_moe[jm])
            y = _moe_down(tile_expert, n_used, act, w_down_moe[jm], slot_gate)
            if l == DEPTH - 1:
                y_final = _combine(pos1, pos2, x, y, norm_final_g[None, :])
            else:
                x = _combine(pos1, pos2, x, y, None)

    if y_final is None:
        y_final = _final_norm(x, norm_final_g[None, :])
    y_prompt = y_final[:MP].reshape(BATCH, SEQ, D_MODEL)
    y_sample = y_final[MP:].reshape(MS, 1, D_MODEL)
    st = lambda name: jnp.stack(outs[name])
    return (y_prompt, y_sample, st("Cp"), st("np"), st("mp"), st("kp"), st("vp"),
            st("Cs"), st("ns"), st("ms"), st("ks"), st("vs"))
```

```python
import functools

import jax
import jax.numpy as jnp
import numpy as np
from jax import lax
from jax.experimental import pallas as pl
from jax.experimental.pallas import tpu as pltpu

F32 = jnp.float32
BF16 = jnp.bfloat16

D_MODEL = 2048
BATCH = 4
SEQ = 2048
DEPTH = 2
DEC_BATCH = 128
PAST_LEN = 8192
M_HEADS = 4
M_DQK = 256
M_DV = 512
N_HEADS = 32
KV_HEADS = 4
HEAD_DIM = 64
GROUP = N_HEADS // KV_HEADS
ROT_DIM = HEAD_DIM // 4
ROPE_THETA = 500000.0
WINDOW = 128
ATT_BLOCK = 128
D_FF = 5632
N_EXPERTS = 8
TOP_K = 2
E_FF = 7168
EPS = 1e-6

MP = BATCH * SEQ
MS = DEC_BATCH
M = MP + MS

C_QM = 0
C_KM = C_QM + M_HEADS * M_DQK
C_VM = C_KM + M_HEADS * M_DQK
C_OM = C_VM + M_HEADS * M_DV
C_QA = C_OM + M_HEADS * M_DV
C_KA = C_QA + N_HEADS * HEAD_DIM
C_VA = C_KA + KV_HEADS * HEAD_DIM
C_GM = C_VA + KV_HEADS * HEAD_DIM
C_GA = C_GM + D_MODEL
N_Z = C_GA + D_MODEL
R_GATES = 2 * M_HEADS * M_DQK + 2 * M_HEADS * M_DV
R_QA = R_GATES + 2 * M_HEADS

LANES = 128
TM = 1040
TN = 512
ML = 256
TB = 8
MOE_PAD = 512
MOE_UP_TM = MOE_PAD
MOE_DN_TM = 256
MOE_TF = 512
MOE_TN = 512
COMBINE_ROWS = TM // 2
N_SLOTS = -(-(TOP_K * M + N_EXPERTS * (MOE_PAD - 1)) // MOE_PAD) * MOE_PAD
VMEM_LIMIT = 56 * 1024 * 1024


def _cparams(sem, vmem=VMEM_LIMIT):
    return pltpu.CompilerParams(dimension_semantics=sem, vmem_limit_bytes=vmem)


def _rms(x, g):
    ms = jnp.mean(x * x, axis=-1, keepdims=True)
    return (x * lax.rsqrt(ms + EPS)) * g


def _sigmoid(x):
    return 1.0 / (1.0 + jnp.exp(-x))


def _log_sigmoid(x):
    return jnp.minimum(x, 0.0) - jnp.log(1.0 + jnp.exp(-jnp.abs(x)))


IN_TN = 1024
KV_COLS = C_GM - C_KA
J_QA0 = C_QA // IN_TN
J_KV = C_KA // IN_TN
N_ZP = -(-N_Z // IN_TN) * IN_TN
assert C_QA % IN_TN == 0 and C_KA % IN_TN == 0 and KV_COLS <= IN_TN


def _rope(acc, cs, sa, sb):
    n = acc.shape[1]
    return acc * cs + pltpu.roll(acc, n - ROT_DIM // 2, 1) * sa + pltpu.roll(acc, ROT_DIM // 2, 1) * sb


def _inproj_kernel(x_ref, g_ref, w_ref, wgate_ref, cs_ref, sa_ref, sb_ref, z_ref, kv_ref, gt_ref, h_scr):
    j = pl.program_id(1)

    @pl.when(j == 0)
    def _():
        h = _rms(x_ref[...], g_ref[...]).astype(BF16)
        h_scr[...] = h
        gt_ref[...] = jnp.dot(h, wgate_ref[...], preferred_element_type=F32)

    acc = jnp.dot(h_scr[...], w_ref[...], preferred_element_type=F32)
    reps = IN_TN // LANES
    is_q = jnp.logical_and(j >= J_QA0, j < J_KV)
    is_kv = j == J_KV

    @pl.when(is_q)
    def _():
        cs = jnp.tile(cs_ref[...], (1, reps))
        sa = jnp.tile(sa_ref[...], (1, reps))
        sb = jnp.tile(sb_ref[...], (1, reps))
        z_ref[...] = _rope(acc, cs, sa, sb).astype(BF16)

    @pl.when(is_kv)
    def _():
        is_k = lax.broadcasted_iota(jnp.int32, acc.shape, 1) < (C_VA - C_KA)
        cs = jnp.where(is_k, jnp.tile(cs_ref[...], (1, reps)), 1.0)
        sa = jnp.where(is_k, jnp.tile(sa_ref[...], (1, reps)), 0.0)
        sb = jnp.where(is_k, jnp.tile(sb_ref[...], (1, reps)), 0.0)
        r = _rope(acc, cs, sa, sb)
        kv_ref[...] = r[:, :KV_COLS]
        z_ref[...] = r.astype(BF16)

    @pl.when(jnp.logical_not(jnp.logical_or(is_q, is_kv)))
    def _():
        z_ref[...] = acc.astype(BF16)


def _inproj(x, g, w_al, w_gate, cs, sa, sb):
    return pl.pallas_call(
        _inproj_kernel,
        grid=(M // TM, N_ZP // IN_TN),
        in_specs=[
            pl.BlockSpec((TM, D_MODEL), lambda i, j: (i, 0)),
            pl.BlockSpec((1, D_MODEL), lambda i, j: (0, 0)),
            pl.BlockSpec((D_MODEL, IN_TN), lambda i, j: (0, j)),
            pl.BlockSpec((D_MODEL, LANES), lambda i, j: (0, 0)),
            pl.BlockSpec((TM, LANES), lambda i, j: (i, 0)),
            pl.BlockSpec((TM, LANES), lambda i, j: (i, 0)),
            pl.BlockSpec((TM, LANES), lambda i, j: (i, 0)),
        ],
        out_specs=[
            pl.BlockSpec((TM, IN_TN), lambda i, j: (i, j)),
            pl.BlockSpec((TM, KV_COLS), lambda i, j: (i, 0)),
            pl.BlockSpec((TM, LANES), lambda i, j: (i, 0)),
        ],
        out_shape=[
            jax.ShapeDtypeStruct((M, N_ZP), BF16),
            jax.ShapeDtypeStruct((M, KV_COLS), F32),
            jax.ShapeDtypeStruct((M, LANES), F32),
        ],
        scratch_shapes=[pltpu.VMEM((TM, D_MODEL), BF16)],
        compiler_params=_cparams(("parallel", "arbitrary")),
        name="inproj",
    )(x, g, w_al, w_gate, cs, sa, sb)


NC = SEQ // ML


def _mlstm_prompt_kernel(q_ref, k_ref, v_ref, o_ref, gt_ref, bias_ref, gn_ref,
                         hm_ref, c_ref, n_ref, m_ref, c_scr, n_scr, m_scr):
    h = pl.program_id(1)
    c = pl.program_id(2)

    @pl.when(c == 0)
    def _():
        c_scr[...] = jnp.zeros_like(c_scr)
        n_scr[...] = jnp.zeros_like(n_scr)
        m_scr[...] = jnp.zeros_like(m_scr)

    gates = gt_ref[...] + bias_ref[...]
    lane = lax.broadcasted_iota(jnp.int32, gates.shape, 1)
    i_msk = jnp.where(lane == h, gates, 0.0)
    lf_msk = jnp.where(lane == h + M_HEADS, _log_sigmoid(gates), 0.0)
    i_col = jnp.sum(i_msk, axis=1, keepdims=True)
    row = lax.broadcasted_iota(jnp.int32, (ML, ML), 0)
    col = lax.broadcasted_iota(jnp.int32, (ML, ML), 1)
    causal = col <= row
    b_full = jnp.dot(causal.astype(F32), lf_msk, preferred_element_type=F32,
                     precision=lax.Precision.HIGHEST)
    b_col = jnp.sum(b_full, axis=1, keepdims=True)
    both = jnp.where(lane == 0, b_col, jnp.where(lane == 1, i_col, 0.0))
    both_t = both.T
    b_row = both_t[0:1, :]
    i_row = both_t[1:2, :]

    m_prev = m_scr[...]
    a_col = b_col + m_prev
    dmat = jnp.where(causal, b_col - b_row + i_row, -jnp.inf)
    m_t = jnp.maximum(a_col, jnp.max(dmat, axis=1, keepdims=True))
    w_intra = jnp.exp(dmat - m_t)
    w_inter = jnp.exp(a_col - m_t)

    q = q_ref[...]
    k = k_ref[...]
    v = v_ref[...]
    qscale = M_DQK ** -0.5
    s = lax.dot_general(q, k, (((1,), (1,)), ((), ())), preferred_element_type=F32) * qscale * w_intra
    qf = q.astype(F32) * qscale
    inter = jnp.dot(q, c_scr[...].astype(BF16), preferred_element_type=F32) * qscale
    num = w_inter * inter + jnp.dot(s.astype(BF16), v, preferred_element_type=F32)
    den = w_inter * jnp.sum(qf * n_scr[...], axis=1, keepdims=True) + jnp.sum(s, axis=1, keepdims=True)
    hh = num / jnp.maximum(jnp.abs(den), jnp.exp(-m_t))
    y = _rms(hh, gn_ref[...]) * _sigmoid(o_ref[...].astype(F32))
    hm_ref[...] = y.astype(BF16)

    m_new = m_t[ML - 1:ML, :]
    b_last = b_col[ML - 1:ML, :]
    w_state = jnp.exp(b_last - b_col + i_col - m_new)
    decay = jnp.exp(b_last + m_prev - m_new)
    kw = k.astype(F32) * w_state
    c_new = decay * c_scr[...] + lax.dot_general(kw.astype(BF16), v, (((0,), (0,)), ((), ())),
                                                 preferred_element_type=F32)
    n_new = decay * n_scr[...] + jnp.sum(kw, axis=0, keepdims=True)
    c_scr[...] = c_new
    n_scr[...] = n_new
    m_scr[...] = m_new

    @pl.when(c == NC - 1)
    def _():
        c_ref[...] = c_new
        n_ref[...] = n_new
        m_ref[...] = jnp.broadcast_to(m_new, m_ref.shape)


def _mlstm_prompt(z, gates, bias, gn):
    qb, vb = M_DQK, M_DV
    return pl.pallas_call(
        _mlstm_prompt_kernel,
        grid=(BATCH, M_HEADS, NC),
        in_specs=[
            pl.BlockSpec((ML, qb), lambda b, h, c: (b * NC + c, C_QM // qb + h)),
            pl.BlockSpec((ML, qb), lambda b, h, c: (b * NC + c, C_KM // qb + h)),
            pl.BlockSpec((ML, vb), lambda b, h, c: (b * NC + c, C_VM // vb + h)),
            pl.BlockSpec((ML, vb), lambda b, h, c: (b * NC + c, C_OM // vb + h)),
            pl.BlockSpec((ML, LANES), lambda b, h, c: (b * NC + c, 0)),
            pl.BlockSpec((1, LANES), lambda b, h, c: (0, 0)),
            pl.BlockSpec((1, vb), lambda b, h, c: (0, h)),
        ],
        out_specs=[
            pl.BlockSpec((ML, vb), lambda b, h, c: (b * NC + c, h)),
            pl.BlockSpec((None, None, M_DQK, M_DV), lambda b, h, c: (b, h, 0, 0)),
            pl.BlockSpec((None, None, 1, M_DQK), lambda b, h, c: (b, h, 0, 0)),
            pl.BlockSpec((None, None, 1, LANES), lambda b, h, c: (b, h, 0, 0)),
        ],
        out_shape=[
            jax.ShapeDtypeStruct((MP, M_HEADS * M_DV), BF16),
            jax.ShapeDtypeStruct((BATCH, M_HEADS, M_DQK, M_DV), F32),
            jax.ShapeDtypeStruct((BATCH, M_HEADS, 1, M_DQK), F32),
            jax.ShapeDtypeStruct((BATCH, M_HEADS, 1, LANES), F32),
        ],
        scratch_shapes=[pltpu.VMEM((M_DQK, M_DV), F32), pltpu.VMEM((1, M_DQK), F32), pltpu.VMEM((1, 1), F32)],
        compiler_params=_cparams(("parallel", "parallel", "arbitrary")),
        name="mlstm_prompt",
    )(z, z, z, z, gates, bias, gn)


def _mlstm_decode_kernel(q_ref, k_ref, v_ref, o_ref, qt_ref, kt_ref, gt_ref, bias_ref, m0_ref, gn_ref,
                         c0_ref, n0_ref, *rest, layer):
    if layer == 0:
        hm_ref, c_ref, n_ref, m_ref = rest
    else:
        _, hm_ref, c_ref, n_ref, m_ref = rest
    i = pl.program_id(0)
    h = pl.program_id(1)
    gates = gt_ref[...] + bias_ref[...]
    lane = lax.broadcasted_iota(jnp.int32, gates.shape, 1)
    log_f = pltpu.roll(_log_sigmoid(gates), LANES - M_HEADS, 1)
    a = log_f + m0_ref[...]
    m_t = jnp.maximum(a, gates)
    w_intra_all = jnp.exp(gates - m_t)
    w_inter_all = jnp.exp(a - m_t)
    floor_all = jnp.exp(-m_t)

    @pl.when(h == 0)
    def _():
        m_ref[...] = m_t

    def pick(arr):
        return jnp.sum(jnp.where(lane == h, arr, 0.0), axis=1, keepdims=True)

    wi = pick(w_intra_all)
    we = pick(w_inter_all)
    fl = pick(floor_all)
    qscale = M_DQK ** -0.5
    q = q_ref[...] * qscale
    k = k_ref[...]
    v = v_ref[...]
    n0 = n0_ref[...]
    s = jnp.sum(q * k, axis=1, keepdims=True) * wi
    den = we * jnp.sum(q * n0, axis=1, keepdims=True) + s
    dd = jnp.maximum(jnp.abs(den), fl)
    n_ref[...] = we * n0 + wi * k

    shift = lax.rem(LANES - i * TB, LANES)
    qt = pltpu.roll(qt_ref[...], shift, 1) * qscale
    kt = pltpu.roll(kt_ref[...], shift, 1)
    gn = gn_ref[...]
    sig_o = _sigmoid(o_ref[...])
    for j in range(TB):
        qc = qt[:, j:j + 1]
        kc = kt[:, j:j + 1]
        c0 = c0_ref[j]
        vj = v[j:j + 1, :]
        qc0 = jnp.sum(qc * c0, axis=0, keepdims=True)
        hrow = (we[j:j + 1, :] * qc0 + s[j:j + 1, :] * vj) / dd[j:j + 1, :]
        hm_ref[j:j + 1, :] = _rms(hrow, gn) * sig_o[j:j + 1, :]
        c_new = we[j:j + 1, :] * c0 + (wi[j:j + 1, :] * kc) * vj
        if layer == 0:
            c_ref[0, j] = c_new
            for d in range(1, DEPTH):
                c_ref[d, j] = jnp.zeros_like(c_new)
        else:
            c_ref[j] = c_new


def _mlstm_decode(layer, zs, qt, kt, gates_s, bias, m0p, gn, state_c, state_n, c_all):
    qb, vb = M_DQK, M_DV
    if layer == 0:
        c_spec = pl.BlockSpec((DEPTH, TB, None, M_DQK, M_DV), lambda i, h: (0, i, h, 0, 0))
        extra_in, extra_specs, aliases = (), [], {}
    else:
        c_spec = pl.BlockSpec((None, TB, None, M_DQK, M_DV), lambda i, h: (layer, i, h, 0, 0))
        extra_in, extra_specs, aliases = (c_all,), [pl.BlockSpec(memory_space=pl.ANY)], {12: 1}
    return pl.pallas_call(
        functools.partial(_mlstm_decode_kernel, layer=layer),
        grid=(MS // TB, M_HEADS),
        input_output_aliases=aliases,
        in_specs=[
            pl.BlockSpec((TB, qb), lambda i, h: (i, C_QM // qb + h)),
            pl.BlockSpec((TB, qb), lambda i, h: (i, C_KM // qb + h)),
            pl.BlockSpec((TB, vb), lambda i, h: (i, C_VM // vb + h)),
            pl.BlockSpec((TB, vb), lambda i, h: (i, C_OM // vb + h)),
            pl.BlockSpec((None, M_DQK, MS), lambda i, h: (h, 0, 0)),
            pl.BlockSpec((None, M_DQK, MS), lambda i, h: (h, 0, 0)),
            pl.BlockSpec((TB, LANES), lambda i, h: (i, 0)),
            pl.BlockSpec((1, LANES), lambda i, h: (0, 0)),
            pl.BlockSpec((TB, LANES), lambda i, h: (i, 0)),
            pl.BlockSpec((1, vb), lambda i, h: (0, h)),
            pl.BlockSpec((None, TB, None, M_DQK, M_DV), lambda i, h: (layer, i, h, 0, 0)),
            pl.BlockSpec((None, TB, M_DQK), lambda i, h: (layer, i, h)),
        ] + extra_specs,
        out_specs=[
            pl.BlockSpec((TB, vb), lambda i, h: (i, h)),
            c_spec,
            pl.BlockSpec((TB, M_DQK), lambda i, h: (i, h)),
            pl.BlockSpec((TB, LANES), lambda i, h: (i, 0)),
        ],
        out_shape=[
            jax.ShapeDtypeStruct((MS, M_HEADS * M_DV), F32),
            jax.ShapeDtypeStruct((DEPTH, MS, M_HEADS, M_DQK, M_DV), F32),
            jax.ShapeDtypeStruct((MS, M_HEADS * M_DQK), F32),
            jax.ShapeDtypeStruct((MS, LANES), F32),
        ],
        compiler_params=_cparams(("parallel", "arbitrary")),
        name="mlstm_decode",
    )(zs, zs, zs, zs, qt, kt, gates_s, bias, m0p, gn, state_c, state_n, *extra_in)


NB = SEQ // ATT_BLOCK
KVW = KV_HEADS * HEAD_DIM


def _swa_prompt_kernel(sink_ref, q_ref, kc_ref, kp_ref, vc_ref, vp_ref, o_ref):
    nb = pl.program_id(1)
    cols = GROUP * ATT_BLOCK
    sidx = lax.broadcasted_iota(jnp.int32, (2 * ATT_BLOCK, cols), 0)
    t = lax.broadcasted_iota(jnp.int32, (2 * ATT_BLOCK, cols), 1) & (ATT_BLOCK - 1)
    rel = t + ATT_BLOCK - sidx
    visible = jnp.logical_and(jnp.logical_and(rel >= 0, rel <= WINDOW),
                              jnp.logical_or(sidx >= ATT_BLOCK, nb > 0))
    bias = jnp.where(visible, 0.0, -jnp.inf)
    q = q_ref[...] * (HEAD_DIM ** -0.5)
    kk = jnp.concatenate([kp_ref[...], kc_ref[...]], axis=0).astype(BF16)
    vv_t = jnp.concatenate([vp_ref[...], vc_ref[...]], axis=0).T.astype(BF16)
    for g in range(KV_HEADS):
        qg = jnp.concatenate(
            [q[:, (g * GROUP + hh) * HEAD_DIM:(g * GROUP + hh + 1) * HEAD_DIM] for hh in range(GROUP)], axis=0)
        kg = kk[:, g * HEAD_DIM:(g + 1) * HEAD_DIM]
        vg_t = vv_t[g * HEAD_DIM:(g + 1) * HEAD_DIM, :]
        sink = jnp.concatenate(
            [jnp.full((1, ATT_BLOCK), sink_ref[g * GROUP + hh], F32) for hh in range(GROUP)], axis=1)
        s = lax.dot_general(kg, qg, (((1,), (1,)), ((), ())), preferred_element_type=F32) + bias
        mx = jnp.maximum(jnp.max(s, axis=0, keepdims=True), sink)
        p = jnp.exp(s - mx)
        denom = jnp.sum(p, axis=0, keepdims=True) + jnp.exp(sink - mx)
        o_t = jnp.dot(vg_t, p.astype(BF16), preferred_element_type=F32) / denom
        og = jnp.concatenate([o_t[:, hh * ATT_BLOCK:(hh + 1) * ATT_BLOCK].T for hh in range(GROUP)], axis=1)
        o_ref[:, g * GROUP * HEAD_DIM:(g + 1) * GROUP * HEAD_DIM] = og.astype(BF16)


def _swa_prompt(z, kv, sinks):
    qw = N_HEADS * HEAD_DIM
    return pl.pallas_call(
        _swa_prompt_kernel,
        grid=(BATCH, NB),
        in_specs=[
            pl.BlockSpec(memory_space=pltpu.SMEM),
            pl.BlockSpec((ATT_BLOCK, qw), lambda b, n: (b * NB + n, C_QA // qw)),
            pl.BlockSpec((ATT_BLOCK, KVW), lambda b, n: (b * NB + n, 0)),
            pl.BlockSpec((ATT_BLOCK, KVW), lambda b, n: (b * NB + jnp.maximum(n - 1, 0), 0)),
            pl.BlockSpec((ATT_BLOCK, KVW), lambda b, n: (b * NB + n, 1)),
            pl.BlockSpec((ATT_BLOCK, KVW), lambda b, n: (b * NB + jnp.maximum(n - 1, 0), 1)),
        ],
        out_specs=pl.BlockSpec((ATT_BLOCK, qw), lambda b, n: (b * NB + n, 0)),
        out_shape=jax.ShapeDtypeStruct((MP, qw), BF16),
        compiler_params=_cparams(("parallel", "arbitrary")),
        name="swa_prompt",
    )(sinks, z, kv, kv, kv, kv)


def _swa_decode_kernel(q_ref, kn_ref, vn_ref, knf_ref, vnf_ref, kb_ref, vb_ref, sink_ref, *rest, layer):
    if layer == 0:
        o_ref, kc_ref, vc_ref = rest
    else:
        _, _, o_ref, kc_ref, vc_ref = rest

    def put(ref, j, rows, val):
        if layer == 0:
            ref[0, j, rows, :] = val
        else:
            ref[j, rows, :] = val

    for j in range(TB):
        put(kc_ref, j, slice(0, WINDOW - 1), kb_ref[j, 1:WINDOW, :])
        put(kc_ref, j, slice(WINDOW - 1, WINDOW), knf_ref[j:j + 1, :])
        put(vc_ref, j, slice(0, WINDOW - 1), vb_ref[j, 1:WINDOW, :])
        put(vc_ref, j, slice(WINDOW - 1, WINDOW), vnf_ref[j:j + 1, :])
    if layer == 0:
        for d in range(1, DEPTH):
            kc_ref[d] = jnp.zeros(kc_ref.shape[1:], F32)
            vc_ref[d] = jnp.zeros(vc_ref.shape[1:], F32)

    rows, cols = TB * GROUP, TB * WINDOW
    own = (lax.broadcasted_iota(jnp.int32, (rows, cols), 0) // GROUP
           == lax.broadcasted_iota(jnp.int32, (rows, cols), 1) // WINDOW)
    bias = jnp.where(own, 0.0, -jnp.inf)
    scale = HEAD_DIM ** -0.5
    kstack = kb_ref[...].reshape(cols, KVW).astype(BF16)
    vstack = vb_ref[...].reshape(cols, KVW).astype(BF16)
    for g in range(KV_HEADS):
        qg = q_ref[:, g * GROUP:(g + 1) * GROUP, :].reshape(rows, HEAD_DIM) * scale
        kn = jnp.broadcast_to(kn_ref[:, g:g + 1, :], (TB, GROUP, HEAD_DIM)).reshape(rows, HEAD_DIM)
        vn = jnp.broadcast_to(vn_ref[:, g:g + 1, :], (TB, GROUP, HEAD_DIM)).reshape(rows, HEAD_DIM)
        sink = jnp.tile(sink_ref[g * GROUP:(g + 1) * GROUP, 0:1], (TB, 1))
        s_c = lax.dot_general(qg.astype(BF16), kstack[:, g * HEAD_DIM:(g + 1) * HEAD_DIM],
                              (((1,), (1,)), ((), ())), preferred_element_type=F32) + bias
        s_n = jnp.sum(qg * kn, axis=1, keepdims=True)
        mx = jnp.maximum(jnp.maximum(jnp.max(s_c, axis=1, keepdims=True), s_n), sink)
        p_c = jnp.exp(s_c - mx)
        p_n = jnp.exp(s_n - mx)
        denom = jnp.sum(p_c, axis=1, keepdims=True) + p_n + jnp.exp(sink - mx)
        o = jnp.dot(p_c.astype(BF16), vstack[:, g * HEAD_DIM:(g + 1) * HEAD_DIM], preferred_element_type=F32)
        o = (o + p_n * vn) / denom
        o_ref[:, g * GROUP:(g + 1) * GROUP, :] = o.reshape(TB, GROUP, HEAD_DIM)


def _swa_decode(layer, q3, kn3, vn3, knf, vnf, kbuf, vbuf, sinks_b, k_all, v_all):
    if layer == 0:
        cache_spec = pl.BlockSpec((DEPTH, TB, WINDOW, KVW), lambda i: (0, i, 0, 0))
        extra_in, extra_specs, aliases = (), [], {}
    else:
        cache_spec = pl.BlockSpec((None, TB, WINDOW, KVW), lambda i: (layer, i, 0, 0))
        extra_in = (k_all, v_all)
        extra_specs = [pl.BlockSpec(memory_space=pl.ANY), pl.BlockSpec(memory_space=pl.ANY)]
        aliases = {8: 1, 9: 2}
    return pl.pallas_call(
        functools.partial(_swa_decode_kernel, layer=layer),
        grid=(MS // TB,),
        input_output_aliases=aliases,
        in_specs=[
            pl.BlockSpec((TB, N_HEADS, HEAD_DIM), lambda i: (i, 0, 0)),
            pl.BlockSpec((TB, KV_HEADS, HEAD_DIM), lambda i: (i, 0, 0)),
            pl.BlockSpec((TB, KV_HEADS, HEAD_DIM), lambda i: (i, 0, 0)),
            pl.BlockSpec((TB, KVW), lambda i: (i, 0)),
            pl.BlockSpec((TB, KVW), lambda i: (i, 0)),
            pl.BlockSpec((None, TB, WINDOW, KVW), lambda i: (layer, i, 0, 0)),
            pl.BlockSpec((None, TB, WINDOW, KVW), lambda i: (layer, i, 0, 0)),
            pl.BlockSpec((N_HEADS, LANES), lambda i: (0, 0)),
        ] + extra_specs,
        out_specs=[
            pl.BlockSpec((TB, N_HEADS, HEAD_DIM), lambda i: (i, 0, 0)),
            cache_spec,
            cache_spec,
        ],
        out_shape=[
            jax.ShapeDtypeStruct((MS, N_HEADS, HEAD_DIM), F32),
            jax.ShapeDtypeStruct((DEPTH, MS, WINDOW, KVW), F32),
            jax.ShapeDtypeStruct((DEPTH, MS, WINDOW, KVW), F32),
        ],
        compiler_params=_cparams(("parallel",)),
        name="swa_decode",
    )(q3, kn3, vn3, knf, vnf, kbuf, vbuf, sinks_b, *extra_in)


def _merge_kernel(hm_ref, ha_ref, wbm_ref, wba_ref, gm_ref, ga_ref, o_ref):
    a = jnp.dot(hm_ref[...], wbm_ref[...], preferred_element_type=F32)
    b = jnp.dot(ha_ref[...], wba_ref[...], preferred_element_type=F32)
    o_ref[...] = (_sigmoid(gm_ref[...].astype(F32)) * a + _sigmoid(ga_ref[...].astype(F32)) * b).astype(BF16)


def _merge(hm, ha, wbm, wba, z):
    return pl.pallas_call(
        _merge_kernel,
        grid=(M // TM, D_MODEL // TN),
        in_specs=[
            pl.BlockSpec((TM, D_MODEL), lambda i, j: (i, 0)),
            pl.BlockSpec((TM, D_MODEL), lambda i, j: (i, 0)),
            pl.BlockSpec((D_MODEL, TN), lambda i, j: (0, j)),
            pl.BlockSpec((D_MODEL, TN), lambda i, j: (0, j)),
            pl.BlockSpec((TM, TN), lambda i, j: (i, C_GM // TN + j)),
            pl.BlockSpec((TM, TN), lambda i, j: (i, C_GA // TN + j)),
        ],
        out_specs=pl.BlockSpec((TM, TN), lambda i, j: (i, j)),
        out_shape=jax.ShapeDtypeStruct((M, D_MODEL), BF16),
        compiler_params=_cparams(("parallel", "arbitrary")),
        name="merge",
    )(hm, ha, wbm, wba, z, z)


def _mm_res_kernel(a_ref, w_ref, r_ref, o_ref):
    o_ref[...] = r_ref[...] + jnp.dot(a_ref[...], w_ref[...], preferred_element_type=F32)


def _mm_res(a, w, res, tm, tn, name):
    kdim = a.shape[1]
    n = w.shape[1]
    return pl.pallas_call(
        _mm_res_kernel,
        grid=(M // tm, n // tn),
        in_specs=[
            pl.BlockSpec((tm, kdim), lambda i, j: (i, 0)),
            pl.BlockSpec((kdim, tn), lambda i, j: (0, j)),
            pl.BlockSpec((tm, tn), lambda i, j: (i, j)),
        ],
        out_specs=pl.BlockSpec((tm, tn), lambda i, j: (i, j)),
        out_shape=jax.ShapeDtypeStruct((M, n), F32),
        compiler_params=_cparams(("parallel", "arbitrary")),
        name=name,
    )(a, w, res)


def _ffn_up_kernel(x_ref, g_ref, wg_ref, wu_ref, act_ref, h_scr):
    @pl.when(pl.program_id(1) == 0)
    def _():
        h_scr[...] = _rms(x_ref[...], g_ref[...]).astype(BF16)

    h = h_scr[...]
    a = jnp.dot(h, wg_ref[...], preferred_element_type=F32)
    u = jnp.dot(h, wu_ref[...], preferred_element_type=F32)
    act_ref[...] = ((a * _sigmoid(a)) * u).astype(BF16)


def _ffn_up(x, g, wg, wu):
    return pl.pallas_call(
        _ffn_up_kernel,
        grid=(M // TM, D_FF // TN),
        in_specs=[
            pl.BlockSpec((TM, D_MODEL), lambda i, j: (i, 0)),
            pl.BlockSpec((1, D_MODEL), lambda i, j: (0, 0)),
            pl.BlockSpec((D_MODEL, TN), lambda i, j: (0, j)),
            pl.BlockSpec((D_MODEL, TN), lambda i, j: (0, j)),
        ],
        out_specs=pl.BlockSpec((TM, TN), lambda i, j: (i, j)),
        out_shape=jax.ShapeDtypeStruct((M, D_FF), BF16),
        scratch_shapes=[pltpu.VMEM((TM, D_MODEL), BF16)],
        compiler_params=_cparams(("parallel", "arbitrary")),
        name="ffn_up",
    )(x, g, wg, wu)


def _router_kernel(x_ref, g_ref, wr_ref, ids_ref, gates_ref, cnt_ref, run_scr):
    @pl.when(pl.program_id(0) == 0)
    def _():
        run_scr[...] = jnp.zeros_like(run_scr)

    h = _rms(x_ref[...], g_ref[...])
    logits = jnp.dot(h, wr_ref[...], preferred_element_type=F32, precision=lax.Precision.HIGHEST)
    lane = lax.broadcasted_iota(jnp.int32, logits.shape, 1)
    lg = jnp.where(lane < N_EXPERTS, logits, -jnp.inf)
    v1 = jnp.max(lg, axis=1, keepdims=True)
    i1 = jnp.min(jnp.where(lg == v1, lane, LANES), axis=1, keepdims=True)
    lg2 = jnp.where(lane == i1, -jnp.inf, lg)
    v2 = jnp.max(lg2, axis=1, keepdims=True)
    i2 = jnp.min(jnp.where(lg2 == v2, lane, LANES), axis=1, keepdims=True)
    e2 = jnp.exp(v2 - v1)
    g1 = 1.0 / (1.0 + e2)
    g2 = e2 / (1.0 + e2)
    gates_ref[...] = jnp.where(lane == 0, g1, jnp.where(lane == 1, g2, 0.0))

    picks = jnp.where(jnp.logical_or(lane == i1, lane == i2), 1.0, 0.0)
    tm = picks.shape[0]
    earlier = lax.broadcasted_iota(jnp.int32, (tm, tm), 1) < lax.broadcasted_iota(jnp.int32, (tm, tm), 0)
    before = jnp.dot(jnp.where(earlier, 1.0, 0.0).astype(BF16), picks.astype(BF16),
                     preferred_element_type=F32) + run_scr[...]
    r1 = jnp.sum(jnp.where(lane == i1, before, 0.0), axis=1, keepdims=True).astype(jnp.int32)
    r2 = jnp.sum(jnp.where(lane == i2, before, 0.0), axis=1, keepdims=True).astype(jnp.int32)
    ids_ref[...] = jnp.where(lane == 0, i1, jnp.where(lane == 1, i2, jnp.where(lane == 2, r1, jnp.where(lane == 3, r2, 0))))
    total = run_scr[...] + jnp.sum(picks, axis=0, keepdims=True)
    run_scr[...] = total
    cnt_ref[...] = total


def _router(x, g, wr_pad):
    return pl.pallas_call(
        _router_kernel,
        grid=(M // TM,),
        in_specs=[
            pl.BlockSpec((TM, D_MODEL), lambda i: (i, 0)),
            pl.BlockSpec((1, D_MODEL), lambda i: (0, 0)),
            pl.BlockSpec((D_MODEL, LANES), lambda i: (0, 0)),
        ],
        out_specs=[
            pl.BlockSpec((TM, LANES), lambda i: (i, 0)),
            pl.BlockSpec((TM, LANES), lambda i: (i, 0)),
            pl.BlockSpec((1, LANES), lambda i: (0, 0)),
        ],
        out_shape=[
            jax.ShapeDtypeStruct((M, LANES), jnp.int32),
            jax.ShapeDtypeStruct((M, LANES), F32),
            jax.ShapeDtypeStruct((1, LANES), F32),
        ],
        scratch_shapes=[pltpu.VMEM((1, LANES), F32)],
        compiler_params=_cparams(("arbitrary",)),
        name="router",
    )(x, g, wr_pad)


def _row_copy(src_hbm, row, buf, r, sem):
    return pltpu.make_async_copy(src_hbm.at[pl.ds(row, 1), :], buf.at[pl.ds(r, 1), :], sem)


def _slot_gather_kernel(tok_ref, nu_ref, x_hbm, g_ref, xs_ref, buf, sem):
    m = pl.program_id(0)
    base = m * MOE_UP_TM

    @pl.when(m < nu_ref[0])
    def _():
        def issue(r, carry):
            _row_copy(x_hbm, tok_ref[base + r], buf, r, sem).start()
            return carry
        lax.fori_loop(0, MOE_UP_TM, issue, 0, unroll=8)

        def wait(r, carry):
            _row_copy(x_hbm, 0, buf, r, sem).wait()
            return carry
        lax.fori_loop(0, MOE_UP_TM, wait, 0, unroll=8)
        xs_ref[...] = _rms(buf[...], g_ref[...]).astype(BF16)

    @pl.when(m >= nu_ref[0])
    def _():
        xs_ref[...] = jnp.zeros_like(xs_ref)


def _slot_gather(slot_token, n_used, x, g):
    return pl.pallas_call(
        _slot_gather_kernel,
        grid_spec=pltpu.PrefetchScalarGridSpec(
            num_scalar_prefetch=2,
            grid=(N_SLOTS // MOE_UP_TM,),
            in_specs=[pl.BlockSpec(memory_space=pl.ANY),
                      pl.BlockSpec((1, D_MODEL), lambda m, tok, nu: (0, 0))],
            out_specs=pl.BlockSpec((MOE_UP_TM, D_MODEL), lambda m, tok, nu: (m, 0)),
            scratch_shapes=[pltpu.VMEM((MOE_UP_TM, D_MODEL), F32), pltpu.SemaphoreType.DMA(())],
        ),
        out_shape=jax.ShapeDtypeStruct((N_SLOTS, D_MODEL), BF16),
        compiler_params=_cparams(("arbitrary",)),
        name="slot_gather",
    )(slot_token, n_used, x, g)


def _combine_kernel(p1_ref, p2_ref, x_ref, y_hbm, gate_ref, g_ref, o_ref, buf1, buf2, sem, *, final_norm):
    base = pl.program_id(0) * COMBINE_ROWS

    def issue(r, carry):
        _row_copy(y_hbm, p1_ref[base + r], buf1, r, sem).start()
        _row_copy(y_hbm, p2_ref[base + r], buf2, r, sem).start()
        return carry
    lax.fori_loop(0, COMBINE_ROWS, issue, 0, unroll=8)

    def wait(r, carry):
        _row_copy(y_hbm, 0, buf1, r, sem).wait()
        _row_copy(y_hbm, 0, buf2, r, sem).wait()
        return carry
    lax.fori_loop(0, COMBINE_ROWS, wait, 0, unroll=8)
    gates = gate_ref[...]
    out = x_ref[...] + (gates[:, 0:1] * buf1[...] + gates[:, 1:2] * buf2[...])
    o_ref[...] = _rms(out, g_ref[...]) if final_norm else out


def _combine(pos1, pos2, x, y, gates, g_final):
    final_norm = g_final is not None
    g = g_final if final_norm else jnp.ones((1, D_MODEL), F32)
    return pl.pallas_call(
        functools.partial(_combine_kernel, final_norm=final_norm),
        grid_spec=pltpu.PrefetchScalarGridSpec(
            num_scalar_prefetch=2,
            grid=(M // COMBINE_ROWS,),
            in_specs=[pl.BlockSpec((COMBINE_ROWS, D_MODEL), lambda i, p1, p2: (i, 0)),
                      pl.BlockSpec(memory_space=pl.ANY),
                      pl.BlockSpec((COMBINE_ROWS, LANES), lambda i, p1, p2: (i, 0)),
                      pl.BlockSpec((1, D_MODEL), lambda i, p1, p2: (0, 0))],
            out_specs=pl.BlockSpec((COMBINE_ROWS, D_MODEL), lambda i, p1, p2: (i, 0)),
            scratch_shapes=[pltpu.VMEM((COMBINE_ROWS, D_MODEL), F32), pltpu.VMEM((COMBINE_ROWS, D_MODEL), F32),
                            pltpu.SemaphoreType.DMA(())],
        ),
        out_shape=jax.ShapeDtypeStruct((M, D_MODEL), F32),
        compiler_params=_cparams(("arbitrary",)),
        name="combine",
    )(pos1, pos2, x, y, gates, g)


def _moe_up_kernel(te_ref, nu_ref, x_ref, wg_ref, wu_ref, act_ref, wg_scr, wu_scr):
    m = pl.program_id(1)

    @pl.when(m < nu_ref[0])
    def _():
        prev = te_ref[jnp.maximum(m - 1, 0)]

        @pl.when(jnp.logical_or(m == 0, te_ref[m] != prev))
        def _():
            wg_scr[...] = wg_ref[...].astype(BF16)
            wu_scr[...] = wu_ref[...].astype(BF16)

        x = x_ref[...]
        a = jnp.dot(x, wg_scr[...], preferred_element_type=F32)
        u = jnp.dot(x, wu_scr[...], preferred_element_type=F32)
        act_ref[...] = ((a * _sigmoid(a)) * u).astype(BF16)

    @pl.when(m >= nu_ref[0])
    def _():
        act_ref[...] = jnp.zeros_like(act_ref)


def _moe_up(tile_expert, n_used, xs, wg, wu):
    return pl.pallas_call(
        _moe_up_kernel,
        grid_spec=pltpu.PrefetchScalarGridSpec(
            num_scalar_prefetch=2,
            grid=(E_FF // MOE_TF, N_SLOTS // MOE_UP_TM),
            in_specs=[
                pl.BlockSpec((MOE_UP_TM, D_MODEL), lambda j, m, te, nu: (m, 0)),
                pl.BlockSpec((None, D_MODEL, MOE_TF), lambda j, m, te, nu: (te[m], 0, j)),
                pl.BlockSpec((None, D_MODEL, MOE_TF), lambda j, m, te, nu: (te[m], 0, j)),
            ],
            out_specs=pl.BlockSpec((MOE_UP_TM, MOE_TF), lambda j, m, te, nu: (m, j)),
            scratch_shapes=[pltpu.VMEM((D_MODEL, MOE_TF), BF16), pltpu.VMEM((D_MODEL, MOE_TF), BF16)],
        ),
        out_shape=jax.ShapeDtypeStruct((N_SLOTS, E_FF), BF16),
        compiler_params=_cparams(("arbitrary", "arbitrary")),
        name="moe_up",
    )(tile_expert, n_used, xs, wg, wu)


def _moe_down_kernel(te_ref, nu_ref, a_ref, wd_ref, y_ref, wd_scr):
    m = pl.program_id(1)

    @pl.when(m < nu_ref[0])
    def _():
        prev = te_ref[jnp.maximum(m - 1, 0)]

        @pl.when(jnp.logical_or(m == 0, te_ref[m] != prev))
        def _():
            wd_scr[...] = wd_ref[...].astype(BF16)

        y_ref[...] = jnp.dot(a_ref[...], wd_scr[...], preferred_element_type=F32)

    @pl.when(m >= nu_ref[0])
    def _():
        y_ref[...] = jnp.zeros_like(y_ref)


def _moe_down(tile_expert, n_used, act, wd):
    return pl.pallas_call(
        _moe_down_kernel,
        grid_spec=pltpu.PrefetchScalarGridSpec(
            num_scalar_prefetch=2,
            grid=(D_MODEL // MOE_TN, N_SLOTS // MOE_DN_TM),
            in_specs=[
                pl.BlockSpec((MOE_DN_TM, E_FF), lambda j, m, te, nu: (m, 0)),
                pl.BlockSpec((None, E_FF, MOE_TN), lambda j, m, te, nu: (te[m], 0, j)),
            ],
            out_specs=pl.BlockSpec((MOE_DN_TM, MOE_TN), lambda j, m, te, nu: (m, j)),
            scratch_shapes=[pltpu.VMEM((E_FF, MOE_TN), BF16)],
        ),
        out_shape=jax.ShapeDtypeStruct((N_SLOTS, D_MODEL), F32),
        compiler_params=_cparams(("arbitrary", "arbitrary")),
        name="moe_down",
    )(tile_expert, n_used, act, wd)


def _final_norm_kernel(x_ref, g_ref, o_ref):
    o_ref[...] = _rms(x_ref[...], g_ref[...])


def _final_norm(x, g):
    spec = pl.BlockSpec((COMBINE_ROWS, D_MODEL), lambda i: (i, 0))
    return pl.pallas_call(
        _final_norm_kernel,
        grid=(M // COMBINE_ROWS,),
        in_specs=[spec, pl.BlockSpec((1, D_MODEL), lambda i: (0, 0))],
        out_specs=spec,
        out_shape=jax.ShapeDtypeStruct((M, D_MODEL), F32),
        compiler_params=_cparams(("parallel",)),
        name="final_norm",
    )(x, g)


def _tile_map(ends, tm):
    n_tiles = N_SLOTS // tm
    n_used = (ends[-1] // tm).astype(jnp.int32)
    tile_start = jnp.arange(n_tiles, dtype=jnp.int32) * tm
    tile_expert = jnp.sum((tile_start[:, None] >= ends[None, :]).astype(jnp.int32), axis=1)
    last = jnp.sum((jnp.maximum(n_used - 1, 0) * tm >= ends).astype(jnp.int32))
    tile_expert = jnp.where(jnp.arange(n_tiles) < n_used, tile_expert, last).astype(jnp.int32)
    return tile_expert, n_used.reshape(1)


def _route(ids, counts):
    counts = counts[0, :N_EXPERTS].astype(jnp.int32)
    padded = ((counts + MOE_PAD - 1) // MOE_PAD) * MOE_PAD
    ends = jnp.cumsum(padded)
    starts = ends - padded
    pos1 = starts[ids[:, 0]] + ids[:, 2]
    pos2 = starts[ids[:, 1]] + ids[:, 3]
    tok = jnp.arange(M, dtype=jnp.int32)
    slot_token = jnp.zeros((N_SLOTS,), jnp.int32).at[jnp.concatenate([pos1, pos2])].set(jnp.concatenate([tok, tok]))
    return slot_token, _tile_map(ends, MOE_UP_TM), _tile_map(ends, MOE_DN_TM), pos1, pos2


def _rope_tables():
    half = ROT_DIM // 2
    pos = jnp.concatenate([jnp.tile(jnp.arange(SEQ, dtype=jnp.int32), BATCH),
                           jnp.full((MS,), PAST_LEN, jnp.int32)])
    inv = jnp.power(ROPE_THETA, -jnp.arange(half, dtype=F32) / half)
    ang = pos.astype(F32)[:, None] * inv[None, :]
    cos, sin = jnp.cos(ang), jnp.sin(ang)
    pad = HEAD_DIM - ROT_DIM
    one = jnp.ones((M, pad), F32)
    zero = jnp.zeros((M, pad), F32)
    zh = jnp.zeros((M, half), F32)
    cs = jnp.concatenate([cos, cos, one], axis=1)
    sa = jnp.concatenate([-sin, zh, zero], axis=1)
    sb = jnp.concatenate([zh, sin, zero], axis=1)
    rep = LANES // HEAD_DIM
    return jnp.tile(cs, (1, rep)), jnp.tile(sa, (1, rep)), jnp.tile(sb, (1, rep))


def _pad_lanes(a):
    return jnp.pad(a, ((0, 0), (0, LANES - a.shape[1])))


def kernel(x_prompt, x_sample, state_mlstm_C, state_mlstm_n, state_mlstm_m, cache_swa_k, cache_swa_v, norm_mix_g, w_in, b_igate, b_fgate, mlstm_norm_g, attn_sinks, w_branch_m, w_branch_a, w_out, norm_ffn_g, w_gate_dense, w_up_dense, w_down_dense, w_router, w_gate_moe, w_up_moe, w_down_moe, norm_final_g):
    x = jnp.concatenate([x_prompt.reshape(MP, D_MODEL), x_sample.reshape(MS, D_MODEL)], axis=0)
    cs, sa, sb = _rope_tables()
    state_n = state_mlstm_n.reshape(DEPTH, MS, M_HEADS * M_DQK)
    kbuf = cache_swa_k.reshape(DEPTH, MS, WINDOW, KVW)
    vbuf = cache_swa_v.reshape(DEPTH, MS, WINDOW, KVW)
    outs = {name: [] for name in ("Cp", "np", "mp", "kp", "vp", "ns", "ms")}
    y_final = None
    c_all = k_all = v_all = None

    for l in range(DEPTH):
        w_l = w_in[l]
        w_al = jnp.concatenate([w_l[:, :R_GATES], w_l[:, R_QA:], jnp.zeros((D_MODEL, N_ZP - N_Z), F32)],
                               axis=1).astype(BF16)
        w_gate = _pad_lanes(w_l[:, R_GATES:R_QA]).astype(BF16)
        bias = _pad_lanes(jnp.concatenate([b_igate[l], b_fgate[l]])[None, :])
        gn = mlstm_norm_g[l][None, :]
        sinks_b = jnp.broadcast_to(attn_sinks[l][:, None], (N_HEADS, LANES))

        z, kv, gates = _inproj(x, norm_mix_g[l][None, :], w_al, w_gate, cs, sa, sb)

        hm_p, c_p, n_p, m_p = _mlstm_prompt(z, gates, bias, gn)
        ha_p = _swa_prompt(z, kv, attn_sinks[l])
        kv_p = kv[:MP].reshape(BATCH, SEQ, 2, KV_HEADS, HEAD_DIM)[:, SEQ - WINDOW:]
        outs["Cp"].append(c_p)
        outs["np"].append(n_p.reshape(BATCH, M_HEADS, M_DQK))
        outs["mp"].append(m_p[:, :, 0, 0])
        outs["kp"].append(kv_p[:, :, 0])
        outs["vp"].append(kv_p[:, :, 1])

        zs = z[MP:].astype(F32)
        kv_s = kv[MP:]
        qt = zs[:, C_QM:C_KM].T.reshape(M_HEADS, M_DQK, MS)
        kt = zs[:, C_KM:C_VM].T.reshape(M_HEADS, M_DQK, MS)
        m0p = _pad_lanes(state_mlstm_m[l])
        hm_s, c_all, n_s, m_s = _mlstm_decode(l, zs, qt, kt, gates[MP:], bias, m0p, gn, state_mlstm_C, state_n, c_all)
        kn = kv_s[:, :KVW]
        vn = kv_s[:, KVW:]
        ha_s, k_all, v_all = _swa_decode(
            l, zs[:, C_QA:C_KA].reshape(MS, N_HEADS, HEAD_DIM), kn.reshape(MS, KV_HEADS, HEAD_DIM),
            vn.reshape(MS, KV_HEADS, HEAD_DIM), kn, vn, kbuf, vbuf, sinks_b, k_all, v_all)
        outs["ns"].append(n_s.reshape(MS, M_HEADS, M_DQK))
        outs["ms"].append(m_s[:, :M_HEADS])

        hm = jnp.concatenate([hm_p, hm_s.astype(BF16)], axis=0)
        ha = jnp.concatenate([ha_p, ha_s.reshape(MS, N_HEADS * HEAD_DIM).astype(BF16)], axis=0)
        merged = _merge(hm, ha, w_branch_m[l].astype(BF16), w_branch_a[l].astype(BF16), z)
        x = _mm_res(merged, w_out[l].astype(BF16), x, TM, TN, "out_proj")

        if l % 2 == 0:
            jd = l // 2
            act = _ffn_up(x, norm_ffn_g[l][None, :], w_gate_dense[jd].astype(BF16), w_up_dense[jd].astype(BF16))
            x = _mm_res(act, w_down_dense[jd].astype(BF16), x, TM, TN, "ffn_down")
        else:
            jm = l // 2
            g_ffn = norm_ffn_g[l][None, :]
            ids, gts, counts = _router(x, g_ffn, _pad_lanes(w_router[jm]))
            slot_token, (te_up, nu_up), (te_dn, nu_dn), pos1, pos2 = _route(ids, counts)
            xs = _slot_gather(slot_token, nu_up, x, g_ffn)
            act = _moe_up(te_up, nu_up, xs, w_gate_moe[jm], w_up_moe[jm])
            y = _moe_down(te_dn, nu_dn, act, w_down_moe[jm])
            if l == DEPTH - 1:
                y_final = _combine(pos1, pos2, x, y, gts, norm_final_g[None, :])
            else:
                x = _combine(pos1, pos2, x, y, gts, None)

    if y_final is None:
        y_final = _final_norm(x, norm_final_g[None, :])
    y_prompt = y_final[:MP].reshape(BATCH, SEQ, D_MODEL)
    y_sample = y_final[MP:].reshape(MS, 1, D_MODEL)
    st = lambda name: jnp.stack(outs[name])
    return (y_prompt, y_sample, st("Cp"), st("np"), st("mp"), st("kp"), st("vp"),
            c_all, st("ns"), st("ms"),
            k_all.reshape(DEPTH, MS, WINDOW, KV_HEADS, HEAD_DIM), v_all.reshape(DEPTH, MS, WINDOW, KV_HEADS, HEAD_DIM))
```

```python
import functools

import jax
import jax.numpy as jnp
import numpy as np
from jax import lax
from jax.experimental import pallas as pl
from jax.experimental.pallas import tpu as pltpu

F32 = jnp.float32
BF16 = jnp.bfloat16

D_MODEL = 2048
BATCH = 4
SEQ = 2048
DEPTH = 2
DEC_BATCH = 128
PAST_LEN = 8192
M_HEADS = 4
M_DQK = 256
M_DV = 512
N_HEADS = 32
KV_HEADS = 4
HEAD_DIM = 64
GROUP = N_HEADS // KV_HEADS
ROT_DIM = HEAD_DIM // 4
ROPE_THETA = 500000.0
WINDOW = 128
ATT_BLOCK = 128
D_FF = 5632
N_EXPERTS = 8
TOP_K = 2
E_FF = 7168
EPS = 1e-6

MP = BATCH * SEQ
MS = DEC_BATCH
M = MP + MS

C_QM = 0
C_KM = C_QM + M_HEADS * M_DQK
C_VM = C_KM + M_HEADS * M_DQK
C_OM = C_VM + M_HEADS * M_DV
C_QA = C_OM + M_HEADS * M_DV
C_KA = C_QA + N_HEADS * HEAD_DIM
C_VA = C_KA + KV_HEADS * HEAD_DIM
C_GM = C_VA + KV_HEADS * HEAD_DIM
C_GA = C_GM + D_MODEL
N_Z = C_GA + D_MODEL
R_GATES = 2 * M_HEADS * M_DQK + 2 * M_HEADS * M_DV
R_QA = R_GATES + 2 * M_HEADS

LANES = 128
BF16_SUBLANES = 16
TM = 1040
TN = 512
ML = 256
TB = 8
MOE_PAD = 512
MOE_UP_TM = MOE_PAD
MOE_DN_TM = MOE_PAD
MOE_TF = 512
MOE_TN = 512
COMBINE_ROWS = TM // 2
N_SLOTS = -(-(TOP_K * M + N_EXPERTS * (MOE_PAD - 1)) // MOE_PAD) * MOE_PAD
VMEM_LIMIT = 56 * 1024 * 1024


def _cparams(sem, vmem=VMEM_LIMIT):
    return pltpu.CompilerParams(dimension_semantics=sem, vmem_limit_bytes=vmem)


def _rms(x, g):
    ms = jnp.mean(x * x, axis=-1, keepdims=True)
    return (x * lax.rsqrt(ms + EPS)) * g


def _sigmoid(x):
    return 1.0 / (1.0 + jnp.exp(-x))


def _log_sigmoid(x):
    return jnp.minimum(x, 0.0) - jnp.log(1.0 + jnp.exp(-jnp.abs(x)))


IN_TN = 1024
KV_COLS = C_GM - C_KA
J_QA0 = C_QA // IN_TN
J_KV = C_KA // IN_TN
N_ZP = -(-N_Z // IN_TN) * IN_TN
assert C_QA % IN_TN == 0 and C_KA % IN_TN == 0 and KV_COLS <= IN_TN


def _rope(acc, cs, sa, sb):
    n = acc.shape[1]
    return acc * cs + pltpu.roll(acc, n - ROT_DIM // 2, 1) * sa + pltpu.roll(acc, ROT_DIM // 2, 1) * sb


def _inproj_kernel(x_ref, g_ref, w_ref, wgate_ref, cs_ref, sa_ref, sb_ref, z_ref, kv_ref, gt_ref, h_scr):
    j = pl.program_id(1)

    @pl.when(j == 0)
    def _():
        h = _rms(x_ref[...], g_ref[...]).astype(BF16)
        h_scr[...] = h
        gt_ref[...] = jnp.dot(h, wgate_ref[...], preferred_element_type=F32)

    acc = jnp.dot(h_scr[...], w_ref[...], preferred_element_type=F32)
    z_ref[...] = acc.astype(BF16)
    reps = IN_TN // LANES
    is_q = jnp.logical_and(j >= J_QA0, j < J_KV)
    is_kv = j == J_KV

    @pl.when(is_q)
    def _():
        cs = jnp.tile(cs_ref[...], (1, reps))
        sa = jnp.tile(sa_ref[...], (1, reps))
        sb = jnp.tile(sb_ref[...], (1, reps))
        z_ref[...] = _rope(acc, cs, sa, sb).astype(BF16)

    @pl.when(is_kv)
    def _():
        is_k = lax.broadcasted_iota(jnp.int32, acc.shape, 1) < (C_VA - C_KA)
        cs = jnp.where(is_k, jnp.tile(cs_ref[...], (1, reps)), 1.0)
        sa = jnp.where(is_k, jnp.tile(sa_ref[...], (1, reps)), 0.0)
        sb = jnp.where(is_k, jnp.tile(sb_ref[...], (1, reps)), 0.0)
        r = _rope(acc, cs, sa, sb)
        kv_ref[...] = r[:, :KV_COLS]
        z_ref[...] = r.astype(BF16)


def _inproj(x, g, w_al, w_gate, cs, sa, sb):
    return pl.pallas_call(
        _inproj_kernel,
        grid=(M // TM, N_ZP // IN_TN),
        in_specs=[
            pl.BlockSpec((TM, D_MODEL), lambda i, j: (i, 0)),
            pl.BlockSpec((1, D_MODEL), lambda i, j: (0, 0)),
            pl.BlockSpec((D_MODEL, IN_TN), lambda i, j: (0, j)),
            pl.BlockSpec((D_MODEL, LANES), lambda i, j: (0, 0)),
            pl.BlockSpec((TM, LANES), lambda i, j: (i, 0)),
            pl.BlockSpec((TM, LANES), lambda i, j: (i, 0)),
            pl.BlockSpec((TM, LANES), lambda i, j: (i, 0)),
        ],
        out_specs=[
            pl.BlockSpec((TM, IN_TN), lambda i, j: (i, j)),
            pl.BlockSpec((TM, KV_COLS), lambda i, j: (i, 0)),
            pl.BlockSpec((TM, LANES), lambda i, j: (i, 0)),
        ],
        out_shape=[
            jax.ShapeDtypeStruct((M, N_ZP), BF16),
            jax.ShapeDtypeStruct((M, KV_COLS), F32),
            jax.ShapeDtypeStruct((M, LANES), F32),
        ],
        scratch_shapes=[pltpu.VMEM((TM, D_MODEL), BF16)],
        compiler_params=_cparams(("parallel", "arbitrary")),
        name="inproj",
    )(x, g, w_al, w_gate, cs, sa, sb)


NC = SEQ // ML
MLSTM_HPS = 4


def _mlstm_prompt_kernel(q_ref, k_ref, v_ref, o_ref, gt_ref, bias_ref, gn_ref,
                         hm_ref, c_ref, n_ref, m_ref, ct_scr, n_scr, m_scr):
    hp = pl.program_id(1)
    c = pl.program_id(2)

    @pl.when(c == 0)
    def _():
        ct_scr[...] = jnp.zeros_like(ct_scr)
        n_scr[...] = jnp.zeros_like(n_scr)
        m_scr[...] = jnp.zeros_like(m_scr)

    gates_t = (gt_ref[...] + bias_ref[...]).T
    sub = lax.broadcasted_iota(jnp.int32, gates_t.shape, 0)
    src = lax.broadcasted_iota(jnp.int32, (ML, ML), 0)
    tgt = lax.broadcasted_iota(jnp.int32, (ML, ML), 1)
    causal = src <= tgt
    causal_f = causal.astype(F32)
    first_row = lax.broadcasted_iota(jnp.int32, (8, ML), 0) == 0
    qscale = M_DQK ** -0.5
    nt = (((1,), (1,)), ((), ()))
    finals = []
    for hh in range(MLSTM_HPS):
        h = hp * MLSTM_HPS + hh
        qcols = slice(hh * M_DQK, (hh + 1) * M_DQK)
        vcols = slice(hh * M_DV, (hh + 1) * M_DV)
        i_row = jnp.sum(jnp.where(sub == h, gates_t, 0.0), axis=0, keepdims=True)
        f_row = jnp.sum(jnp.where(sub == h + M_HEADS, gates_t, 0.0), axis=0, keepdims=True)
        lf8 = jnp.where(first_row, _log_sigmoid(f_row), 0.0)
        b_row = jnp.dot(lf8, causal_f, preferred_element_type=F32,
                        precision=lax.Precision.HIGHEST)[0:1, :]
        c_row = i_row - b_row
        c_col = jnp.where(sub == 0, c_row, 0.0).T[:, 0:1]

        m_prev = m_scr[hh]
        a_row = b_row + m_prev
        dmat = jnp.where(causal, b_row + c_col, -jnp.inf)
        m_row = jnp.maximum(a_row, jnp.max(dmat, axis=0, keepdims=True))
        w_intra = jnp.exp(dmat - m_row)
        w_inter = jnp.exp(a_row - m_row)

        q = q_ref[:, qcols]
        k = k_ref[:, qcols]
        v_t = v_ref[:, vcols].astype(F32).T.astype(BF16)
        ct = ct_scr[hh]
        n_prev = n_scr[hh]
        s_t = lax.dot_general(k, q, nt, preferred_element_type=F32) * qscale * w_intra
        inter = lax.dot_general(ct.astype(BF16), q, nt, preferred_element_type=F32) * qscale
        num = w_inter * inter + jnp.dot(v_t, s_t.astype(BF16), preferred_element_type=F32)
        n8 = jnp.broadcast_to(n_prev, (8, M_DQK)).astype(BF16)
        qn = lax.dot_general(n8, q, nt, preferred_element_type=F32)[0:1, :] * qscale
        den = w_inter * qn + jnp.sum(s_t, axis=0, keepdims=True)
        hd = num / jnp.maximum(jnp.abs(den), jnp.exp(-m_row))
        ms = jnp.mean(hd * hd, axis=0, keepdims=True)
        y_t = (hd * lax.rsqrt(ms + EPS)) * jnp.tile(gn_ref[hh], (1, ML // LANES))
        hm_ref[:, vcols] = (y_t.T * _sigmoid(o_ref[:, vcols].astype(F32))).astype(BF16)

        m_new = m_row[:, ML - 1:ML]
        b_last = b_row[:, ML - 1:ML]
        w_state = jnp.exp(c_col + (b_last - m_new))
        decay = jnp.exp(b_last + m_prev - m_new)
        kw = k.astype(F32) * w_state
        ct_new = decay * ct + jnp.dot(v_t, kw.astype(BF16), preferred_element_type=F32)
        n_new = decay * n_prev + jnp.sum(kw, axis=0, keepdims=True)
        ct_scr[hh] = ct_new
        n_scr[hh] = n_new
        m_scr[hh] = m_new
        finals.append((ct_new, n_new, m_new))

    @pl.when(c == NC - 1)
    def _():
        for hh, (ct_new, n_new, m_new) in enumerate(finals):
            c_ref[hh] = ct_new.T
            n_ref[hh] = n_new
            m_ref[hh] = jnp.broadcast_to(m_new, (1, LANES))


def _mlstm_prompt(z, gates, bias, gn):
    hps = MLSTM_HPS
    qb, vb = hps * M_DQK, hps * M_DV
    gn_cols = jnp.broadcast_to(gn.reshape(M_HEADS, M_DV, 1), (M_HEADS, M_DV, LANES))
    return pl.pallas_call(
        _mlstm_prompt_kernel,
        grid=(BATCH, M_HEADS // hps, NC),
        in_specs=[
            pl.BlockSpec((ML, qb), lambda b, h, c: (b * NC + c, C_QM // qb + h)),
            pl.BlockSpec((ML, qb), lambda b, h, c: (b * NC + c, C_KM // qb + h)),
            pl.BlockSpec((ML, vb), lambda b, h, c: (b * NC + c, C_VM // vb + h)),
            pl.BlockSpec((ML, vb), lambda b, h, c: (b * NC + c, C_OM // vb + h)),
            pl.BlockSpec((ML, LANES), lambda b, h, c: (b * NC + c, 0)),
            pl.BlockSpec((1, LANES), lambda b, h, c: (0, 0)),
            pl.BlockSpec((hps, M_DV, LANES), lambda b, h, c: (h, 0, 0)),
        ],
        out_specs=[
            pl.BlockSpec((ML, vb), lambda b, h, c: (b * NC + c, h)),
            pl.BlockSpec((None, hps, M_DQK, M_DV), lambda b, h, c: (b, h, 0, 0)),
            pl.BlockSpec((None, hps, 1, M_DQK), lambda b, h, c: (b, h, 0, 0)),
            pl.BlockSpec((None, hps, 1, LANES), lambda b, h, c: (b, h, 0, 0)),
        ],
        out_shape=[
            jax.ShapeDtypeStruct((MP, M_HEADS * M_DV), BF16),
            jax.ShapeDtypeStruct((BATCH, M_HEADS, M_DQK, M_DV), F32),
            jax.ShapeDtypeStruct((BATCH, M_HEADS, 1, M_DQK), F32),
            jax.ShapeDtypeStruct((BATCH, M_HEADS, 1, LANES), F32),
        ],
        scratch_shapes=[pltpu.VMEM((hps, M_DV, M_DQK), F32), pltpu.VMEM((hps, 1, M_DQK), F32),
                        pltpu.VMEM((hps, 1, 1), F32)],
        compiler_params=_cparams(("parallel", "parallel", "arbitrary")),
        name="mlstm_prompt",
    )(z, z, z, z, gates, bias, gn_cols)


def _mlstm_decode_kernel(q_ref, k_ref, v_ref, o_ref, qt_ref, kt_ref, gt_ref, bias_ref, m0_ref, gn_ref,
                         c0_ref, n0_ref, *rest, layer):
    if layer == 0:
        hm_ref, c_ref, n_ref, m_ref = rest
    else:
        _, hm_ref, c_ref, n_ref, m_ref = rest
    i = pl.program_id(0)
    h = pl.program_id(1)
    gates = gt_ref[...] + bias_ref[...]
    lane = lax.broadcasted_iota(jnp.int32, gates.shape, 1)
    log_f = pltpu.roll(_log_sigmoid(gates), LANES - M_HEADS, 1)
    a = log_f + m0_ref[...]
    m_t = jnp.maximum(a, gates)
    w_intra_all = jnp.exp(gates - m_t)
    w_inter_all = jnp.exp(a - m_t)
    floor_all = jnp.exp(-m_t)

    @pl.when(h == 0)
    def _():
        m_ref[...] = m_t

    def pick(arr):
        return jnp.sum(jnp.where(lane == h, arr, 0.0), axis=1, keepdims=True)

    wi = pick(w_intra_all)
    we = pick(w_inter_all)
    fl = pick(floor_all)
    qscale = M_DQK ** -0.5
    q = q_ref[...] * qscale
    k = k_ref[...]
    v = v_ref[...]
    n0 = n0_ref[...]
    s = jnp.sum(q * k, axis=1, keepdims=True) * wi
    den = we * jnp.sum(q * n0, axis=1, keepdims=True) + s
    dd = jnp.maximum(jnp.abs(den), fl)
    n_ref[...] = we * n0 + wi * k

    shift = lax.rem(LANES - i * TB, LANES)
    qt = pltpu.roll(qt_ref[...], shift, 1) * qscale
    kt = pltpu.roll(kt_ref[...], shift, 1)
    gn = gn_ref[...]
    sig_o = _sigmoid(o_ref[...])
    for j in range(TB):
        qc = qt[:, j:j + 1]
        kc = kt[:, j:j + 1]
        c0 = c0_ref[j]
        vj = v[j:j + 1, :]
        qc0 = jnp.sum(qc * c0, axis=0, keepdims=True)
        hrow = (we[j:j + 1, :] * qc0 + s[j:j + 1, :] * vj) / dd[j:j + 1, :]
        hm_ref[j:j + 1, :] = _rms(hrow, gn) * sig_o[j:j + 1, :]
        c_new = we[j:j + 1, :] * c0 + (wi[j:j + 1, :] * kc) * vj
        if layer == 0:
            c_ref[0, j] = c_new
            for d in range(1, DEPTH):
                c_ref[d, j] = jnp.zeros_like(c_new)
        else:
            c_ref[j] = c_new


def _mlstm_decode(layer, zs, qt, kt, gates_s, bias, m0p, gn, state_c, state_n, c_all):
    qb, vb = M_DQK, M_DV
    if layer == 0:
        c_spec = pl.BlockSpec((DEPTH, TB, None, M_DQK, M_DV), lambda i, h: (0, i, h, 0, 0))
        extra_in, extra_specs, aliases = (), [], {}
    else:
        c_spec = pl.BlockSpec((None, TB, None, M_DQK, M_DV), lambda i, h: (layer, i, h, 0, 0))
        extra_in, extra_specs, aliases = (c_all,), [pl.BlockSpec(memory_space=pl.ANY)], {12: 1}
    return pl.pallas_call(
        functools.partial(_mlstm_decode_kernel, layer=layer),
        grid=(MS // TB, M_HEADS),
        input_output_aliases=aliases,
        in_specs=[
            pl.BlockSpec((TB, qb), lambda i, h: (i, C_QM // qb + h)),
            pl.BlockSpec((TB, qb), lambda i, h: (i, C_KM // qb + h)),
            pl.BlockSpec((TB, vb), lambda i, h: (i, C_VM // vb + h)),
            pl.BlockSpec((TB, vb), lambda i, h: (i, C_OM // vb + h)),
            pl.BlockSpec((None, M_DQK, MS), lambda i, h: (h, 0, 0)),
            pl.BlockSpec((None, M_DQK, MS), lambda i, h: (h, 0, 0)),
            pl.BlockSpec((TB, LANES), lambda i, h: (i, 0)),
            pl.BlockSpec((1, LANES), lambda i, h: (0, 0)),
            pl.BlockSpec((TB, LANES), lambda i, h: (i, 0)),
            pl.BlockSpec((1, vb), lambda i, h: (0, h)),
            pl.BlockSpec((None, TB, None, M_DQK, M_DV), lambda i, h: (layer, i, h, 0, 0)),
            pl.BlockSpec((None, TB, M_DQK), lambda i, h: (layer, i, h)),
        ] + extra_specs,
        out_specs=[
            pl.BlockSpec((TB, vb), lambda i, h: (i, h)),
            c_spec,
            pl.BlockSpec((TB, M_DQK), lambda i, h: (i, h)),
            pl.BlockSpec((TB, LANES), lambda i, h: (i, 0)),
        ],
        out_shape=[
            jax.ShapeDtypeStruct((MS, M_HEADS * M_DV), F32),
            jax.ShapeDtypeStruct((DEPTH, MS, M_HEADS, M_DQK, M_DV), F32),
            jax.ShapeDtypeStruct((MS, M_HEADS * M_DQK), F32),
            jax.ShapeDtypeStruct((MS, LANES), F32),
        ],
        compiler_params=_cparams(("parallel", "arbitrary")),
        name="mlstm_decode",
    )(zs, zs, zs, zs, qt, kt, gates_s, bias, m0p, gn, state_c, state_n, *extra_in)


NB = SEQ // ATT_BLOCK
KVW = KV_HEADS * HEAD_DIM


def _swa_prompt_kernel(sink_ref, q_ref, kc_ref, kp_ref, vc_ref, vp_ref, o_ref):
    nb = pl.program_id(1)
    cols = GROUP * ATT_BLOCK
    sidx = lax.broadcasted_iota(jnp.int32, (2 * ATT_BLOCK, cols), 0)
    t = lax.broadcasted_iota(jnp.int32, (2 * ATT_BLOCK, cols), 1) & (ATT_BLOCK - 1)
    rel = t + ATT_BLOCK - sidx
    visible = jnp.logical_and(jnp.logical_and(rel >= 0, rel <= WINDOW),
                              jnp.logical_or(sidx >= ATT_BLOCK, nb > 0))
    bias = jnp.where(visible, 0.0, -jnp.inf)
    q = q_ref[...] * (HEAD_DIM ** -0.5)
    kk = jnp.concatenate([kp_ref[...], kc_ref[...]], axis=0).astype(BF16)
    vv_t = jnp.concatenate([vp_ref[...], vc_ref[...]], axis=0).T.astype(BF16)
    for g in range(KV_HEADS):
        qg = jnp.concatenate(
            [q[:, (g * GROUP + hh) * HEAD_DIM:(g * GROUP + hh + 1) * HEAD_DIM] for hh in range(GROUP)], axis=0)
        kg = kk[:, g * HEAD_DIM:(g + 1) * HEAD_DIM]
        vg_t = vv_t[g * HEAD_DIM:(g + 1) * HEAD_DIM, :]
        sink = jnp.concatenate(
            [jnp.full((1, ATT_BLOCK), sink_ref[g * GROUP + hh], F32) for hh in range(GROUP)], axis=1)
        s = lax.dot_general(kg, qg, (((1,), (1,)), ((), ())), preferred_element_type=F32) + bias
        mx = jnp.maximum(jnp.max(s, axis=0, keepdims=True), sink)
        p = jnp.exp(s - mx)
        denom = jnp.sum(p, axis=0, keepdims=True) + jnp.exp(sink - mx)
        o_t = jnp.dot(vg_t, p.astype(BF16), preferred_element_type=F32) / denom
        og = jnp.concatenate([o_t[:, hh * ATT_BLOCK:(hh + 1) * ATT_BLOCK].T for hh in range(GROUP)], axis=1)
        o_ref[:, g * GROUP * HEAD_DIM:(g + 1) * GROUP * HEAD_DIM] = og.astype(BF16)


def _swa_prompt(z, kv, sinks):
    qw = N_HEADS * HEAD_DIM
    return pl.pallas_call(
        _swa_prompt_kernel,
        grid=(BATCH, NB),
        in_specs=[
            pl.BlockSpec(memory_space=pltpu.SMEM),
            pl.BlockSpec((ATT_BLOCK, qw), lambda b, n: (b * NB + n, C_QA // qw)),
            pl.BlockSpec((ATT_BLOCK, KVW), lambda b, n: (b * NB + n, 0)),
            pl.BlockSpec((ATT_BLOCK, KVW), lambda b, n: (b * NB + jnp.maximum(n - 1, 0), 0)),
            pl.BlockSpec((ATT_BLOCK, KVW), lambda b, n: (b * NB + n, 1)),
            pl.BlockSpec((ATT_BLOCK, KVW), lambda b, n: (b * NB + jnp.maximum(n - 1, 0), 1)),
        ],
        out_specs=pl.BlockSpec((ATT_BLOCK, qw), lambda b, n: (b * NB + n, 0)),
        out_shape=jax.ShapeDtypeStruct((MP, qw), BF16),
        compiler_params=_cparams(("parallel", "arbitrary")),
        name="swa_prompt",
    )(sinks, z, kv, kv, kv, kv)


def _swa_decode_kernel(q_ref, kn_ref, vn_ref, knf_ref, vnf_ref, kb_ref, vb_ref, sink_ref, *rest, layer):
    if layer == 0:
        o_ref, kc_ref, vc_ref = rest
    else:
        _, _, o_ref, kc_ref, vc_ref = rest

    def put(ref, j, rows, val):
        if layer == 0:
            ref[0, j, rows, :] = val
        else:
            ref[j, rows, :] = val

    for j in range(TB):
        put(kc_ref, j, slice(0, WINDOW - 1), kb_ref[j, 1:WINDOW, :])
        put(kc_ref, j, slice(WINDOW - 1, WINDOW), knf_ref[j:j + 1, :])
        put(vc_ref, j, slice(0, WINDOW - 1), vb_ref[j, 1:WINDOW, :])
        put(vc_ref, j, slice(WINDOW - 1, WINDOW), vnf_ref[j:j + 1, :])
    if layer == 0:
        for d in range(1, DEPTH):
            kc_ref[d] = jnp.zeros(kc_ref.shape[1:], F32)
            vc_ref[d] = jnp.zeros(vc_ref.shape[1:], F32)

    rows, cols = TB * GROUP, TB * WINDOW
    own = (lax.broadcasted_iota(jnp.int32, (rows, cols), 0) // GROUP
           == lax.broadcasted_iota(jnp.int32, (rows, cols), 1) // WINDOW)
    bias = jnp.where(own, 0.0, -jnp.inf)
    scale = HEAD_DIM ** -0.5
    kstack = kb_ref[...].reshape(cols, KVW).astype(BF16)
    vstack = vb_ref[...].reshape(cols, KVW).astype(BF16)
    for g in range(KV_HEADS):
        qg = q_ref[:, g * GROUP:(g + 1) * GROUP, :].reshape(rows, HEAD_DIM) * scale
        kn = jnp.broadcast_to(kn_ref[:, g:g + 1, :], (TB, GROUP, HEAD_DIM)).reshape(rows, HEAD_DIM)
        vn = jnp.broadcast_to(vn_ref[:, g:g + 1, :], (TB, GROUP, HEAD_DIM)).reshape(rows, HEAD_DIM)
        sink = jnp.tile(sink_ref[g * GROUP:(g + 1) * GROUP, 0:1], (TB, 1))
        s_c = lax.dot_general(qg.astype(BF16), kstack[:, g * HEAD_DIM:(g + 1) * HEAD_DIM],
                              (((1,), (1,)), ((), ())), preferred_element_type=F32) + bias
        s_n = jnp.sum(qg * kn, axis=1, keepdims=True)
        mx = jnp.maximum(jnp.maximum(jnp.max(s_c, axis=1, keepdims=True), s_n), sink)
        p_c = jnp.exp(s_c - mx)
        p_n = jnp.exp(s_n - mx)
        denom = jnp.sum(p_c, axis=1, keepdims=True) + p_n + jnp.exp(sink - mx)
        o = jnp.dot(p_c.astype(BF16), vstack[:, g * HEAD_DIM:(g + 1) * HEAD_DIM], preferred_element_type=F32)
        o = (o + p_n * vn) / denom
        o_ref[:, g * GROUP:(g + 1) * GROUP, :] = o.reshape(TB, GROUP, HEAD_DIM)


def _swa_decode(layer, q3, kn3, vn3, knf, vnf, kbuf, vbuf, sinks_b, k_all, v_all):
    if layer == 0:
        cache_spec = pl.BlockSpec((DEPTH, TB, WINDOW, KVW), lambda i: (0, i, 0, 0))
        extra_in, extra_specs, aliases = (), [], {}
    else:
        cache_spec = pl.BlockSpec((None, TB, WINDOW, KVW), lambda i: (layer, i, 0, 0))
        extra_in = (k_all, v_all)
        extra_specs = [pl.BlockSpec(memory_space=pl.ANY), pl.BlockSpec(memory_space=pl.ANY)]
        aliases = {8: 1, 9: 2}
    return pl.pallas_call(
        functools.partial(_swa_decode_kernel, layer=layer),
        grid=(MS // TB,),
        input_output_aliases=aliases,
        in_specs=[
            pl.BlockSpec((TB, N_HEADS, HEAD_DIM), lambda i: (i, 0, 0)),
            pl.BlockSpec((TB, KV_HEADS, HEAD_DIM), lambda i: (i, 0, 0)),
            pl.BlockSpec((TB, KV_HEADS, HEAD_DIM), lambda i: (i, 0, 0)),
            pl.BlockSpec((TB, KVW), lambda i: (i, 0)),
            pl.BlockSpec((TB, KVW), lambda i: (i, 0)),
            pl.BlockSpec((None, TB, WINDOW, KVW), lambda i: (layer, i, 0, 0)),
            pl.BlockSpec((None, TB, WINDOW, KVW), lambda i: (layer, i, 0, 0)),
            pl.BlockSpec((N_HEADS, LANES), lambda i: (0, 0)),
        ] + extra_specs,
        out_specs=[
            pl.BlockSpec((TB, N_HEADS, HEAD_DIM), lambda i: (i, 0, 0)),
            cache_spec,
            cache_spec,
        ],
        out_shape=[
            jax.ShapeDtypeStruct((MS, N_HEADS, HEAD_DIM), F32),
            jax.ShapeDtypeStruct((DEPTH, MS, WINDOW, KVW), F32),
            jax.ShapeDtypeStruct((DEPTH, MS, WINDOW, KVW), F32),
        ],
        compiler_params=_cparams(("parallel",)),
        name="swa_decode",
    )(q3, kn3, vn3, knf, vnf, kbuf, vbuf, sinks_b, *extra_in)


def _merge_kernel(hm_ref, ha_ref, wbm_ref, wba_ref, gm_ref, ga_ref, o_ref):
    a = jnp.dot(hm_ref[...], wbm_ref[...], preferred_element_type=F32)
    b = jnp.dot(ha_ref[...], wba_ref[...], preferred_element_type=F32)
    o_ref[...] = (_sigmoid(gm_ref[...].astype(F32)) * a + _sigmoid(ga_ref[...].astype(F32)) * b).astype(BF16)


def _merge(hm, ha, wbm, wba, z):
    return pl.pallas_call(
        _merge_kernel,
        grid=(M // TM, D_MODEL // TN),
        in_specs=[
            pl.BlockSpec((TM, D_MODEL), lambda i, j: (i, 0)),
            pl.BlockSpec((TM, D_MODEL), lambda i, j: (i, 0)),
            pl.BlockSpec((D_MODEL, TN), lambda i, j: (0, j)),
            pl.BlockSpec((D_MODEL, TN), lambda i, j: (0, j)),
            pl.BlockSpec((TM, TN), lambda i, j: (i, C_GM // TN + j)),
            pl.BlockSpec((TM, TN), lambda i, j: (i, C_GA // TN + j)),
        ],
        out_specs=pl.BlockSpec((TM, TN), lambda i, j: (i, j)),
        out_shape=jax.ShapeDtypeStruct((M, D_MODEL), BF16),
        compiler_params=_cparams(("parallel", "arbitrary")),
        name="merge",
    )(hm, ha, wbm, wba, z, z)


def _mm_res_kernel(a_ref, w_ref, r_ref, o_ref):
    o_ref[...] = r_ref[...] + jnp.dot(a_ref[...], w_ref[...], preferred_element_type=F32)


def _mm_res(a, w, res, tm, tn, name):
    kdim = a.shape[1]
    n = w.shape[1]
    return pl.pallas_call(
        _mm_res_kernel,
        grid=(M // tm, n // tn),
        in_specs=[
            pl.BlockSpec((tm, kdim), lambda i, j: (i, 0)),
            pl.BlockSpec((kdim, tn), lambda i, j: (0, j)),
            pl.BlockSpec((tm, tn), lambda i, j: (i, j)),
        ],
        out_specs=pl.BlockSpec((tm, tn), lambda i, j: (i, j)),
        out_shape=jax.ShapeDtypeStruct((M, n), F32),
        compiler_params=_cparams(("parallel", "arbitrary")),
        name=name,
    )(a, w, res)


def _ffn_up_kernel(x_ref, g_ref, wg_ref, wu_ref, act_ref, h_scr):
    @pl.when(pl.program_id(1) == 0)
    def _():
        h_scr[...] = _rms(x_ref[...], g_ref[...]).astype(BF16)

    h = h_scr[...]
    a = jnp.dot(h, wg_ref[...], preferred_element_type=F32)
    u = jnp.dot(h, wu_ref[...], preferred_element_type=F32)
    act_ref[...] = ((a * _sigmoid(a)) * u).astype(BF16)


def _ffn_up(x, g, wg, wu):
    return pl.pallas_call(
        _ffn_up_kernel,
        grid=(M // TM, D_FF // TN),
        in_specs=[
            pl.BlockSpec((TM, D_MODEL), lambda i, j: (i, 0)),
            pl.BlockSpec((1, D_MODEL), lambda i, j: (0, 0)),
            pl.BlockSpec((D_MODEL, TN), lambda i, j: (0, j)),
            pl.BlockSpec((D_MODEL, TN), lambda i, j: (0, j)),
        ],
        out_specs=pl.BlockSpec((TM, TN), lambda i, j: (i, j)),
        out_shape=jax.ShapeDtypeStruct((M, D_FF), BF16),
        scratch_shapes=[pltpu.VMEM((TM, D_MODEL), BF16)],
        compiler_params=_cparams(("parallel", "arbitrary")),
        name="ffn_up",
    )(x, g, wg, wu)


def _router_kernel(x_ref, g_ref, wr_ref, ids_ref, gates_ref, cnt_ref, run_scr):
    @pl.when(pl.program_id(0) == 0)
    def _():
        run_scr[...] = jnp.zeros_like(run_scr)

    h = _rms(x_ref[...], g_ref[...])
    logits = jnp.dot(h, wr_ref[...], preferred_element_type=F32, precision=lax.Precision.HIGHEST)
    lane = lax.broadcasted_iota(jnp.int32, logits.shape, 1)
    lg = jnp.where(lane < N_EXPERTS, logits, -jnp.inf)
    v1 = jnp.max(lg, axis=1, keepdims=True)
    i1 = jnp.min(jnp.where(lg == v1, lane, LANES), axis=1, keepdims=True)
    lg2 = jnp.where(lane == i1, -jnp.inf, lg)
    v2 = jnp.max(lg2, axis=1, keepdims=True)
    i2 = jnp.min(jnp.where(lg2 == v2, lane, LANES), axis=1, keepdims=True)
    e2 = jnp.exp(v2 - v1)
    g1 = 1.0 / (1.0 + e2)
    g2 = e2 / (1.0 + e2)
    gates_ref[...] = jnp.where(lane == 0, g1, jnp.where(lane == 1, g2, 0.0))

    picks = jnp.where(jnp.logical_or(lane == i1, lane == i2), 1.0, 0.0)
    tm = picks.shape[0]
    earlier = lax.broadcasted_iota(jnp.int32, (tm, tm), 1) < lax.broadcasted_iota(jnp.int32, (tm, tm), 0)
    before = jnp.dot(jnp.where(earlier, 1.0, 0.0).astype(BF16), picks.astype(BF16),
                     preferred_element_type=F32) + run_scr[...]
    r1 = jnp.sum(jnp.where(lane == i1, before, 0.0), axis=1, keepdims=True).astype(jnp.int32)
    r2 = jnp.sum(jnp.where(lane == i2, before, 0.0), axis=1, keepdims=True).astype(jnp.int32)
    ids_ref[...] = jnp.where(lane == 0, i1, jnp.where(lane == 1, i2, jnp.where(lane == 2, r1, jnp.where(lane == 3, r2, 0))))
    total = run_scr[...] + jnp.sum(picks, axis=0, keepdims=True)
    run_scr[...] = total
    cnt_ref[...] = total


def _router(x, g, wr_pad):
    return pl.pallas_call(
        _router_kernel,
        grid=(M // TM,),
        in_specs=[
            pl.BlockSpec((TM, D_MODEL), lambda i: (i, 0)),
            pl.BlockSpec((1, D_MODEL), lambda i: (0, 0)),
            pl.BlockSpec((D_MODEL, LANES), lambda i: (0, 0)),
        ],
        out_specs=[
            pl.BlockSpec((TM, LANES), lambda i: (i, 0)),
            pl.BlockSpec((TM, LANES), lambda i: (i, 0)),
            pl.BlockSpec((1, LANES), lambda i: (0, 0)),
        ],
        out_shape=[
            jax.ShapeDtypeStruct((M, LANES), jnp.int32),
            jax.ShapeDtypeStruct((M, LANES), F32),
            jax.ShapeDtypeStruct((1, LANES), F32),
        ],
        scratch_shapes=[pltpu.VMEM((1, LANES), F32)],
        compiler_params=_cparams(("arbitrary",)),
        name="router",
    )(x, g, wr_pad)


def _row_copy(src_hbm, row, buf, r, sem):
    return pltpu.make_async_copy(src_hbm.at[pl.ds(row, 1), :], buf.at[pl.ds(r, 1), :], sem)


def _slot_gather_kernel(tok_ref, nu_ref, x_hbm, g_ref, xs_ref, buf, sem):
    m = pl.program_id(0)
    n_used = nu_ref[0]
    slot = lax.rem(m, 2)

    def request(tile, into):
        def issue(r, carry):
            _row_copy(x_hbm, tok_ref[tile * MOE_UP_TM + r], buf.at[into], r, sem.at[into]).start()
            return carry
        lax.fori_loop(0, MOE_UP_TM, issue, 0, unroll=8)

    @pl.when(m == 0)
    def _():
        request(0, 0)

    @pl.when(m + 1 < n_used)
    def _():
        request(m + 1, 1 - slot)

    @pl.when(m < n_used)
    def _():
        def wait(r, carry):
            _row_copy(x_hbm, 0, buf.at[slot], r, sem.at[slot]).wait()
            return carry
        lax.fori_loop(0, MOE_UP_TM, wait, 0, unroll=8)
        xs_ref[...] = _rms(buf[slot], g_ref[...]).astype(BF16)

    @pl.when(m >= n_used)
    def _():
        xs_ref[...] = jnp.zeros_like(xs_ref)


def _slot_gather(slot_token, n_used, x, g):
    return pl.pallas_call(
        _slot_gather_kernel,
        grid_spec=pltpu.PrefetchScalarGridSpec(
            num_scalar_prefetch=2,
            grid=(N_SLOTS // MOE_UP_TM,),
            in_specs=[pl.BlockSpec(memory_space=pl.ANY),
                      pl.BlockSpec((1, D_MODEL), lambda m, tok, nu: (0, 0))],
            out_specs=pl.BlockSpec((MOE_UP_TM, D_MODEL), lambda m, tok, nu: (m, 0)),
            scratch_shapes=[pltpu.VMEM((2, MOE_UP_TM, D_MODEL), F32), pltpu.SemaphoreType.DMA((2,))],
        ),
        out_shape=jax.ShapeDtypeStruct((N_SLOTS, D_MODEL), BF16),
        compiler_params=_cparams(("arbitrary",)),
        name="slot_gather",
    )(slot_token, n_used, x, g)


def _combine_kernel(p1_ref, p2_ref, x_ref, y_hbm, gate_ref, g_ref, o_ref, buf1, buf2, sem, *, final_norm):
    base = pl.program_id(0) * COMBINE_ROWS

    def issue(r, carry):
        _row_copy(y_hbm, p1_ref[base + r], buf1, r, sem).start()
        _row_copy(y_hbm, p2_ref[base + r], buf2, r, sem).start()
        return carry
    lax.fori_loop(0, COMBINE_ROWS, issue, 0, unroll=8)

    def wait(r, carry):
        _row_copy(y_hbm, 0, buf1, r, sem).wait()
        _row_copy(y_hbm, 0, buf2, r, sem).wait()
        return carry
    lax.fori_loop(0, COMBINE_ROWS, wait, 0, unroll=8)
    gates = gate_ref[...]
    out = x_ref[...] + (gates[:, 0:1] * buf1[...] + gates[:, 1:2] * buf2[...])
    o_ref[...] = _rms(out, g_ref[...]) if final_norm else out


def _combine(pos1, pos2, x, y, gates, g_final):
    final_norm = g_final is not None
    g = g_final if final_norm else jnp.ones((1, D_MODEL), F32)
    return pl.pallas_call(
        functools.partial(_combine_kernel, final_norm=final_norm),
        grid_spec=pltpu.PrefetchScalarGridSpec(
            num_scalar_prefetch=2,
            grid=(M // COMBINE_ROWS,),
            in_specs=[pl.BlockSpec((COMBINE_ROWS, D_MODEL), lambda i, p1, p2: (i, 0)),
                      pl.BlockSpec(memory_space=pl.ANY),
                      pl.BlockSpec((COMBINE_ROWS, LANES), lambda i, p1, p2: (i, 0)),
                      pl.BlockSpec((1, D_MODEL), lambda i, p1, p2: (0, 0))],
            out_specs=pl.BlockSpec((COMBINE_ROWS, D_MODEL), lambda i, p1, p2: (i, 0)),
            scratch_shapes=[pltpu.VMEM((COMBINE_ROWS, D_MODEL), F32), pltpu.VMEM((COMBINE_ROWS, D_MODEL), F32),
                            pltpu.SemaphoreType.DMA(())],
        ),
        out_shape=jax.ShapeDtypeStruct((M, D_MODEL), F32),
        compiler_params=_cparams(("arbitrary",)),
        name="combine",
    )(pos1, pos2, x, y, gates, g)


def _moe_up_kernel(te_ref, nu_ref, x_ref, wg_ref, wu_ref, wdf_ref, act_ref, wdb_ref, wg_scr, wu_scr):
    m = pl.program_id(1)
    wdb_ref[...] = wdf_ref[...].astype(BF16)

    @pl.when(m < nu_ref[0])
    def _():
        prev = te_ref[jnp.maximum(m - 1, 0)]

        @pl.when(jnp.logical_or(m == 0, te_ref[m] != prev))
        def _():
            wg_scr[...] = wg_ref[...].astype(BF16)
            wu_scr[...] = wu_ref[...].astype(BF16)

        x = x_ref[...]
        a = jnp.dot(x, wg_scr[...], preferred_element_type=F32)
        u = jnp.dot(x, wu_scr[...], preferred_element_type=F32)
        act_ref[...] = ((a * _sigmoid(a)) * u).astype(BF16)

    @pl.when(m >= nu_ref[0])
    def _():
        act_ref[...] = jnp.zeros_like(act_ref)


def _side_chunk(total_rows, n_steps):
    chunk = BF16_SUBLANES
    while total_rows % chunk or total_rows // chunk > n_steps:
        chunk += BF16_SUBLANES
    return chunk


def _moe_up(tile_expert, n_used, xs, wg, wu, wd):
    n_j, n_m = E_FF // MOE_TF, N_SLOTS // MOE_UP_TM
    wd_rows = N_EXPERTS * E_FF
    chunk = _side_chunk(wd_rows, n_j * n_m)
    last_chunk = wd_rows // chunk - 1
    side = lambda j, m, te, nu: (jnp.minimum(j * n_m + m, last_chunk), 0)
    return pl.pallas_call(
        _moe_up_kernel,
        grid_spec=pltpu.PrefetchScalarGridSpec(
            num_scalar_prefetch=2,
            grid=(n_j, n_m),
            in_specs=[
                pl.BlockSpec((MOE_UP_TM, D_MODEL), lambda j, m, te, nu: (m, 0)),
                pl.BlockSpec((None, D_MODEL, MOE_TF), lambda j, m, te, nu: (te[m], 0, j)),
                pl.BlockSpec((None, D_MODEL, MOE_TF), lambda j, m, te, nu: (te[m], 0, j)),
                pl.BlockSpec((chunk, D_MODEL), side),
            ],
            out_specs=[pl.BlockSpec((MOE_UP_TM, MOE_TF), lambda j, m, te, nu: (m, j)),
                       pl.BlockSpec((chunk, D_MODEL), side)],
            scratch_shapes=[pltpu.VMEM((D_MODEL, MOE_TF), BF16), pltpu.VMEM((D_MODEL, MOE_TF), BF16)],
        ),
        out_shape=[jax.ShapeDtypeStruct((N_SLOTS, E_FF), BF16),
                   jax.ShapeDtypeStruct((wd_rows, D_MODEL), BF16)],
        compiler_params=_cparams(("arbitrary", "arbitrary")),
        name="moe_up",
    )(tile_expert, n_used, xs, wg, wu, wd.reshape(wd_rows, D_MODEL))


def _moe_down_kernel(te_ref, nu_ref, a_ref, wd_ref, y_ref):
    m = pl.program_id(1)

    @pl.when(m < nu_ref[0])
    def _():
        y_ref[...] = jnp.dot(a_ref[...], wd_ref[...], preferred_element_type=F32)

    @pl.when(m >= nu_ref[0])
    def _():
        y_ref[...] = jnp.zeros_like(y_ref)


def _moe_down(tile_expert, n_used, act, wd_bf16):
    return pl.pallas_call(
        _moe_down_kernel,
        grid_spec=pltpu.PrefetchScalarGridSpec(
            num_scalar_prefetch=2,
            grid=(D_MODEL // MOE_TN, N_SLOTS // MOE_DN_TM),
            in_specs=[
                pl.BlockSpec((MOE_DN_TM, E_FF), lambda j, m, te, nu: (m, 0)),
                pl.BlockSpec((None, E_FF, MOE_TN), lambda j, m, te, nu: (te[m], 0, j)),
            ],
            out_specs=pl.BlockSpec((MOE_DN_TM, MOE_TN), lambda j, m, te, nu: (m, j)),
        ),
        out_shape=jax.ShapeDtypeStruct((N_SLOTS, D_MODEL), F32),
        compiler_params=_cparams(("arbitrary", "arbitrary")),
        name="moe_down",
    )(tile_expert, n_used, act, wd_bf16.reshape(N_EXPERTS, E_FF, D_MODEL))


def _final_norm_kernel(x_ref, g_ref, o_ref):
    o_ref[...] = _rms(x_ref[...], g_ref[...])


def _final_norm(x, g):
    spec = pl.BlockSpec((COMBINE_ROWS, D_MODEL), lambda i: (i, 0))
    return pl.pallas_call(
        _final_norm_kernel,
        grid=(M // COMBINE_ROWS,),
        in_specs=[spec, pl.BlockSpec((1, D_MODEL), lambda i: (0, 0))],
        out_specs=spec,
        out_shape=jax.ShapeDtypeStruct((M, D_MODEL), F32),
        compiler_params=_cparams(("parallel",)),
        name="final_norm",
    )(x, g)


def _tile_map(ends, tm):
    n_tiles = N_SLOTS // tm
    n_used = (ends[-1] // tm).astype(jnp.int32)
    tile_start = jnp.arange(n_tiles, dtype=jnp.int32) * tm
    tile_expert = jnp.sum((tile_start[:, None] >= ends[None, :]).astype(jnp.int32), axis=1)
    last = jnp.sum((jnp.maximum(n_used - 1, 0) * tm >= ends).astype(jnp.int32))
    tile_expert = jnp.where(jnp.arange(n_tiles) < n_used, tile_expert, last).astype(jnp.int32)
    return tile_expert, n_used.reshape(1)


def _route(ids, counts):
    counts = counts[0, :N_EXPERTS].astype(jnp.int32)
    padded = ((counts + MOE_PAD - 1) // MOE_PAD) * MOE_PAD
    ends = jnp.cumsum(padded)
    starts = ends - padded
    pos1 = starts[ids[:, 0]] + ids[:, 2]
    pos2 = starts[ids[:, 1]] + ids[:, 3]
    tok = jnp.arange(M, dtype=jnp.int32)
    slot_token = jnp.zeros((N_SLOTS,), jnp.int32).at[jnp.concatenate([pos1, pos2])].set(jnp.concatenate([tok, tok]))
    return slot_token, _tile_map(ends, MOE_UP_TM), _tile_map(ends, MOE_DN_TM), pos1, pos2


def _rope_tables():
    half = ROT_DIM // 2
    pos = jnp.concatenate([jnp.tile(jnp.arange(SEQ, dtype=jnp.int32), BATCH),
                           jnp.full((MS,), PAST_LEN, jnp.int32)])
    inv = jnp.power(ROPE_THETA, -jnp.arange(half, dtype=F32) / half)
    ang = pos.astype(F32)[:, None] * inv[None, :]
    cos, sin = jnp.cos(ang), jnp.sin(ang)
    pad = HEAD_DIM - ROT_DIM
    one = jnp.ones((M, pad), F32)
    zero = jnp.zeros((M, pad), F32)
    zh = jnp.zeros((M, half), F32)
    cs = jnp.concatenate([cos, cos, one], axis=1)
    sa = jnp.concatenate([-sin, zh, zero], axis=1)
    sb = jnp.concatenate([zh, sin, zero], axis=1)
    rep = LANES // HEAD_DIM
    return jnp.tile(cs, (1, rep)), jnp.tile(sa, (1, rep)), jnp.tile(sb, (1, rep))


def _pad_lanes(a):
    return jnp.pad(a, ((0, 0), (0, LANES - a.shape[1])))


def kernel(x_prompt, x_sample, state_mlstm_C, state_mlstm_n, state_mlstm_m, cache_swa_k, cache_swa_v, norm_mix_g, w_in, b_igate, b_fgate, mlstm_norm_g, attn_sinks, w_branch_m, w_branch_a, w_out, norm_ffn_g, w_gate_dense, w_up_dense, w_down_dense, w_router, w_gate_moe, w_up_moe, w_down_moe, norm_final_g):
    x = jnp.concatenate([x_prompt.reshape(MP, D_MODEL), x_sample.reshape(MS, D_MODEL)], axis=0)
    cs, sa, sb = _rope_tables()
    state_n = state_mlstm_n.reshape(DEPTH, MS, M_HEADS * M_DQK)
    kbuf = cache_swa_k.reshape(DEPTH, MS, WINDOW, KVW)
    vbuf = cache_swa_v.reshape(DEPTH, MS, WINDOW, KVW)
    outs = {name: [] for name in ("Cp", "np", "mp", "kp", "vp", "ns", "ms")}
    y_final = None
    c_all = k_all = v_all = None

    for l in range(DEPTH):
        w_l = w_in[l]
        w_al = jnp.concatenate([w_l[:, :R_GATES], w_l[:, R_QA:], jnp.zeros((D_MODEL, N_ZP - N_Z), F32)],
                               axis=1).astype(BF16)
        w_gate = _pad_lanes(w_l[:, R_GATES:R_QA]).astype(BF16)
        bias = _pad_lanes(jnp.concatenate([b_igate[l], b_fgate[l]])[None, :])
        gn = mlstm_norm_g[l][None, :]
        sinks_b = jnp.broadcast_to(attn_sinks[l][:, None], (N_HEADS, LANES))

        z, kv, gates = _inproj(x, norm_mix_g[l][None, :], w_al, w_gate, cs, sa, sb)

        hm_p, c_p, n_p, m_p = _mlstm_prompt(z, gates, bias, gn)
        ha_p = _swa_prompt(z, kv, attn_sinks[l])
        kv_p = kv[:MP].reshape(BATCH, SEQ, 2, KV_HEADS, HEAD_DIM)[:, SEQ - WINDOW:]
        outs["Cp"].append(c_p)
        outs["np"].append(n_p.reshape(BATCH, M_HEADS, M_DQK))
        outs["mp"].append(m_p[:, :, 0, 0])
        outs["kp"].append(kv_p[:, :, 0])
        outs["vp"].append(kv_p[:, :, 1])

        zs = z[MP:].astype(F32)
        kv_s = kv[MP:]
        qt = zs[:, C_QM:C_KM].T.reshape(M_HEADS, M_DQK, MS)
        kt = zs[:, C_KM:C_VM].T.reshape(M_HEADS, M_DQK, MS)
        m0p = _pad_lanes(state_mlstm_m[l])
        hm_s, c_all, n_s, m_s = _mlstm_decode(l, zs, qt, kt, gates[MP:], bias, m0p, gn, state_mlstm_C, state_n, c_all)
        kn = kv_s[:, :KVW]
        vn = kv_s[:, KVW:]
        ha_s, k_all, v_all = _swa_decode(
            l, zs[:, C_QA:C_KA].reshape(MS, N_HEADS, HEAD_DIM), kn.reshape(MS, KV_HEADS, HEAD_DIM),
            vn.reshape(MS, KV_HEADS, HEAD_DIM), kn, vn, kbuf, vbuf, sinks_b, k_all, v_all)
        outs["ns"].append(n_s.reshape(MS, M_HEADS, M_DQK))
        outs["ms"].append(m_s[:, :M_HEADS])

        hm = jnp.concatenate([hm_p, hm_s.astype(BF16)], axis=0)
        ha = jnp.concatenate([ha_p, ha_s.reshape(MS, N_HEADS * HEAD_DIM).astype(BF16)], axis=0)
        merged = _merge(hm, ha, w_branch_m[l].astype(BF16), w_branch_a[l].astype(BF16), z)
        x = _mm_res(merged, w_out[l].astype(BF16), x, TM, TN, "out_proj")

        if l % 2 == 0:
            jd = l // 2
            act = _ffn_up(x, norm_ffn_g[l][None, :], w_gate_dense[jd].astype(BF16), w_up_dense[jd].astype(BF16))
            x = _mm_res(act, w_down_dense[jd].astype(BF16), x, TM, TN, "ffn_down")
        else:
            jm = l // 2
            g_ffn = norm_ffn_g[l][None, :]
            ids, gts, counts = _router(x, g_ffn, _pad_lanes(w_router[jm]))
            slot_token, (te_up, nu_up), (te_dn, nu_dn), pos1, pos2 = _route(ids, counts)
            xs = _slot_gather(slot_token, nu_up, x, g_ffn)
            act, wd_bf16 = _moe_up(te_up, nu_up, xs, w_gate_moe[jm], w_up_moe[jm], w_down_moe[jm])
            y = _moe_down(te_dn, nu_dn, act, wd_bf16)
            if l == DEPTH - 1:
                y_final = _combine(pos1, pos2, x, y, gts, norm_final_g[None, :])
            else:
                x = _combine(pos1, pos2, x, y, gts, None)

    if y_final is None:
        y_final = _final_norm(x, norm_final_g[None, :])
    y_prompt = y_final[:MP].reshape(BATCH, SEQ, D_MODEL)
    y_sample = y_final[MP:].reshape(MS, 1, D_MODEL)
    st = lambda name: jnp.stack(outs[name])
    return (y_prompt, y_sample, st("Cp"), st("np"), st("mp"), st("kp"), st("vp"),
            c_all, st("ns"), st("ms"),
            k_all.reshape(DEPTH, MS, WINDOW, KV_HEADS, HEAD_DIM), v_all.reshape(DEPTH, MS, WINDOW, KV_HEADS, HEAD_DIM))
```

```python
import functools

import jax
import jax.numpy as jnp
import numpy as np
from jax import lax
from jax.experimental import pallas as pl
from jax.experimental.pallas import tpu as pltpu

F32 = jnp.float32
BF16 = jnp.bfloat16

D_MODEL = 2048
BATCH = 4
SEQ = 2048
DEPTH = 2
DEC_BATCH = 128
PAST_LEN = 8192
M_HEADS = 4
M_DQK = 256
M_DV = 512
N_HEADS = 32
KV_HEADS = 4
HEAD_DIM = 64
GROUP = N_HEADS // KV_HEADS
ROT_DIM = HEAD_DIM // 4
ROPE_THETA = 500000.0
WINDOW = 128
ATT_BLOCK = 128
D_FF = 5632
N_EXPERTS = 8
TOP_K = 2
E_FF = 7168
EPS = 1e-6

MP = BATCH * SEQ
MS = DEC_BATCH
M = MP + MS

C_QM = 0
C_KM = C_QM + M_HEADS * M_DQK
C_VM = C_KM + M_HEADS * M_DQK
C_OM = C_VM + M_HEADS * M_DV
C_QA = C_OM + M_HEADS * M_DV
C_KA = C_QA + N_HEADS * HEAD_DIM
C_VA = C_KA + KV_HEADS * HEAD_DIM
C_GM = C_VA + KV_HEADS * HEAD_DIM
C_GA = C_GM + D_MODEL
N_Z = C_GA + D_MODEL
R_GATES = 2 * M_HEADS * M_DQK + 2 * M_HEADS * M_DV
R_QA = R_GATES + 2 * M_HEADS

LANES = 128
BF16_SUBLANES = 16
TM = 1040
TN = 512
OUT_TM = 640
ML = 256
TB = 8
MOE_PAD = 512
MOE_UP_TM = MOE_PAD
MOE_DN_TM = MOE_PAD
MOE_TF = 512
MOE_TN = 512
COMBINE_ROWS = TM // 2
N_SLOTS = -(-(TOP_K * M + N_EXPERTS * (MOE_PAD - 1)) // MOE_PAD) * MOE_PAD
VMEM_LIMIT = 56 * 1024 * 1024


def _cparams(sem, vmem=VMEM_LIMIT):
    return pltpu.CompilerParams(dimension_semantics=sem, vmem_limit_bytes=vmem)


def _rms(x, g):
    ms = jnp.mean(x * x, axis=-1, keepdims=True)
    return (x * lax.rsqrt(ms + EPS)) * g


def _sigmoid(x):
    return 1.0 / (1.0 + jnp.exp(-x))


def _log_sigmoid(x):
    return jnp.minimum(x, 0.0) - jnp.log(1.0 + jnp.exp(-jnp.abs(x)))


IN_TN = 1024
KV_COLS = C_GM - C_KA
J_QA0 = C_QA // IN_TN
J_KV = C_KA // IN_TN
N_ZP = -(-N_Z // IN_TN) * IN_TN
assert C_QA % IN_TN == 0 and C_KA % IN_TN == 0 and KV_COLS <= IN_TN


def _rope(acc, cs, sa, sb):
    n = acc.shape[1]
    return acc * cs + pltpu.roll(acc, n - ROT_DIM // 2, 1) * sa + pltpu.roll(acc, ROT_DIM // 2, 1) * sb


def _inproj_kernel(x_ref, g_ref, w_ref, wgate_ref, cs_ref, sa_ref, sb_ref, z_ref, kv_ref, gt_ref, h_scr):
    j = pl.program_id(1)

    @pl.when(j == 0)
    def _():
        h = _rms(x_ref[...], g_ref[...]).astype(BF16)
        h_scr[...] = h
        gt_ref[...] = jnp.dot(h, wgate_ref[...], preferred_element_type=F32)

    acc = jnp.dot(h_scr[...], w_ref[...], preferred_element_type=F32)
    z_ref[...] = acc.astype(BF16)
    reps = IN_TN // LANES
    is_q = jnp.logical_and(j >= J_QA0, j < J_KV)
    is_kv = j == J_KV

    @pl.when(is_q)
    def _():
        cs = jnp.tile(cs_ref[...], (1, reps))
        sa = jnp.tile(sa_ref[...], (1, reps))
        sb = jnp.tile(sb_ref[...], (1, reps))
        z_ref[...] = _rope(acc, cs, sa, sb).astype(BF16)

    @pl.when(is_kv)
    def _():
        is_k = lax.broadcasted_iota(jnp.int32, acc.shape, 1) < (C_VA - C_KA)
        cs = jnp.where(is_k, jnp.tile(cs_ref[...], (1, reps)), 1.0)
        sa = jnp.where(is_k, jnp.tile(sa_ref[...], (1, reps)), 0.0)
        sb = jnp.where(is_k, jnp.tile(sb_ref[...], (1, reps)), 0.0)
        r = _rope(acc, cs, sa, sb)
        kv_ref[...] = r[:, :KV_COLS]
        z_ref[...] = r.astype(BF16)


REPACK_ROWS = 256


def _repack_kernel(w_ref, w3_ref, wg_ref):
    rows = w_ref.shape[0]
    n_lead = R_GATES // IN_TN
    n_tail = (N_Z - R_GATES) // IN_TN
    for t in range(n_lead):
        w3_ref[t] = w_ref[:, t * IN_TN:(t + 1) * IN_TN].astype(BF16)
    for t in range(n_tail):
        w3_ref[n_lead + t] = w_ref[:, R_QA + t * IN_TN:R_QA + (t + 1) * IN_TN].astype(BF16)
    rest = N_Z - R_GATES - n_tail * IN_TN
    if rest:
        last = jnp.concatenate([w_ref[:, R_QA + n_tail * IN_TN:], jnp.zeros((rows, IN_TN - rest), F32)], axis=1)
        w3_ref[n_lead + n_tail] = last.astype(BF16)
    gate_cols = R_QA - R_GATES
    wg_ref[...] = jnp.concatenate([w_ref[:, R_GATES:R_QA], jnp.zeros((rows, LANES - gate_cols), F32)],
                                  axis=1).astype(BF16)


def _repack_w_in(w_in):
    n_tiles = N_ZP // IN_TN
    n_in = w_in.shape[2]
    return pl.pallas_call(
        _repack_kernel,
        grid=(DEPTH, D_MODEL // REPACK_ROWS),
        in_specs=[pl.BlockSpec((None, REPACK_ROWS, n_in), lambda l, i: (l, i, 0))],
        out_specs=[
            pl.BlockSpec((None, n_tiles, REPACK_ROWS, IN_TN), lambda l, i: (l, 0, i, 0)),
            pl.BlockSpec((None, REPACK_ROWS, LANES), lambda l, i: (l, i, 0)),
        ],
        out_shape=[
            jax.ShapeDtypeStruct((DEPTH, n_tiles, D_MODEL, IN_TN), BF16),
            jax.ShapeDtypeStruct((DEPTH, D_MODEL, LANES), BF16),
        ],
        compiler_params=_cparams(("parallel", "parallel")),
        name="repack_w_in",
    )(w_in)


def _inproj(layer, x, g, w_tiles, w_gates, cs, sa, sb):
    return pl.pallas_call(
        _inproj_kernel,
        grid=(M // TM, N_ZP // IN_TN),
        in_specs=[
            pl.BlockSpec((TM, D_MODEL), lambda i, j: (i, 0)),
            pl.BlockSpec((1, D_MODEL), lambda i, j: (0, 0)),
            pl.BlockSpec((None, None, D_MODEL, IN_TN), lambda i, j: (layer, j, 0, 0)),
            pl.BlockSpec((None, D_MODEL, LANES), lambda i, j: (layer, 0, 0)),
            pl.BlockSpec((TM, LANES), lambda i, j: (i, 0)),
            pl.BlockSpec((TM, LANES), lambda i, j: (i, 0)),
            pl.BlockSpec((TM, LANES), lambda i, j: (i, 0)),
        ],
        out_specs=[
            pl.BlockSpec((TM, IN_TN), lambda i, j: (i, j)),
            pl.BlockSpec((TM, KV_COLS), lambda i, j: (i, 0)),
            pl.BlockSpec((TM, LANES), lambda i, j: (i, 0)),
        ],
        out_shape=[
            jax.ShapeDtypeStruct((M, N_ZP), BF16),
            jax.ShapeDtypeStruct((M, KV_COLS), F32),
            jax.ShapeDtypeStruct((M, LANES), F32),
        ],
        scratch_shapes=[pltpu.VMEM((TM, D_MODEL), BF16)],
        compiler_params=_cparams(("parallel", "arbitrary")),
        name="inproj",
    )(x, g, w_tiles, w_gates, cs, sa, sb)


NC = SEQ // ML
MLSTM_HPS = 4


def _mlstm_prompt_kernel(q_ref, k_ref, v_ref, o_ref, gt_ref, bias_ref, gn_ref,
                         hm_ref, c_ref, n_ref, m_ref, ct_scr, n_scr, m_scr):
    hp = pl.program_id(1)
    c = pl.program_id(2)

    @pl.when(c == 0)
    def _():
        ct_scr[...] = jnp.zeros_like(ct_scr)
        n_scr[...] = jnp.zeros_like(n_scr)
        m_scr[...] = jnp.zeros_like(m_scr)

    gates_t = (gt_ref[...] + bias_ref[...]).T
    sub = lax.broadcasted_iota(jnp.int32, gates_t.shape, 0)
    src = lax.broadcasted_iota(jnp.int32, (ML, ML), 0)
    tgt = lax.broadcasted_iota(jnp.int32, (ML, ML), 1)
    causal = src <= tgt
    causal_f = causal.astype(F32)
    first_row = lax.broadcasted_iota(jnp.int32, (8, ML), 0) == 0
    qscale = M_DQK ** -0.5
    nt = (((1,), (1,)), ((), ()))
    finals = []
    for hh in range(MLSTM_HPS):
        h = hp * MLSTM_HPS + hh
        qcols = slice(hh * M_DQK, (hh + 1) * M_DQK)
        vcols = slice(hh * M_DV, (hh + 1) * M_DV)
        i_row = jnp.sum(jnp.where(sub == h, gates_t, 0.0), axis=0, keepdims=True)
        f_row = jnp.sum(jnp.where(sub == h + M_HEADS, gates_t, 0.0), axis=0, keepdims=True)
        lf8 = jnp.where(first_row, _log_sigmoid(f_row), 0.0)
        b_row = jnp.dot(lf8, causal_f, preferred_element_type=F32,
                        precision=lax.Precision.HIGHEST)[0:1, :]
        c_row = i_row - b_row
        c_col = jnp.where(sub == 0, c_row, 0.0).T[:, 0:1]

        m_prev = m_scr[hh]
        a_row = b_row + m_prev
        dmat = jnp.where(causal, b_row + c_col, -jnp.inf)
        m_row = jnp.maximum(a_row, jnp.max(dmat, axis=0, keepdims=True))
        w_intra = jnp.exp(dmat - m_row)
        w_inter = jnp.exp(a_row - m_row)

        q = q_ref[:, qcols]
        k = k_ref[:, qcols]
        v_t = v_ref[:, vcols].astype(F32).T.astype(BF16)
        ct = ct_scr[hh]
        n_prev = n_scr[hh]
        s_t = lax.dot_general(k, q, nt, preferred_element_type=F32) * qscale * w_intra
        inter = lax.dot_general(ct.astype(BF16), q, nt, preferred_element_type=F32) * qscale
        num = w_inter * inter + jnp.dot(v_t, s_t.astype(BF16), preferred_element_type=F32)
        n8 = jnp.broadcast_to(n_prev, (8, M_DQK)).astype(BF16)
        qn = lax.dot_general(n8, q, nt, preferred_element_type=F32)[0:1, :] * qscale
        den = w_inter * qn + jnp.sum(s_t, axis=0, keepdims=True)
        hd = num / jnp.maximum(jnp.abs(den), jnp.exp(-m_row))
        ms = jnp.mean(hd * hd, axis=0, keepdims=True)
        y_t = (hd * lax.rsqrt(ms + EPS)) * jnp.tile(gn_ref[hh], (1, ML // LANES))
        hm_ref[:, vcols] = (y_t.T * _sigmoid(o_ref[:, vcols].astype(F32))).astype(BF16)

        m_new = m_row[:, ML - 1:ML]
        b_last = b_row[:, ML - 1:ML]
        w_state = jnp.exp(c_col + (b_last - m_new))
        decay = jnp.exp(b_last + m_prev - m_new)
        kw = k.astype(F32) * w_state
        ct_new = decay * ct + jnp.dot(v_t, kw.astype(BF16), preferred_element_type=F32)
        n_new = decay * n_prev + jnp.sum(kw, axis=0, keepdims=True)
        ct_scr[hh] = ct_new
        n_scr[hh] = n_new
        m_scr[hh] = m_new
        finals.append((ct_new, n_new, m_new))

    @pl.when(c == NC - 1)
    def _():
        for hh, (ct_new, n_new, m_new) in enumerate(finals):
            c_ref[hh] = ct_new.T
            n_ref[hh] = n_new
            m_ref[hh] = jnp.broadcast_to(m_new, (1, LANES))


def _mlstm_prompt(z, gates, bias, gn):
    hps = MLSTM_HPS
    qb, vb = hps * M_DQK, hps * M_DV
    gn_cols = jnp.broadcast_to(gn.reshape(M_HEADS, M_DV, 1), (M_HEADS, M_DV, LANES))
    return pl.pallas_call(
        _mlstm_prompt_kernel,
        grid=(BATCH, M_HEADS // hps, NC),
        in_specs=[
            pl.BlockSpec((ML, qb), lambda b, h, c: (b * NC + c, C_QM // qb + h)),
            pl.BlockSpec((ML, qb), lambda b, h, c: (b * NC + c, C_KM // qb + h)),
            pl.BlockSpec((ML, vb), lambda b, h, c: (b * NC + c, C_VM // vb + h)),
            pl.BlockSpec((ML, vb), lambda b, h, c: (b * NC + c, C_OM // vb + h)),
            pl.BlockSpec((ML, LANES), lambda b, h, c: (b * NC + c, 0)),
            pl.BlockSpec((1, LANES), lambda b, h, c: (0, 0)),
            pl.BlockSpec((hps, M_DV, LANES), lambda b, h, c: (h, 0, 0)),
        ],
        out_specs=[
            pl.BlockSpec((ML, vb), lambda b, h, c: (b * NC + c, h)),
            pl.BlockSpec((None, hps, M_DQK, M_DV), lambda b, h, c: (b, h, 0, 0)),
            pl.BlockSpec((None, hps, 1, M_DQK), lambda b, h, c: (b, h, 0, 0)),
            pl.BlockSpec((None, hps, 1, LANES), lambda b, h, c: (b, h, 0, 0)),
        ],
        out_shape=[
            jax.ShapeDtypeStruct((MP, M_HEADS * M_DV), BF16),
            jax.ShapeDtypeStruct((BATCH, M_HEADS, M_DQK, M_DV), F32),
            jax.ShapeDtypeStruct((BATCH, M_HEADS, 1, M_DQK), F32),
            jax.ShapeDtypeStruct((BATCH, M_HEADS, 1, LANES), F32),
        ],
        scratch_shapes=[pltpu.VMEM((hps, M_DV, M_DQK), F32), pltpu.VMEM((hps, 1, M_DQK), F32),
                        pltpu.VMEM((hps, 1, 1), F32)],
        compiler_params=_cparams(("parallel", "parallel", "arbitrary")),
        name="mlstm_prompt",
    )(z, z, z, z, gates, bias, gn_cols)


def _mlstm_decode_kernel(q_ref, k_ref, v_ref, o_ref, qt_ref, kt_ref, gt_ref, bias_ref, m0_ref, gn_ref,
                         c0_ref, n0_ref, *rest, layer):
    if layer == 0:
        hm_ref, c_ref, n_ref, m_ref = rest
    else:
        _, hm_ref, c_ref, n_ref, m_ref = rest
    i = pl.program_id(0)
    h = pl.program_id(1)
    gates = gt_ref[...] + bias_ref[...]
    lane = lax.broadcasted_iota(jnp.int32, gates.shape, 1)
    log_f = pltpu.roll(_log_sigmoid(gates), LANES - M_HEADS, 1)
    a = log_f + m0_ref[...]
    m_t = jnp.maximum(a, gates)
    w_intra_all = jnp.exp(gates - m_t)
    w_inter_all = jnp.exp(a - m_t)
    floor_all = jnp.exp(-m_t)

    @pl.when(h == 0)
    def _():
        m_ref[...] = m_t

    def pick(arr):
        return jnp.sum(jnp.where(lane == h, arr, 0.0), axis=1, keepdims=True)

    wi = pick(w_intra_all)
    we = pick(w_inter_all)
    fl = pick(floor_all)
    qscale = M_DQK ** -0.5
    q = q_ref[...] * qscale
    k = k_ref[...]
    v = v_ref[...]
    n0 = n0_ref[...]
    s = jnp.sum(q * k, axis=1, keepdims=True) * wi
    den = we * jnp.sum(q * n0, axis=1, keepdims=True) + s
    dd = jnp.maximum(jnp.abs(den), fl)
    n_ref[...] = we * n0 + wi * k

    shift = lax.rem(LANES - i * TB, LANES)
    qt = pltpu.roll(qt_ref[...], shift, 1) * qscale
    kt = pltpu.roll(kt_ref[...], shift, 1)
    gn = gn_ref[...]
    sig_o = _sigmoid(o_ref[...])
    for j in range(TB):
        qc = qt[:, j:j + 1]
        kc = kt[:, j:j + 1]
        c0 = c0_ref[j]
        vj = v[j:j + 1, :]
        qc0 = jnp.sum(qc * c0, axis=0, keepdims=True)
        hrow = (we[j:j + 1, :] * qc0 + s[j:j + 1, :] * vj) / dd[j:j + 1, :]
        hm_ref[j:j + 1, :] = _rms(hrow, gn) * sig_o[j:j + 1, :]
        c_new = we[j:j + 1, :] * c0 + (wi[j:j + 1, :] * kc) * vj
        if layer == 0:
            c_ref[0, j] = c_new
            for d in range(1, DEPTH):
                c_ref[d, j] = jnp.zeros_like(c_new)
        else:
            c_ref[j] = c_new


def _mlstm_decode(layer, zs, qt, kt, gates_s, bias, m0p, gn, state_c, state_n, c_all):
    qb, vb = M_DQK, M_DV
    if layer == 0:
        c_spec = pl.BlockSpec((DEPTH, TB, None, M_DQK, M_DV), lambda i, h: (0, i, h, 0, 0))
        extra_in, extra_specs, aliases = (), [], {}
    else:
        c_spec = pl.BlockSpec((None, TB, None, M_DQK, M_DV), lambda i, h: (layer, i, h, 0, 0))
        extra_in, extra_specs, aliases = (c_all,), [pl.BlockSpec(memory_space=pl.ANY)], {12: 1}
    return pl.pallas_call(
        functools.partial(_mlstm_decode_kernel, layer=layer),
        grid=(MS // TB, M_HEADS),
        input_output_aliases=aliases,
        in_specs=[
            pl.BlockSpec((TB, qb), lambda i, h: (i, C_QM // qb + h)),
            pl.BlockSpec((TB, qb), lambda i, h: (i, C_KM // qb + h)),
            pl.BlockSpec((TB, vb), lambda i, h: (i, C_VM // vb + h)),
            pl.BlockSpec((TB, vb), lambda i, h: (i, C_OM // vb + h)),
            pl.BlockSpec((None, M_DQK, MS), lambda i, h: (h, 0, 0)),
            pl.BlockSpec((None, M_DQK, MS), lambda i, h: (h, 0, 0)),
            pl.BlockSpec((TB, LANES), lambda i, h: (i, 0)),
            pl.BlockSpec((1, LANES), lambda i, h: (0, 0)),
            pl.BlockSpec((TB, LANES), lambda i, h: (i, 0)),
            pl.BlockSpec((1, vb), lambda i, h: (0, h)),
            pl.BlockSpec((None, TB, None, M_DQK, M_DV), lambda i, h: (layer, i, h, 0, 0)),
            pl.BlockSpec((None, TB, M_DQK), lambda i, h: (layer, i, h)),
        ] + extra_specs,
        out_specs=[
            pl.BlockSpec((TB, vb), lambda i, h: (i, h)),
            c_spec,
            pl.BlockSpec((TB, M_DQK), lambda i, h: (i, h)),
            pl.BlockSpec((TB, LANES), lambda i, h: (i, 0)),
        ],
        out_shape=[
            jax.ShapeDtypeStruct((MS, M_HEADS * M_DV), F32),
            jax.ShapeDtypeStruct((DEPTH, MS, M_HEADS, M_DQK, M_DV), F32),
            jax.ShapeDtypeStruct((MS, M_HEADS * M_DQK), F32),
            jax.ShapeDtypeStruct((MS, LANES), F32),
        ],
        compiler_params=_cparams(("parallel", "arbitrary")),
        name="mlstm_decode",
    )(zs, zs, zs, zs, qt, kt, gates_s, bias, m0p, gn, state_c, state_n, *extra_in)


NB = SEQ // ATT_BLOCK
KVW = KV_HEADS * HEAD_DIM


def _swa_prompt_kernel(sink_ref, q_ref, kc_ref, kp_ref, vc_ref, vp_ref, o_ref):
    nb = pl.program_id(1)
    cols = GROUP * ATT_BLOCK
    sidx = lax.broadcasted_iota(jnp.int32, (2 * ATT_BLOCK, cols), 0)
    t = lax.broadcasted_iota(jnp.int32, (2 * ATT_BLOCK, cols), 1) & (ATT_BLOCK - 1)
    rel = t + ATT_BLOCK - sidx
    visible = jnp.logical_and(jnp.logical_and(rel >= 0, rel <= WINDOW),
                              jnp.logical_or(sidx >= ATT_BLOCK, nb > 0))
    bias = jnp.where(visible, 0.0, -jnp.inf)
    q = q_ref[...] * (HEAD_DIM ** -0.5)
    kk = jnp.concatenate([kp_ref[...], kc_ref[...]], axis=0).astype(BF16)
    vv_t = jnp.concatenate([vp_ref[...], vc_ref[...]], axis=0).T.astype(BF16)
    for g in range(KV_HEADS):
        qg = jnp.concatenate(
            [q[:, (g * GROUP + hh) * HEAD_DIM:(g * GROUP + hh + 1) * HEAD_DIM] for hh in range(GROUP)], axis=0)
        kg = kk[:, g * HEAD_DIM:(g + 1) * HEAD_DIM]
        vg_t = vv_t[g * HEAD_DIM:(g + 1) * HEAD_DIM, :]
        sink = jnp.concatenate(
            [jnp.full((1, ATT_BLOCK), sink_ref[g * GROUP + hh], F32) for hh in range(GROUP)], axis=1)
        s = lax.dot_general(kg, qg, (((1,), (1,)), ((), ())), preferred_element_type=F32) + bias
        mx = jnp.maximum(jnp.max(s, axis=0, keepdims=True), sink)
        p = jnp.exp(s - mx)
        denom = jnp.sum(p, axis=0, keepdims=True) + jnp.exp(sink - mx)
        o_t = jnp.dot(vg_t, p.astype(BF16), preferred_element_type=F32) / denom
        og = jnp.concatenate([o_t[:, hh * ATT_BLOCK:(hh + 1) * ATT_BLOCK].T for hh in range(GROUP)], axis=1)
        o_ref[:, g * GROUP * HEAD_DIM:(g + 1) * GROUP * HEAD_DIM] = og.astype(BF16)


def _swa_prompt(z, kv, sinks):
    qw = N_HEADS * HEAD_DIM
    return pl.pallas_call(
        _swa_prompt_kernel,
        grid=(BATCH, NB),
        in_specs=[
            pl.BlockSpec(memory_space=pltpu.SMEM),
            pl.BlockSpec((ATT_BLOCK, qw), lambda b, n: (b * NB + n, C_QA // qw)),
            pl.BlockSpec((ATT_BLOCK, KVW), lambda b, n: (b * NB + n, 0)),
            pl.BlockSpec((ATT_BLOCK, KVW), lambda b, n: (b * NB + jnp.maximum(n - 1, 0), 0)),
            pl.BlockSpec((ATT_BLOCK, KVW), lambda b, n: (b * NB + n, 1)),
            pl.BlockSpec((ATT_BLOCK, KVW), lambda b, n: (b * NB + jnp.maximum(n - 1, 0), 1)),
        ],
        out_specs=pl.BlockSpec((ATT_BLOCK, qw), lambda b, n: (b * NB + n, 0)),
        out_shape=jax.ShapeDtypeStruct((MP, qw), BF16),
        compiler_params=_cparams(("parallel", "arbitrary")),
        name="swa_prompt",
    )(sinks, z, kv, kv, kv, kv)


def _swa_decode_kernel(q_ref, kn_ref, vn_ref, knf_ref, vnf_ref, kb_ref, vb_ref, sink_ref, *rest, layer):
    if layer == 0:
        o_ref, kc_ref, vc_ref = rest
    else:
        _, _, o_ref, kc_ref, vc_ref = rest

    def put(ref, j, rows, val):
        if layer == 0:
            ref[0, j, rows, :] = val
        else:
            ref[j, rows, :] = val

    for j in range(TB):
        put(kc_ref, j, slice(0, WINDOW - 1), kb_ref[j, 1:WINDOW, :])
        put(kc_ref, j, slice(WINDOW - 1, WINDOW), knf_ref[j:j + 1, :])
        put(vc_ref, j, slice(0, WINDOW - 1), vb_ref[j, 1:WINDOW, :])
        put(vc_ref, j, slice(WINDOW - 1, WINDOW), vnf_ref[j:j + 1, :])
    if layer == 0:
        for d in range(1, DEPTH):
            kc_ref[d] = jnp.zeros(kc_ref.shape[1:], F32)
            vc_ref[d] = jnp.zeros(vc_ref.shape[1:], F32)

    rows, cols = TB * GROUP, TB * WINDOW
    own = (lax.broadcasted_iota(jnp.int32, (rows, cols), 0) // GROUP
           == lax.broadcasted_iota(jnp.int32, (rows, cols), 1) // WINDOW)
    bias = jnp.where(own, 0.0, -jnp.inf)
    scale = HEAD_DIM ** -0.5
    kstack = kb_ref[...].reshape(cols, KVW).astype(BF16)
    vstack = vb_ref[...].reshape(cols, KVW).astype(BF16)
    for g in range(KV_HEADS):
        qg = q_ref[:, g * GROUP:(g + 1) * GROUP, :].reshape(rows, HEAD_DIM) * scale
        kn = jnp.broadcast_to(kn_ref[:, g:g + 1, :], (TB, GROUP, HEAD_DIM)).reshape(rows, HEAD_DIM)
        vn = jnp.broadcast_to(vn_ref[:, g:g + 1, :], (TB, GROUP, HEAD_DIM)).reshape(rows, HEAD_DIM)
        sink = jnp.tile(sink_ref[g * GROUP:(g + 1) * GROUP, 0:1], (TB, 1))
        s_c = lax.dot_general(qg.astype(BF16), kstack[:, g * HEAD_DIM:(g + 1) * HEAD_DIM],
                              (((1,), (1,)), ((), ())), preferred_element_type=F32) + bias
        s_n = jnp.sum(qg * kn, axis=1, keepdims=True)
        mx = jnp.maximum(jnp.maximum(jnp.max(s_c, axis=1, keepdims=True), s_n), sink)
        p_c = jnp.exp(s_c - mx)
        p_n = jnp.exp(s_n - mx)
        denom = jnp.sum(p_c, axis=1, keepdims=True) + p_n + jnp.exp(sink - mx)
        o = jnp.dot(p_c.astype(BF16), vstack[:, g * HEAD_DIM:(g + 1) * HEAD_DIM], preferred_element_type=F32)
        o = (o + p_n * vn) / denom
        o_ref[:, g * GROUP:(g + 1) * GROUP, :] = o.reshape(TB, GROUP, HEAD_DIM)


def _swa_decode(layer, q3, kn3, vn3, knf, vnf, kbuf, vbuf, sinks_b, k_all, v_all):
    if layer == 0:
        cache_spec = pl.BlockSpec((DEPTH, TB, WINDOW, KVW), lambda i: (0, i, 0, 0))
        extra_in, extra_specs, aliases = (), [], {}
    else:
        cache_spec = pl.BlockSpec((None, TB, WINDOW, KVW), lambda i: (layer, i, 0, 0))
        extra_in = (k_all, v_all)
        extra_specs = [pl.BlockSpec(memory_space=pl.ANY), pl.BlockSpec(memory_space=pl.ANY)]
        aliases = {8: 1, 9: 2}
    return pl.pallas_call(
        functools.partial(_swa_decode_kernel, layer=layer),
        grid=(MS // TB,),
        input_output_aliases=aliases,
        in_specs=[
            pl.BlockSpec((TB, N_HEADS, HEAD_DIM), lambda i: (i, 0, 0)),
            pl.BlockSpec((TB, KV_HEADS, HEAD_DIM), lambda i: (i, 0, 0)),
            pl.BlockSpec((TB, KV_HEADS, HEAD_DIM), lambda i: (i, 0, 0)),
            pl.BlockSpec((TB, KVW), lambda i: (i, 0)),
            pl.BlockSpec((TB, KVW), lambda i: (i, 0)),
            pl.BlockSpec((None, TB, WINDOW, KVW), lambda i: (layer, i, 0, 0)),
            pl.BlockSpec((None, TB, WINDOW, KVW), lambda i: (layer, i, 0, 0)),
            pl.BlockSpec((N_HEADS, LANES), lambda i: (0, 0)),
        ] + extra_specs,
        out_specs=[
            pl.BlockSpec((TB, N_HEADS, HEAD_DIM), lambda i: (i, 0, 0)),
            cache_spec,
            cache_spec,
        ],
        out_shape=[
            jax.ShapeDtypeStruct((MS, N_HEADS, HEAD_DIM), F32),
            jax.ShapeDtypeStruct((DEPTH, MS, WINDOW, KVW), F32),
            jax.ShapeDtypeStruct((DEPTH, MS, WINDOW, KVW), F32),
        ],
        compiler_params=_cparams(("parallel",)),
        name="swa_decode",
    )(q3, kn3, vn3, knf, vnf, kbuf, vbuf, sinks_b, *extra_in)


def _merge_kernel(hm_ref, ha_ref, wbm_ref, wba_ref, gm_ref, ga_ref, o_ref):
    a = jnp.dot(hm_ref[...], wbm_ref[...], preferred_element_type=F32)
    b = jnp.dot(ha_ref[...], wba_ref[...], preferred_element_type=F32)
    o_ref[...] = (_sigmoid(gm_ref[...].astype(F32)) * a + _sigmoid(ga_ref[...].astype(F32)) * b).astype(BF16)


def _merge(hm, ha, wbm, wba, z):
    return pl.pallas_call(
        _merge_kernel,
        grid=(M // TM, D_MODEL // TN),
        in_specs=[
            pl.BlockSpec((TM, D_MODEL), lambda i, j: (i, 0)),
            pl.BlockSpec((TM, D_MODEL), lambda i, j: (i, 0)),
            pl.BlockSpec((D_MODEL, TN), lambda i, j: (0, j)),
            pl.BlockSpec((D_MODEL, TN), lambda i, j: (0, j)),
            pl.BlockSpec((TM, TN), lambda i, j: (i, C_GM // TN + j)),
            pl.BlockSpec((TM, TN), lambda i, j: (i, C_GA // TN + j)),
        ],
        out_specs=pl.BlockSpec((TM, TN), lambda i, j: (i, j)),
        out_shape=jax.ShapeDtypeStruct((M, D_MODEL), BF16),
        compiler_params=_cparams(("parallel", "arbitrary")),
        name="merge",
    )(hm, ha, wbm, wba, z, z)


def _mm_res_kernel(a_ref, w_ref, r_ref, o_ref):
    o_ref[...] = r_ref[...] + jnp.dot(a_ref[...], w_ref[...], preferred_element_type=F32)


def _mm_res(a, w, res, tm, tn, name):
    kdim = a.shape[1]
    n = w.shape[1]
    return pl.pallas_call(
        _mm_res_kernel,
        grid=(M // tm, n // tn),
        in_specs=[
            pl.BlockSpec((tm, kdim), lambda i, j: (i, 0)),
            pl.BlockSpec((kdim, tn), lambda i, j: (0, j)),
            pl.BlockSpec((tm, tn), lambda i, j: (i, j)),
        ],
        out_specs=pl.BlockSpec((tm, tn), lambda i, j: (i, j)),
        out_shape=jax.ShapeDtypeStruct((M, n), F32),
        compiler_params=_cparams(("parallel", "arbitrary")),
        name=name,
    )(a, w, res)


def _ffn_up_kernel(x_ref, g_ref, wg_ref, wu_ref, act_ref, h_scr):
    @pl.when(pl.program_id(1) == 0)
    def _():
        h_scr[...] = _rms(x_ref[...], g_ref[...]).astype(BF16)

    h = h_scr[...]
    a = jnp.dot(h, wg_ref[...], preferred_element_type=F32)
    u = jnp.dot(h, wu_ref[...], preferred_element_type=F32)
    act_ref[...] = ((a * _sigmoid(a)) * u).astype(BF16)


def _ffn_up(x, g, wg, wu):
    return pl.pallas_call(
        _ffn_up_kernel,
        grid=(M // TM, D_FF // TN),
        in_specs=[
            pl.BlockSpec((TM, D_MODEL), lambda i, j: (i, 0)),
            pl.BlockSpec((1, D_MODEL), lambda i, j: (0, 0)),
            pl.BlockSpec((D_MODEL, TN), lambda i, j: (0, j)),
            pl.BlockSpec((D_MODEL, TN), lambda i, j: (0, j)),
        ],
        out_specs=pl.BlockSpec((TM, TN), lambda i, j: (i, j)),
        out_shape=jax.ShapeDtypeStruct((M, D_FF), BF16),
        scratch_shapes=[pltpu.VMEM((TM, D_MODEL), BF16)],
        compiler_params=_cparams(("parallel", "arbitrary")),
        name="ffn_up",
    )(x, g, wg, wu)


def _router_kernel(x_ref, g_ref, wr_ref, ids_ref, gates_ref, cnt_ref, run_scr):
    @pl.when(pl.program_id(0) == 0)
    def _():
        run_scr[...] = jnp.zeros_like(run_scr)

    h = _rms(x_ref[...], g_ref[...])
    logits = jnp.dot(h, wr_ref[...], preferred_element_type=F32, precision=lax.Precision.HIGHEST)
    lane = lax.broadcasted_iota(jnp.int32, logits.shape, 1)
    lg = jnp.where(lane < N_EXPERTS, logits, -jnp.inf)
    v1 = jnp.max(lg, axis=1, keepdims=True)
    i1 = jnp.min(jnp.where(lg == v1, lane, LANES), axis=1, keepdims=True)
    lg2 = jnp.where(lane == i1, -jnp.inf, lg)
    v2 = jnp.max(lg2, axis=1, keepdims=True)
    i2 = jnp.min(jnp.where(lg2 == v2, lane, LANES), axis=1, keepdims=True)
    e2 = jnp.exp(v2 - v1)
    g1 = 1.0 / (1.0 + e2)
    g2 = e2 / (1.0 + e2)
    gates_ref[...] = jnp.where(lane == 0, g1, jnp.where(lane == 1, g2, 0.0))

    picks = jnp.where(jnp.logical_or(lane == i1, lane == i2), 1.0, 0.0)
    tm = picks.shape[0]
    earlier = lax.broadcasted_iota(jnp.int32, (tm, tm), 1) < lax.broadcasted_iota(jnp.int32, (tm, tm), 0)
    before = jnp.dot(jnp.where(earlier, 1.0, 0.0).astype(BF16), picks.astype(BF16),
                     preferred_element_type=F32) + run_scr[...]
    r1 = jnp.sum(jnp.where(lane == i1, before, 0.0), axis=1, keepdims=True).astype(jnp.int32)
    r2 = jnp.sum(jnp.where(lane == i2, before, 0.0), axis=1, keepdims=True).astype(jnp.int32)
    ids_ref[...] = jnp.where(lane == 0, i1, jnp.where(lane == 1, i2, jnp.where(lane == 2, r1, jnp.where(lane == 3, r2, 0))))
    total = run_scr[...] + jnp.sum(picks, axis=0, keepdims=True)
    run_scr[...] = total
    cnt_ref[...] = total


def _router(x, g, wr_pad):
    return pl.pallas_call(
        _router_kernel,
        grid=(M // TM,),
        in_specs=[
            pl.BlockSpec((TM, D_MODEL), lambda i: (i, 0)),
            pl.BlockSpec((1, D_MODEL), lambda i: (0, 0)),
            pl.BlockSpec((D_MODEL, LANES), lambda i: (0, 0)),
        ],
        out_specs=[
            pl.BlockSpec((TM, LANES), lambda i: (i, 0)),
            pl.BlockSpec((TM, LANES), lambda i: (i, 0)),
            pl.BlockSpec((1, LANES), lambda i: (0, 0)),
        ],
        out_shape=[
            jax.ShapeDtypeStruct((M, LANES), jnp.int32),
            jax.ShapeDtypeStruct((M, LANES), F32),
            jax.ShapeDtypeStruct((1, LANES), F32),
        ],
        scratch_shapes=[pltpu.VMEM((1, LANES), F32)],
        compiler_params=_cparams(("arbitrary",)),
        name="router",
    )(x, g, wr_pad)


def _row_copy(src_hbm, row, buf, r, sem):
    return pltpu.make_async_copy(src_hbm.at[pl.ds(row, 1), :], buf.at[pl.ds(r, 1), :], sem)


def _slot_gather_kernel(tok_ref, nu_ref, x_hbm, g_ref, xs_ref, buf, sem):
    m = pl.program_id(0)
    n_used = nu_ref[0]
    slot = lax.rem(m, 2)

    def request(tile, into):
        def issue(grp, carry):
            for u in range(8):
                r = grp * 8 + u
                _row_copy(x_hbm, tok_ref[tile * MOE_UP_TM + r], buf.at[into], r, sem.at[into]).start(priority=u % 2)
            return carry
        lax.fori_loop(0, MOE_UP_TM // 8, issue, 0)

    @pl.when(jnp.logical_and(m == 0, n_used > 0))
    def _():
        request(0, 0)

    @pl.when(m + 1 < n_used)
    def _():
        request(m + 1, 1 - slot)

    @pl.when(m < n_used)
    def _():
        def wait(r, carry):
            _row_copy(x_hbm, 0, buf.at[slot], r, sem.at[slot]).wait()
            return carry
        lax.fori_loop(0, MOE_UP_TM, wait, 0, unroll=8)
        xs_ref[...] = _rms(buf[slot], g_ref[...]).astype(BF16)

    @pl.when(m >= n_used)
    def _():
        xs_ref[...] = jnp.zeros_like(xs_ref)


def _slot_gather(slot_token, n_used, x, g):
    return pl.pallas_call(
        _slot_gather_kernel,
        grid_spec=pltpu.PrefetchScalarGridSpec(
            num_scalar_prefetch=2,
            grid=(N_SLOTS // MOE_UP_TM,),
            in_specs=[pl.BlockSpec(memory_space=pl.ANY),
                      pl.BlockSpec((1, D_MODEL), lambda m, tok, nu: (0, 0))],
            out_specs=pl.BlockSpec((MOE_UP_TM, D_MODEL), lambda m, tok, nu: (m, 0)),
            scratch_shapes=[pltpu.VMEM((2, MOE_UP_TM, D_MODEL), F32), pltpu.SemaphoreType.DMA((2,))],
        ),
        out_shape=jax.ShapeDtypeStruct((N_SLOTS, D_MODEL), BF16),
        compiler_params=_cparams(("arbitrary",)),
        name="slot_gather",
    )(slot_token, n_used, x, g)


def _combine_kernel(p1_ref, p2_ref, x_ref, y_hbm, gate_ref, g_ref, o_ref, buf1, buf2, sem, *, final_norm):
    base = pl.program_id(0) * COMBINE_ROWS

    def issue(r, carry):
        _row_copy(y_hbm, p1_ref[base + r], buf1, r, sem).start()
        _row_copy(y_hbm, p2_ref[base + r], buf2, r, sem).start()
        return carry
    lax.fori_loop(0, COMBINE_ROWS, issue, 0, unroll=8)

    def wait(r, carry):
        _row_copy(y_hbm, 0, buf1, r, sem).wait()
        _row_copy(y_hbm, 0, buf2, r, sem).wait()
        return carry
    lax.fori_loop(0, COMBINE_ROWS, wait, 0, unroll=8)
    gates = gate_ref[...]
    out = x_ref[...] + (gates[:, 0:1] * buf1[...] + gates[:, 1:2] * buf2[...])
    o_ref[...] = _rms(out, g_ref[...]) if final_norm else out


def _combine(pos1, pos2, x, y, gates, g_final):
    final_norm = g_final is not None
    g = g_final if final_norm else jnp.ones((1, D_MODEL), F32)
    return pl.pallas_call(
        functools.partial(_combine_kernel, final_norm=final_norm),
        grid_spec=pltpu.PrefetchScalarGridSpec(
            num_scalar_prefetch=2,
            grid=(M // COMBINE_ROWS,),
            in_specs=[pl.BlockSpec((COMBINE_ROWS, D_MODEL), lambda i, p1, p2: (i, 0)),
                      pl.BlockSpec(memory_space=pl.ANY),
                      pl.BlockSpec((COMBINE_ROWS, LANES), lambda i, p1, p2: (i, 0)),
                      pl.BlockSpec((1, D_MODEL), lambda i, p1, p2: (0, 0))],
            out_specs=pl.BlockSpec((COMBINE_ROWS, D_MODEL), lambda i, p1, p2: (i, 0)),
            scratch_shapes=[pltpu.VMEM((COMBINE_ROWS, D_MODEL), F32), pltpu.VMEM((COMBINE_ROWS, D_MODEL), F32),
                            pltpu.SemaphoreType.DMA(())],
        ),
        out_shape=jax.ShapeDtypeStruct((M, D_MODEL), F32),
        compiler_params=_cparams(("arbitrary",)),
        name="combine",
    )(pos1, pos2, x, y, gates, g)


def _moe_up_kernel(te_ref, nu_ref, x_ref, wg_ref, wu_ref, wdf_ref, act_ref, wdb_ref, wg_scr, wu_scr):
    m = pl.program_id(1)
    wdb_ref[...] = wdf_ref[...].astype(BF16)

    @pl.when(m < nu_ref[0])
    def _():
        prev = te_ref[jnp.maximum(m - 1, 0)]

        @pl.when(jnp.logical_or(m == 0, te_ref[m] != prev))
        def _():
            wg_scr[...] = wg_ref[...].astype(BF16)
            wu_scr[...] = wu_ref[...].astype(BF16)

        x = x_ref[...]
        a = jnp.dot(x, wg_scr[...], preferred_element_type=F32)
        u = jnp.dot(x, wu_scr[...], preferred_element_type=F32)
        act_ref[...] = ((a * _sigmoid(a)) * u).astype(BF16)

    @pl.when(m >= nu_ref[0])
    def _():
        act_ref[...] = jnp.zeros_like(act_ref)


def _side_chunk(total_rows, n_steps):
    chunk = BF16_SUBLANES
    while total_rows % chunk or total_rows // chunk > n_steps:
        chunk += BF16_SUBLANES
    return chunk


def _moe_up(tile_expert, n_used, xs, wg, wu, wd):
    n_j, n_m = E_FF // MOE_TF, N_SLOTS // MOE_UP_TM
    wd_rows = N_EXPERTS * E_FF
    chunk = _side_chunk(wd_rows, n_j * n_m)
    last_chunk = wd_rows // chunk - 1
    side = lambda j, m, te, nu: (jnp.minimum(j * n_m + m, last_chunk), 0)
    return pl.pallas_call(
        _moe_up_kernel,
        grid_spec=pltpu.PrefetchScalarGridSpec(
            num_scalar_prefetch=2,
            grid=(n_j, n_m),
            in_specs=[
                pl.BlockSpec((MOE_UP_TM, D_MODEL), lambda j, m, te, nu: (m, 0)),
                pl.BlockSpec((None, D_MODEL, MOE_TF), lambda j, m, te, nu: (te[m], 0, j)),
                pl.BlockSpec((None, D_MODEL, MOE_TF), lambda j, m, te, nu: (te[m], 0, j)),
                pl.BlockSpec((chunk, D_MODEL), side),
            ],
            out_specs=[pl.BlockSpec((MOE_UP_TM, MOE_TF), lambda j, m, te, nu: (m, j)),
                       pl.BlockSpec((chunk, D_MODEL), side)],
            scratch_shapes=[pltpu.VMEM((D_MODEL, MOE_TF), BF16), pltpu.VMEM((D_MODEL, MOE_TF), BF16)],
        ),
        out_shape=[jax.ShapeDtypeStruct((N_SLOTS, E_FF), BF16),
                   jax.ShapeDtypeStruct((wd_rows, D_MODEL), BF16)],
        compiler_params=_cparams(("arbitrary", "arbitrary")),
        name="moe_up",
    )(tile_expert, n_used, xs, wg, wu, wd.reshape(wd_rows, D_MODEL))


def _moe_down_kernel(te_ref, nu_ref, a_ref, wd_ref, y_ref):
    m = pl.program_id(1)

    @pl.when(m < nu_ref[0])
    def _():
        y_ref[...] = jnp.dot(a_ref[...], wd_ref[...], preferred_element_type=F32)

    @pl.when(m >= nu_ref[0])
    def _():
        y_ref[...] = jnp.zeros_like(y_ref)


def _moe_down(tile_expert, n_used, act, wd_bf16):
    return pl.pallas_call(
        _moe_down_kernel,
        grid_spec=pltpu.PrefetchScalarGridSpec(
            num_scalar_prefetch=2,
            grid=(D_MODEL // MOE_TN, N_SLOTS // MOE_DN_TM),
            in_specs=[
                pl.BlockSpec((MOE_DN_TM, E_FF), lambda j, m, te, nu: (m, 0)),
                pl.BlockSpec((None, E_FF, MOE_TN), lambda j, m, te, nu: (te[m], 0, j)),
            ],
            out_specs=pl.BlockSpec((MOE_DN_TM, MOE_TN), lambda j, m, te, nu: (m, j)),
        ),
        out_shape=jax.ShapeDtypeStruct((N_SLOTS, D_MODEL), F32),
        compiler_params=_cparams(("arbitrary", "arbitrary")),
        name="moe_down",
    )(tile_expert, n_used, act, wd_bf16.reshape(N_EXPERTS, E_FF, D_MODEL))


def _final_norm_kernel(x_ref, g_ref, o_ref):
    o_ref[...] = _rms(x_ref[...], g_ref[...])


def _final_norm(x, g):
    spec = pl.BlockSpec((COMBINE_ROWS, D_MODEL), lambda i: (i, 0))
    return pl.pallas_call(
        _final_norm_kernel,
        grid=(M // COMBINE_ROWS,),
        in_specs=[spec, pl.BlockSpec((1, D_MODEL), lambda i: (0, 0))],
        out_specs=spec,
        out_shape=jax.ShapeDtypeStruct((M, D_MODEL), F32),
        compiler_params=_cparams(("parallel",)),
        name="final_norm",
    )(x, g)


def _tile_map(ends, tm):
    n_tiles = N_SLOTS // tm
    n_used = (ends[-1] // tm).astype(jnp.int32)
    tile_start = jnp.arange(n_tiles, dtype=jnp.int32) * tm
    tile_expert = jnp.sum((tile_start[:, None] >= ends[None, :]).astype(jnp.int32), axis=1)
    last = jnp.sum((jnp.maximum(n_used - 1, 0) * tm >= ends).astype(jnp.int32))
    tile_expert = jnp.where(jnp.arange(n_tiles) < n_used, tile_expert, last)
    return jnp.minimum(tile_expert, N_EXPERTS - 1).astype(jnp.int32), n_used.reshape(1)


def _route(ids, counts):
    counts = counts[0, :N_EXPERTS].astype(jnp.int32)
    padded = ((counts + MOE_PAD - 1) // MOE_PAD) * MOE_PAD
    ends = jnp.cumsum(padded)
    starts = ends - padded
    pos1 = starts[ids[:, 0]] + ids[:, 2]
    pos2 = starts[ids[:, 1]] + ids[:, 3]
    tok = jnp.arange(M, dtype=jnp.int32)
    slot_token = jnp.zeros((N_SLOTS,), jnp.int32).at[jnp.concatenate([pos1, pos2])].set(jnp.concatenate([tok, tok]))
    return slot_token, _tile_map(ends, MOE_UP_TM), _tile_map(ends, MOE_DN_TM), pos1, pos2


def _rope_tables():
    half = ROT_DIM // 2
    pos = jnp.concatenate([jnp.tile(jnp.arange(SEQ, dtype=jnp.int32), BATCH),
                           jnp.full((MS,), PAST_LEN, jnp.int32)])
    inv = jnp.power(ROPE_THETA, -jnp.arange(half, dtype=F32) / half)
    ang = pos.astype(F32)[:, None] * inv[None, :]
    cos, sin = jnp.cos(ang), jnp.sin(ang)
    pad = HEAD_DIM - ROT_DIM
    one = jnp.ones((M, pad), F32)
    zero = jnp.zeros((M, pad), F32)
    zh = jnp.zeros((M, half), F32)
    cs = jnp.concatenate([cos, cos, one], axis=1)
    sa = jnp.concatenate([-sin, zh, zero], axis=1)
    sb = jnp.concatenate([zh, sin, zero], axis=1)
    rep = LANES // HEAD_DIM
    return jnp.tile(cs, (1, rep)), jnp.tile(sa, (1, rep)), jnp.tile(sb, (1, rep))


def _pad_lanes(a):
    return jnp.pad(a, ((0, 0), (0, LANES - a.shape[1])))


def kernel(x_prompt, x_sample, state_mlstm_C, state_mlstm_n, state_mlstm_m, cache_swa_k, cache_swa_v, norm_mix_g, w_in, b_igate, b_fgate, mlstm_norm_g, attn_sinks, w_branch_m, w_branch_a, w_out, norm_ffn_g, w_gate_dense, w_up_dense, w_down_dense, w_router, w_gate_moe, w_up_moe, w_down_moe, norm_final_g):
    x = jnp.concatenate([x_prompt.reshape(MP, D_MODEL), x_sample.reshape(MS, D_MODEL)], axis=0)
    cs, sa, sb = _rope_tables()
    state_n = state_mlstm_n.reshape(DEPTH, MS, M_HEADS * M_DQK)
    kbuf = cache_swa_k.reshape(DEPTH, MS, WINDOW, KVW)
    vbuf = cache_swa_v.reshape(DEPTH, MS, WINDOW, KVW)
    outs = {name: [] for name in ("Cp", "np", "mp", "kp", "vp", "ns", "ms")}
    y_final = None
    c_all = k_all = v_all = None

    w_tiles, w_gates = _repack_w_in(w_in)

    for l in range(DEPTH):
        bias = _pad_lanes(jnp.concatenate([b_igate[l], b_fgate[l]])[None, :])
        gn = mlstm_norm_g[l][None, :]
        sinks_b = jnp.broadcast_to(attn_sinks[l][:, None], (N_HEADS, LANES))

        z, kv, gates = _inproj(l, x, norm_mix_g[l][None, :], w_tiles, w_gates, cs, sa, sb)

        hm_p, c_p, n_p, m_p = _mlstm_prompt(z, gates, bias, gn)
        ha_p = _swa_prompt(z, kv, attn_sinks[l])
        kv_p = kv[:MP].reshape(BATCH, SEQ, 2 * KVW)[:, SEQ - WINDOW:].reshape(BATCH, WINDOW, 2, KV_HEADS, HEAD_DIM)
        outs["Cp"].append(c_p)
        outs["np"].append(n_p.reshape(BATCH, M_HEADS, M_DQK))
        outs["mp"].append(m_p[:, :, 0, 0])
        outs["kp"].append(kv_p[:, :, 0])
        outs["vp"].append(kv_p[:, :, 1])

        zs = z[MP:].astype(F32)
        kv_s = kv[MP:]
        qt = zs[:, C_QM:C_KM].T.reshape(M_HEADS, M_DQK, MS)
        kt = zs[:, C_KM:C_VM].T.reshape(M_HEADS, M_DQK, MS)
        m0p = _pad_lanes(state_mlstm_m[l])
        hm_s, c_all, n_s, m_s = _mlstm_decode(l, zs, qt, kt, gates[MP:], bias, m0p, gn, state_mlstm_C, state_n, c_all)
        kn = kv_s[:, :KVW]
        vn = kv_s[:, KVW:]
        ha_s, k_all, v_all = _swa_decode(
            l, zs[:, C_QA:C_KA].reshape(MS, N_HEADS, HEAD_DIM), kn.reshape(MS, KV_HEADS, HEAD_DIM),
            vn.reshape(MS, KV_HEADS, HEAD_DIM), kn, vn, kbuf, vbuf, sinks_b, k_all, v_all)
        outs["ns"].append(n_s.reshape(MS, M_HEADS, M_DQK))
        outs["ms"].append(m_s[:, :M_HEADS])

        hm = jnp.concatenate([hm_p, hm_s.astype(BF16)], axis=0)
        ha = jnp.concatenate([ha_p, ha_s.reshape(MS, N_HEADS * HEAD_DIM).astype(BF16)], axis=0)
        merged = _merge(hm, ha, w_branch_m[l].astype(BF16), w_branch_a[l].astype(BF16), z)
        x = _mm_res(merged, w_out[l].astype(BF16), x, OUT_TM, D_MODEL, "out_proj")

        if l % 2 == 0:
            jd = l // 2
            act = _ffn_up(x, norm_ffn_g[l][None, :], w_gate_dense[jd].astype(BF16), w_up_dense[jd].astype(BF16))
            x = _mm_res(act, w_down_dense[jd].astype(BF16), x, TM, TN, "ffn_down")
        else:
            jm = l // 2
            g_ffn = norm_ffn_g[l][None, :]
            ids, gts, counts = _router(x, g_ffn, _pad_lanes(w_router[jm]))
            slot_token, (te_up, nu_up), (te_dn, nu_dn), pos1, pos2 = _route(ids, counts)
            xs = _slot_gather(slot_token, nu_up, x, g_ffn)
            act, wd_bf16 = _moe_up(te_up, nu_up, xs, w_gate_moe[jm], w_up_moe[jm], w_down_moe[jm])
            y = _moe_down(te_dn, nu_dn, act, wd_bf16)
            if l == DEPTH - 1:
                y_final = _combine(pos1, pos2, x, y, gts, norm_final_g[None, :])
            else:
                x = _combine(pos1, pos2, x, y, gts, None)

    if y_final is None:
        y_final = _final_norm(x, norm_final_g[None, :])
    y_prompt = y_final[:MP].reshape(BATCH, SEQ, D_MODEL)
    y_sample = y_final[MP:].reshape(MS, 1, D_MODEL)
    st = lambda name: jnp.stack(outs[name])
    return (y_prompt, y_sample, st("Cp"), st("np"), st("mp"), st("kp"), st("vp"),
            c_all, st("ns"), st("ms"),
            k_all.reshape(DEPTH, MS, WINDOW, KV_HEADS, HEAD_DIM), v_all.reshape(DEPTH, MS, WINDOW, KV_HEADS, HEAD_DIM))
```

```python
import functools

import jax
import jax.numpy as jnp
import numpy as np
from jax import lax
from jax.experimental import pallas as pl
from jax.experimental.pallas import tpu as pltpu

F32 = jnp.float32
BF16 = jnp.bfloat16

D_MODEL = 2048
BATCH = 4
SEQ = 2048
DEPTH = 2
DEC_BATCH = 128
PAST_LEN = 8192
M_HEADS = 4
M_DQK = 256
M_DV = 512
N_HEADS = 32
KV_HEADS = 4
HEAD_DIM = 64
GROUP = N_HEADS // KV_HEADS
ROT_DIM = HEAD_DIM // 4
ROPE_THETA = 500000.0
WINDOW = 128
ATT_BLOCK = 128
D_FF = 5632
N_EXPERTS = 8
TOP_K = 2
E_FF = 7168
EPS = 1e-6

MP = BATCH * SEQ
MS = DEC_BATCH
M = MP + MS

C_QM = 0
C_KM = C_QM + M_HEADS * M_DQK
C_VM = C_KM + M_HEADS * M_DQK
C_OM = C_VM + M_HEADS * M_DV
C_QA = C_OM + M_HEADS * M_DV
C_KA = C_QA + N_HEADS * HEAD_DIM
C_VA = C_KA + KV_HEADS * HEAD_DIM
C_GM = C_VA + KV_HEADS * HEAD_DIM
C_GA = C_GM + D_MODEL
N_Z = C_GA + D_MODEL
R_GATES = 2 * M_HEADS * M_DQK + 2 * M_HEADS * M_DV
R_QA = R_GATES + 2 * M_HEADS

LANES = 128
BF16_SUBLANES = 16
TM = 1040
TN = 512
OUT_TM = 640
ML = 256
TB = 8
MOE_PAD = 512
MOE_UP_TM = MOE_PAD
MOE_DN_TM = MOE_PAD
MOE_TF = 512
MOE_TN = 512
COMBINE_ROWS = TM // 2
N_SLOTS = -(-(TOP_K * M + N_EXPERTS * (MOE_PAD - 1)) // MOE_PAD) * MOE_PAD
VMEM_LIMIT = 56 * 1024 * 1024


def _cparams(sem, vmem=VMEM_LIMIT):
    return pltpu.CompilerParams(dimension_semantics=sem, vmem_limit_bytes=vmem)


NT_DIMS = (((1,), (1,)), ((), ()))


def _rms(x, g):
    ms = jnp.mean(x * x, axis=-1, keepdims=True)
    return (x * lax.rsqrt(ms + EPS)) * g


def _sigmoid(x):
    return 1.0 / (1.0 + jnp.exp(-x))


def _log_sigmoid(x):
    return jnp.minimum(x, 0.0) - jnp.log(1.0 + jnp.exp(-jnp.abs(x)))


IN_TN = 1024
KV_COLS = C_GM - C_KA
J_QA0 = C_QA // IN_TN
J_KV = C_KA // IN_TN
N_ZP = -(-N_Z // IN_TN) * IN_TN
assert C_QA % IN_TN == 0 and C_KA % IN_TN == 0 and KV_COLS <= IN_TN


def _rope(acc, cs, sa, sb):
    n = acc.shape[1]
    return acc * cs + pltpu.roll(acc, n - ROT_DIM // 2, 1) * sa + pltpu.roll(acc, ROT_DIM // 2, 1) * sb


def _inproj_kernel(x_ref, g_ref, w_ref, wgate_ref, cs_ref, sa_ref, sb_ref, z_ref, kv_ref, gt_ref, h_scr):
    j = pl.program_id(1)

    @pl.when(j == 0)
    def _():
        h = _rms(x_ref[...], g_ref[...]).astype(BF16)
        h_scr[...] = h
        gt_ref[...] = lax.dot_general(h, wgate_ref[...], NT_DIMS, preferred_element_type=F32)

    acc = lax.dot_general(h_scr[...], w_ref[...], NT_DIMS, preferred_element_type=F32)
    z_ref[...] = acc.astype(BF16)
    reps = IN_TN // LANES
    is_q = jnp.logical_and(j >= J_QA0, j < J_KV)
    is_kv = j == J_KV

    @pl.when(is_q)
    def _():
        cs = jnp.tile(cs_ref[...], (1, reps))
        sa = jnp.tile(sa_ref[...], (1, reps))
        sb = jnp.tile(sb_ref[...], (1, reps))
        z_ref[...] = _rope(acc, cs, sa, sb).astype(BF16)

    @pl.when(is_kv)
    def _():
        is_k = lax.broadcasted_iota(jnp.int32, acc.shape, 1) < (C_VA - C_KA)
        cs = jnp.where(is_k, jnp.tile(cs_ref[...], (1, reps)), 1.0)
        sa = jnp.where(is_k, jnp.tile(sa_ref[...], (1, reps)), 0.0)
        sb = jnp.where(is_k, jnp.tile(sb_ref[...], (1, reps)), 0.0)
        r = _rope(acc, cs, sa, sb)
        kv_ref[...] = r[:, :KV_COLS]
        z_ref[...] = r.astype(BF16)


N_IN_TILES = N_ZP // IN_TN
N_LEAD_TILES = R_GATES // IN_TN
TAIL_ROWS = N_Z - (N_IN_TILES - 1 - N_LEAD_TILES) * IN_TN - R_GATES
GATE_COLS = R_QA - R_GATES
assert R_GATES % IN_TN == 0 and 0 < TAIL_ROWS <= IN_TN and TAIL_ROWS % 8 == 0 and R_QA % 8 == 0


def _repack_kernel(wt_hbm, w3_ref, wg_ref, buf, gbuf, sem, gsem):
    layer = pl.program_id(0)
    t = pl.program_id(1)
    step = layer * N_IN_TILES + t
    slot = lax.rem(step, 2)

    def src_row(tile):
        return jnp.where(tile < N_LEAD_TILES, tile * IN_TN, R_QA + (tile - N_LEAD_TILES) * IN_TN)

    def full_copy(lyr, tile, into):
        return pltpu.make_async_copy(wt_hbm.at[lyr, pl.ds(pl.multiple_of(src_row(tile), 8), IN_TN), :],
                                     buf.at[into], sem.at[into])

    def tail_copy(lyr, into):
        start = R_QA + (N_IN_TILES - 1 - N_LEAD_TILES) * IN_TN
        return pltpu.make_async_copy(wt_hbm.at[lyr, pl.ds(start, TAIL_ROWS), :],
                                     buf.at[into, pl.ds(0, TAIL_ROWS), :], sem.at[into])

    def fetch(lyr, tile, into):
        @pl.when(tile < N_IN_TILES - 1)
        def _():
            full_copy(lyr, tile, into).start()

        @pl.when(tile == N_IN_TILES - 1)
        def _():
            tail_copy(lyr, into).start()

    @pl.when(step == 0)
    def _():
        fetch(layer, t, slot)

    @pl.when(step + 1 < DEPTH * N_IN_TILES)
    def _():
        nxt = step + 1
        fetch(nxt // N_IN_TILES, lax.rem(nxt, N_IN_TILES), 1 - slot)

    @pl.when(t == 0)
    def _():
        gates = pltpu.make_async_copy(wt_hbm.at[layer, pl.ds(R_GATES, GATE_COLS), :], gbuf, gsem)
        gates.start()
        gates.wait()
        wg_ref[...] = jnp.concatenate([gbuf[...], jnp.zeros((LANES - GATE_COLS, D_MODEL), F32)], axis=0).astype(BF16)

    @pl.when(t < N_IN_TILES - 1)
    def _():
        full_copy(layer, t, slot).wait()
        w3_ref[...] = buf[slot].astype(BF16)

    @pl.when(t == N_IN_TILES - 1)
    def _():
        tail_copy(layer, slot).wait()
        valid = lax.broadcasted_iota(jnp.int32, (IN_TN, D_MODEL), 0) < TAIL_ROWS
        w3_ref[...] = jnp.where(valid, buf[slot], 0.0).astype(BF16)


def _repack_w_in(w_in):
    wt = jnp.swapaxes(w_in, 1, 2)
    return pl.pallas_call(
        _repack_kernel,
        grid=(DEPTH, N_IN_TILES),
        in_specs=[pl.BlockSpec(memory_space=pl.ANY)],
        out_specs=[
            pl.BlockSpec((None, None, IN_TN, D_MODEL), lambda l, t: (l, t, 0, 0)),
            pl.BlockSpec((None, LANES, D_MODEL), lambda l, t: (l, 0, 0)),
        ],
        out_shape=[
            jax.ShapeDtypeStruct((DEPTH, N_IN_TILES, IN_TN, D_MODEL), BF16),
            jax.ShapeDtypeStruct((DEPTH, LANES, D_MODEL), BF16),
        ],
        scratch_shapes=[pltpu.VMEM((2, IN_TN, D_MODEL), F32), pltpu.VMEM((GATE_COLS, D_MODEL), F32),
                        pltpu.SemaphoreType.DMA((2,)), pltpu.SemaphoreType.DMA(())],
        compiler_params=_cparams(("arbitrary", "arbitrary")),
        name="repack_w_in",
    )(wt)


def _inproj(layer, x, g, w_tiles, w_gates, cs, sa, sb):
    return pl.pallas_call(
        _inproj_kernel,
        grid=(M // TM, N_ZP // IN_TN),
        in_specs=[
            pl.BlockSpec((TM, D_MODEL), lambda i, j: (i, 0)),
            pl.BlockSpec((1, D_MODEL), lambda i, j: (0, 0)),
            pl.BlockSpec((None, None, IN_TN, D_MODEL), lambda i, j: (layer, j, 0, 0)),
            pl.BlockSpec((None, LANES, D_MODEL), lambda i, j: (layer, 0, 0)),
            pl.BlockSpec((TM, LANES), lambda i, j: (i, 0)),
            pl.BlockSpec((TM, LANES), lambda i, j: (i, 0)),
            pl.BlockSpec((TM, LANES), lambda i, j: (i, 0)),
        ],
        out_specs=[
            pl.BlockSpec((TM, IN_TN), lambda i, j: (i, j)),
            pl.BlockSpec((TM, KV_COLS), lambda i, j: (i, 0)),
            pl.BlockSpec((TM, LANES), lambda i, j: (i, 0)),
        ],
        out_shape=[
            jax.ShapeDtypeStruct((M, N_ZP), BF16),
            jax.ShapeDtypeStruct((M, KV_COLS), F32),
            jax.ShapeDtypeStruct((M, LANES), F32),
        ],
        scratch_shapes=[pltpu.VMEM((TM, D_MODEL), BF16)],
        compiler_params=_cparams(("parallel", "arbitrary")),
        name="inproj",
    )(x, g, w_tiles, w_gates, cs, sa, sb)


NC = SEQ // ML
MLSTM_HPS = 4


def _mlstm_prompt_kernel(q_ref, k_ref, v_ref, o_ref, gt_ref, bias_ref, gn_ref,
                         hm_ref, c_ref, n_ref, m_ref, ct_scr, n_scr, m_scr):
    hp = pl.program_id(1)
    c = pl.program_id(2)

    @pl.when(c == 0)
    def _():
        ct_scr[...] = jnp.zeros_like(ct_scr)
        n_scr[...] = jnp.zeros_like(n_scr)
        m_scr[...] = jnp.zeros_like(m_scr)

    gates_t = (gt_ref[...] + bias_ref[...]).T
    sub = lax.broadcasted_iota(jnp.int32, gates_t.shape, 0)
    src = lax.broadcasted_iota(jnp.int32, (ML, ML), 0)
    tgt = lax.broadcasted_iota(jnp.int32, (ML, ML), 1)
    causal = src <= tgt
    causal_f = causal.astype(F32)
    first_row = lax.broadcasted_iota(jnp.int32, (8, ML), 0) == 0
    qscale = M_DQK ** -0.5
    nt = (((1,), (1,)), ((), ()))
    finals = []
    for hh in range(MLSTM_HPS):
        h = hp * MLSTM_HPS + hh
        qcols = slice(hh * M_DQK, (hh + 1) * M_DQK)
        vcols = slice(hh * M_DV, (hh + 1) * M_DV)
        i_row = jnp.sum(jnp.where(sub == h, gates_t, 0.0), axis=0, keepdims=True)
        f_row = jnp.sum(jnp.where(sub == h + M_HEADS, gates_t, 0.0), axis=0, keepdims=True)
        lf8 = jnp.where(first_row, _log_sigmoid(f_row), 0.0)
        b_row = jnp.dot(lf8, causal_f, preferred_element_type=F32,
                        precision=lax.Precision.HIGHEST)[0:1, :]
        c_row = i_row - b_row
        c_col = jnp.where(sub == 0, c_row, 0.0).T[:, 0:1]

        m_prev = m_scr[hh]
        a_row = b_row + m_prev
        dmat = jnp.where(causal, b_row + c_col, -jnp.inf)
        m_row = jnp.maximum(a_row, jnp.max(dmat, axis=0, keepdims=True))
        w_intra = jnp.exp(dmat - m_row)
        w_inter = jnp.exp(a_row - m_row)

        q = q_ref[:, qcols]
        k = k_ref[:, qcols]
        v_t = v_ref[:, vcols].astype(F32).T.astype(BF16)
        ct = ct_scr[hh]
        n_prev = n_scr[hh]
        s_t = lax.dot_general(k, q, nt, preferred_element_type=F32) * qscale * w_intra
        inter = lax.dot_general(ct.astype(BF16), q, nt, preferred_element_type=F32) * qscale
        num = w_inter * inter + jnp.dot(v_t, s_t.astype(BF16), preferred_element_type=F32)
        n8 = jnp.broadcast_to(n_prev, (8, M_DQK)).astype(BF16)
        qn = lax.dot_general(n8, q, nt, preferred_element_type=F32)[0:1, :] * qscale
        den = w_inter * qn + jnp.sum(s_t, axis=0, keepdims=True)
        hd = num / jnp.maximum(jnp.abs(den), jnp.exp(-m_row))
        ms = jnp.mean(hd * hd, axis=0, keepdims=True)
        y_t = (hd * lax.rsqrt(ms + EPS)) * jnp.tile(gn_ref[hh], (1, ML // LANES))
        hm_ref[:, vcols] = (y_t.T * _sigmoid(o_ref[:, vcols].astype(F32))).astype(BF16)

        m_new = m_row[:, ML - 1:ML]
        b_last = b_row[:, ML - 1:ML]
        w_state = jnp.exp(c_col + (b_last - m_new))
        decay = jnp.exp(b_last + m_prev - m_new)
        kw = k.astype(F32) * w_state
        ct_new = decay * ct + jnp.dot(v_t, kw.astype(BF16), preferred_element_type=F32)
        n_new = decay * n_prev + jnp.sum(kw, axis=0, keepdims=True)
        ct_scr[hh] = ct_new
        n_scr[hh] = n_new
        m_scr[hh] = m_new
        finals.append((ct_new, n_new, m_new))

    @pl.when(c == NC - 1)
    def _():
        for hh, (ct_new, n_new, m_new) in enumerate(finals):
            c_ref[hh] = ct_new.T
            n_ref[hh] = n_new
            m_ref[hh] = jnp.broadcast_to(m_new, (1, LANES))


def _mlstm_prompt(z, gates, bias, gn):
    hps = MLSTM_HPS
    qb, vb = hps * M_DQK, hps * M_DV
    gn_cols = jnp.broadcast_to(gn.reshape(M_HEADS, M_DV, 1), (M_HEADS, M_DV, LANES))
    return pl.pallas_call(
        _mlstm_prompt_kernel,
        grid=(BATCH, M_HEADS // hps, NC),
        in_specs=[
            pl.BlockSpec((ML, qb), lambda b, h, c: (b * NC + c, C_QM // qb + h)),
            pl.BlockSpec((ML, qb), lambda b, h, c: (b * NC + c, C_KM // qb + h)),
            pl.BlockSpec((ML, vb), lambda b, h, c: (b * NC + c, C_VM // vb + h)),
            pl.BlockSpec((ML, vb), lambda b, h, c: (b * NC + c, C_OM // vb + h)),
            pl.BlockSpec((ML, LANES), lambda b, h, c: (b * NC + c, 0)),
            pl.BlockSpec((1, LANES), lambda b, h, c: (0, 0)),
            pl.BlockSpec((hps, M_DV, LANES), lambda b, h, c: (h, 0, 0)),
        ],
        out_specs=[
            pl.BlockSpec((ML, vb), lambda b, h, c: (b * NC + c, h)),
            pl.BlockSpec((None, hps, M_DQK, M_DV), lambda b, h, c: (b, h, 0, 0)),
            pl.BlockSpec((None, hps, 1, M_DQK), lambda b, h, c: (b, h, 0, 0)),
            pl.BlockSpec((None, hps, 1, LANES), lambda b, h, c: (b, h, 0, 0)),
        ],
        out_shape=[
            jax.ShapeDtypeStruct((MP, M_HEADS * M_DV), BF16),
            jax.ShapeDtypeStruct((BATCH, M_HEADS, M_DQK, M_DV), F32),
            jax.ShapeDtypeStruct((BATCH, M_HEADS, 1, M_DQK), F32),
            jax.ShapeDtypeStruct((BATCH, M_HEADS, 1, LANES), F32),
        ],
        scratch_shapes=[pltpu.VMEM((hps, M_DV, M_DQK), F32), pltpu.VMEM((hps, 1, M_DQK), F32),
                        pltpu.VMEM((hps, 1, 1), F32)],
        compiler_params=_cparams(("parallel", "parallel", "arbitrary")),
        name="mlstm_prompt",
    )(z, z, z, z, gates, bias, gn_cols)


def _mlstm_decode_kernel(q_ref, k_ref, v_ref, o_ref, qt_ref, kt_ref, gt_ref, bias_ref, m0_ref, gn_ref,
                         c0_ref, n0_ref, *rest, layer):
    if layer == 0:
        hm_ref, c_ref, n_ref, m_ref = rest
    else:
        _, hm_ref, c_ref, n_ref, m_ref = rest
    i = pl.program_id(0)
    h = pl.program_id(1)
    gates = gt_ref[...] + bias_ref[...]
    lane = lax.broadcasted_iota(jnp.int32, gates.shape, 1)
    log_f = pltpu.roll(_log_sigmoid(gates), LANES - M_HEADS, 1)
    a = log_f + m0_ref[...]
    m_t = jnp.maximum(a, gates)
    w_intra_all = jnp.exp(gates - m_t)
    w_inter_all = jnp.exp(a - m_t)
    floor_all = jnp.exp(-m_t)

    @pl.when(h == 0)
    def _():
        m_ref[...] = m_t

    def pick(arr):
        return jnp.sum(jnp.where(lane == h, arr, 0.0), axis=1, keepdims=True)

    wi = pick(w_intra_all)
    we = pick(w_inter_all)
    fl = pick(floor_all)
    qscale = M_DQK ** -0.5
    q = q_ref[...] * qscale
    k = k_ref[...]
    v = v_ref[...]
    n0 = n0_ref[...]
    s = jnp.sum(q * k, axis=1, keepdims=True) * wi
    den = we * jnp.sum(q * n0, axis=1, keepdims=True) + s
    dd = jnp.maximum(jnp.abs(den), fl)
    n_ref[...] = we * n0 + wi * k

    shift = lax.rem(LANES - i * TB, LANES)
    qt = pltpu.roll(qt_ref[...], shift, 1) * qscale
    kt = pltpu.roll(kt_ref[...], shift, 1)
    gn = gn_ref[...]
    sig_o = _sigmoid(o_ref[...])
    for j in range(TB):
        qc = qt[:, j:j + 1]
        kc = kt[:, j:j + 1]
        c0 = c0_ref[j]
        vj = v[j:j + 1, :]
        qc0 = jnp.sum(qc * c0, axis=0, keepdims=True)
        hrow = (we[j:j + 1, :] * qc0 + s[j:j + 1, :] * vj) / dd[j:j + 1, :]
        hm_ref[j:j + 1, :] = _rms(hrow, gn) * sig_o[j:j + 1, :]
        c_new = we[j:j + 1, :] * c0 + (wi[j:j + 1, :] * kc) * vj
        if layer == 0:
            c_ref[0, j] = c_new
            for d in range(1, DEPTH):
                c_ref[d, j] = jnp.zeros_like(c_new)
        else:
            c_ref[j] = c_new


def _mlstm_decode(layer, zs, qt, kt, gates_s, bias, m0p, gn, state_c, state_n, c_all):
    qb, vb = M_DQK, M_DV
    if layer == 0:
        c_spec = pl.BlockSpec((DEPTH, TB, None, M_DQK, M_DV), lambda i, h: (0, i, h, 0, 0))
        extra_in, extra_specs, aliases = (), [], {}
    else:
        c_spec = pl.BlockSpec((None, TB, None, M_DQK, M_DV), lambda i, h: (layer, i, h, 0, 0))
        extra_in, extra_specs, aliases = (c_all,), [pl.BlockSpec(memory_space=pl.ANY)], {12: 1}
    return pl.pallas_call(
        functools.partial(_mlstm_decode_kernel, layer=layer),
        grid=(MS // TB, M_HEADS),
        input_output_aliases=aliases,
        in_specs=[
            pl.BlockSpec((TB, qb), lambda i, h: (i, C_QM // qb + h)),
            pl.BlockSpec((TB, qb), lambda i, h: (i, C_KM // qb + h)),
            pl.BlockSpec((TB, vb), lambda i, h: (i, C_VM // vb + h)),
            pl.BlockSpec((TB, vb), lambda i, h: (i, C_OM // vb + h)),
            pl.BlockSpec((None, M_DQK, MS), lambda i, h: (h, 0, 0)),
            pl.BlockSpec((None, M_DQK, MS), lambda i, h: (h, 0, 0)),
            pl.BlockSpec((TB, LANES), lambda i, h: (i, 0)),
            pl.BlockSpec((1, LANES), lambda i, h: (0, 0)),
            pl.BlockSpec((TB, LANES), lambda i, h: (i, 0)),
            pl.BlockSpec((1, vb), lambda i, h: (0, h)),
            pl.BlockSpec((None, TB, None, M_DQK, M_DV), lambda i, h: (layer, i, h, 0, 0)),
            pl.BlockSpec((None, TB, M_DQK), lambda i, h: (layer, i, h)),
        ] + extra_specs,
        out_specs=[
            pl.BlockSpec((TB, vb), lambda i, h: (i, h)),
            c_spec,
            pl.BlockSpec((TB, M_DQK), lambda i, h: (i, h)),
            pl.BlockSpec((TB, LANES), lambda i, h: (i, 0)),
        ],
        out_shape=[
            jax.ShapeDtypeStruct((MS, M_HEADS * M_DV), F32),
            jax.ShapeDtypeStruct((DEPTH, MS, M_HEADS, M_DQK, M_DV), F32),
            jax.ShapeDtypeStruct((MS, M_HEADS * M_DQK), F32),
            jax.ShapeDtypeStruct((MS, LANES), F32),
        ],
        compiler_params=_cparams(("parallel", "arbitrary")),
        name="mlstm_decode",
    )(zs, zs, zs, zs, qt, kt, gates_s, bias, m0p, gn, state_c, state_n, *extra_in)


NB = SEQ // ATT_BLOCK
KVW = KV_HEADS * HEAD_DIM


def _swa_prompt_kernel(sink_ref, q_ref, kc_ref, kp_ref, vc_ref, vp_ref, o_ref):
    nb = pl.program_id(1)
    cols = GROUP * ATT_BLOCK
    sidx = lax.broadcasted_iota(jnp.int32, (2 * ATT_BLOCK, cols), 0)
    t = lax.broadcasted_iota(jnp.int32, (2 * ATT_BLOCK, cols), 1) & (ATT_BLOCK - 1)
    rel = t + ATT_BLOCK - sidx
    visible = jnp.logical_and(jnp.logical_and(rel >= 0, rel <= WINDOW),
                              jnp.logical_or(sidx >= ATT_BLOCK, nb > 0))
    bias = jnp.where(visible, 0.0, -jnp.inf)
    q = q_ref[...] * (HEAD_DIM ** -0.5)
    kk = jnp.concatenate([kp_ref[...], kc_ref[...]], axis=0).astype(BF16)
    vv_t = jnp.concatenate([vp_ref[...], vc_ref[...]], axis=0).T.astype(BF16)
    for g in range(KV_HEADS):
        qg = jnp.concatenate(
            [q[:, (g * GROUP + hh) * HEAD_DIM:(g * GROUP + hh + 1) * HEAD_DIM] for hh in range(GROUP)], axis=0)
        kg = kk[:, g * HEAD_DIM:(g + 1) * HEAD_DIM]
        vg_t = vv_t[g * HEAD_DIM:(g + 1) * HEAD_DIM, :]
        sink = jnp.concatenate(
            [jnp.full((1, ATT_BLOCK), sink_ref[g * GROUP + hh], F32) for hh in range(GROUP)], axis=1)
        s = lax.dot_general(kg, qg, (((1,), (1,)), ((), ())), preferred_element_type=F32) + bias
        mx = jnp.maximum(jnp.max(s, axis=0, keepdims=True), sink)
        p = jnp.exp(s - mx)
        denom = jnp.sum(p, axis=0, keepdims=True) + jnp.exp(sink - mx)
        o_t = jnp.dot(vg_t, p.astype(BF16), preferred_element_type=F32) / denom
        og = jnp.concatenate([o_t[:, hh * ATT_BLOCK:(hh + 1) * ATT_BLOCK].T for hh in range(GROUP)], axis=1)
        o_ref[:, g * GROUP * HEAD_DIM:(g + 1) * GROUP * HEAD_DIM] = og.astype(BF16)


def _swa_prompt(z, kv, sinks):
    qw = N_HEADS * HEAD_DIM
    return pl.pallas_call(
        _swa_prompt_kernel,
        grid=(BATCH, NB),
        in_specs=[
            pl.BlockSpec(memory_space=pltpu.SMEM),
            pl.BlockSpec((ATT_BLOCK, qw), lambda b, n: (b * NB + n, C_QA // qw)),
            pl.BlockSpec((ATT_BLOCK, KVW), lambda b, n: (b * NB + n, 0)),
            pl.BlockSpec((ATT_BLOCK, KVW), lambda b, n: (b * NB + jnp.maximum(n - 1, 0), 0)),
            pl.BlockSpec((ATT_BLOCK, KVW), lambda b, n: (b * NB + n, 1)),
            pl.BlockSpec((ATT_BLOCK, KVW), lambda b, n: (b * NB + jnp.maximum(n - 1, 0), 1)),
        ],
        out_specs=pl.BlockSpec((ATT_BLOCK, qw), lambda b, n: (b * NB + n, 0)),
        out_shape=jax.ShapeDtypeStruct((MP, qw), BF16),
        compiler_params=_cparams(("parallel", "arbitrary")),
        name="swa_prompt",
    )(sinks, z, kv, kv, kv, kv)


def _swa_decode_kernel(q_ref, kn_ref, vn_ref, knf_ref, vnf_ref, kb_ref, vb_ref, sink_ref, *rest, layer):
    if layer == 0:
        o_ref, kc_ref, vc_ref = rest
    else:
        _, _, o_ref, kc_ref, vc_ref = rest

    def put(ref, j, rows, val):
        if layer == 0:
            ref[0, j, rows, :] = val
        else:
            ref[j, rows, :] = val

    for j in range(TB):
        put(kc_ref, j, slice(0, WINDOW - 1), kb_ref[j, 1:WINDOW, :])
        put(kc_ref, j, slice(WINDOW - 1, WINDOW), knf_ref[j:j + 1, :])
        put(vc_ref, j, slice(0, WINDOW - 1), vb_ref[j, 1:WINDOW, :])
        put(vc_ref, j, slice(WINDOW - 1, WINDOW), vnf_ref[j:j + 1, :])
    if layer == 0:
        for d in range(1, DEPTH):
            kc_ref[d] = jnp.zeros(kc_ref.shape[1:], F32)
            vc_ref[d] = jnp.zeros(vc_ref.shape[1:], F32)

    rows, cols = TB * GROUP, TB * WINDOW
    own = (lax.broadcasted_iota(jnp.int32, (rows, cols), 0) // GROUP
           == lax.broadcasted_iota(jnp.int32, (rows, cols), 1) // WINDOW)
    bias = jnp.where(own, 0.0, -jnp.inf)
    scale = HEAD_DIM ** -0.5
    kstack = kb_ref[...].reshape(cols, KVW).astype(BF16)
    vstack = vb_ref[...].reshape(cols, KVW).astype(BF16)
    for g in range(KV_HEADS):
        qg = q_ref[:, g * GROUP:(g + 1) * GROUP, :].reshape(rows, HEAD_DIM) * scale
        kn = jnp.broadcast_to(kn_ref[:, g:g + 1, :], (TB, GROUP, HEAD_DIM)).reshape(rows, HEAD_DIM)
        vn = jnp.broadcast_to(vn_ref[:, g:g + 1, :], (TB, GROUP, HEAD_DIM)).reshape(rows, HEAD_DIM)
        sink = jnp.tile(sink_ref[g * GROUP:(g + 1) * GROUP, 0:1], (TB, 1))
        s_c = lax.dot_general(qg.astype(BF16), kstack[:, g * HEAD_DIM:(g + 1) * HEAD_DIM],
                              (((1,), (1,)), ((), ())), preferred_element_type=F32) + bias
        s_n = jnp.sum(qg * kn, axis=1, keepdims=True)
        mx = jnp.maximum(jnp.maximum(jnp.max(s_c, axis=1, keepdims=True), s_n), sink)
        p_c = jnp.exp(s_c - mx)
        p_n = jnp.exp(s_n - mx)
        denom = jnp.sum(p_c, axis=1, keepdims=True) + p_n + jnp.exp(sink - mx)
        o = jnp.dot(p_c.astype(BF16), vstack[:, g * HEAD_DIM:(g + 1) * HEAD_DIM], preferred_element_type=F32)
        o = (o + p_n * vn) / denom
        o_ref[:, g * GROUP:(g + 1) * GROUP, :] = o.reshape(TB, GROUP, HEAD_DIM)


def _swa_decode(layer, q3, kn3, vn3, knf, vnf, kbuf, vbuf, sinks_b, k_all, v_all):
    if layer == 0:
        cache_spec = pl.BlockSpec((DEPTH, TB, WINDOW, KVW), lambda i: (0, i, 0, 0))
        extra_in, extra_specs, aliases = (), [], {}
    else:
        cache_spec = pl.BlockSpec((None, TB, WINDOW, KVW), lambda i: (layer, i, 0, 0))
        extra_in = (k_all, v_all)
        extra_specs = [pl.BlockSpec(memory_space=pl.ANY), pl.BlockSpec(memory_space=pl.ANY)]
        aliases = {8: 1, 9: 2}
    return pl.pallas_call(
        functools.partial(_swa_decode_kernel, layer=layer),
        grid=(MS // TB,),
        input_output_aliases=aliases,
        in_specs=[
            pl.BlockSpec((TB, N_HEADS, HEAD_DIM), lambda i: (i, 0, 0)),
            pl.BlockSpec((TB, KV_HEADS, HEAD_DIM), lambda i: (i, 0, 0)),
            pl.BlockSpec((TB, KV_HEADS, HEAD_DIM), lambda i: (i, 0, 0)),
            pl.BlockSpec((TB, KVW), lambda i: (i, 0)),
            pl.BlockSpec((TB, KVW), lambda i: (i, 0)),
            pl.BlockSpec((None, TB, WINDOW, KVW), lambda i: (layer, i, 0, 0)),
            pl.BlockSpec((None, TB, WINDOW, KVW), lambda i: (layer, i, 0, 0)),
            pl.BlockSpec((N_HEADS, LANES), lambda i: (0, 0)),
        ] + extra_specs,
        out_specs=[
            pl.BlockSpec((TB, N_HEADS, HEAD_DIM), lambda i: (i, 0, 0)),
            cache_spec,
            cache_spec,
        ],
        out_shape=[
            jax.ShapeDtypeStruct((MS, N_HEADS, HEAD_DIM), F32),
            jax.ShapeDtypeStruct((DEPTH, MS, WINDOW, KVW), F32),
            jax.ShapeDtypeStruct((DEPTH, MS, WINDOW, KVW), F32),
        ],
        compiler_params=_cparams(("parallel",)),
        name="swa_decode",
    )(q3, kn3, vn3, knf, vnf, kbuf, vbuf, sinks_b, *extra_in)


def _merge_kernel(hm_ref, ha_ref, wbm_ref, wba_ref, gm_ref, ga_ref, o_ref):
    a = jnp.dot(hm_ref[...], wbm_ref[...], preferred_element_type=F32)
    b = jnp.dot(ha_ref[...], wba_ref[...], preferred_element_type=F32)
    o_ref[...] = (_sigmoid(gm_ref[...].astype(F32)) * a + _sigmoid(ga_ref[...].astype(F32)) * b).astype(BF16)


def _merge(hm, ha, wbm, wba, z):
    return pl.pallas_call(
        _merge_kernel,
        grid=(M // TM, D_MODEL // TN),
        in_specs=[
            pl.BlockSpec((TM, D_MODEL), lambda i, j: (i, 0)),
            pl.BlockSpec((TM, D_MODEL), lambda i, j: (i, 0)),
            pl.BlockSpec((D_MODEL, TN), lambda i, j: (0, j)),
            pl.BlockSpec((D_MODEL, TN), lambda i, j: (0, j)),
            pl.BlockSpec((TM, TN), lambda i, j: (i, C_GM // TN + j)),
            pl.BlockSpec((TM, TN), lambda i, j: (i, C_GA // TN + j)),
        ],
        out_specs=pl.BlockSpec((TM, TN), lambda i, j: (i, j)),
        out_shape=jax.ShapeDtypeStruct((M, D_MODEL), BF16),
        compiler_params=_cparams(("parallel", "arbitrary")),
        name="merge",
    )(hm, ha, wbm, wba, z, z)


def _mm_res_kernel(a_ref, w_ref, r_ref, o_ref):
    o_ref[...] = r_ref[...] + jnp.dot(a_ref[...], w_ref[...], preferred_element_type=F32)


def _mm_res(a, w, res, tm, tn, name):
    kdim = a.shape[1]
    n = w.shape[1]
    return pl.pallas_call(
        _mm_res_kernel,
        grid=(M // tm, n // tn),
        in_specs=[
            pl.BlockSpec((tm, kdim), lambda i, j: (i, 0)),
            pl.BlockSpec((kdim, tn), lambda i, j: (0, j)),
            pl.BlockSpec((tm, tn), lambda i, j: (i, j)),
        ],
        out_specs=pl.BlockSpec((tm, tn), lambda i, j: (i, j)),
        out_shape=jax.ShapeDtypeStruct((M, n), F32),
        compiler_params=_cparams(("parallel", "arbitrary")),
        name=name,
    )(a, w, res)


def _ffn_up_kernel(x_ref, g_ref, wg_ref, wu_ref, act_ref, h_scr):
    @pl.when(pl.program_id(1) == 0)
    def _():
        h_scr[...] = _rms(x_ref[...], g_ref[...]).astype(BF16)

    h = h_scr[...]
    a = jnp.dot(h, wg_ref[...], preferred_element_type=F32)
    u = jnp.dot(h, wu_ref[...], preferred_element_type=F32)
    act_ref[...] = ((a * _sigmoid(a)) * u).astype(BF16)


def _ffn_up(x, g, wg, wu):
    return pl.pallas_call(
        _ffn_up_kernel,
        grid=(M // TM, D_FF // TN),
        in_specs=[
            pl.BlockSpec((TM, D_MODEL), lambda i, j: (i, 0)),
            pl.BlockSpec((1, D_MODEL), lambda i, j: (0, 0)),
            pl.BlockSpec((D_MODEL, TN), lambda i, j: (0, j)),
            pl.BlockSpec((D_MODEL, TN), lambda i, j: (0, j)),
        ],
        out_specs=pl.BlockSpec((TM, TN), lambda i, j: (i, j)),
        out_shape=jax.ShapeDtypeStruct((M, D_FF), BF16),
        scratch_shapes=[pltpu.VMEM((TM, D_MODEL), BF16)],
        compiler_params=_cparams(("parallel", "arbitrary")),
        name="ffn_up",
    )(x, g, wg, wu)


def _router_kernel(x_ref, g_ref, wr_ref, ids_ref, gates_ref, cnt_ref, run_scr):
    @pl.when(pl.program_id(0) == 0)
    def _():
        run_scr[...] = jnp.zeros_like(run_scr)

    h = _rms(x_ref[...], g_ref[...])
    logits = jnp.dot(h, wr_ref[...], preferred_element_type=F32, precision=lax.Precision.HIGHEST)
    lane = lax.broadcasted_iota(jnp.int32, logits.shape, 1)
    lg = jnp.where(lane < N_EXPERTS, logits, -jnp.inf)
    v1 = jnp.max(lg, axis=1, keepdims=True)
    i1 = jnp.min(jnp.where(lg == v1, lane, LANES), axis=1, keepdims=True)
    lg2 = jnp.where(lane == i1, -jnp.inf, lg)
    v2 = jnp.max(lg2, axis=1, keepdims=True)
    i2 = jnp.min(jnp.where(lg2 == v2, lane, LANES), axis=1, keepdims=True)
    e2 = jnp.exp(v2 - v1)
    g1 = 1.0 / (1.0 + e2)
    g2 = e2 / (1.0 + e2)
    gates_ref[...] = jnp.where(lane == 0, g1, jnp.where(lane == 1, g2, 0.0))

    picks = jnp.where(jnp.logical_or(lane == i1, lane == i2), 1.0, 0.0)
    tm = picks.shape[0]
    earlier = lax.broadcasted_iota(jnp.int32, (tm, tm), 1) < lax.broadcasted_iota(jnp.int32, (tm, tm), 0)
    before = jnp.dot(jnp.where(earlier, 1.0, 0.0).astype(BF16), picks.astype(BF16),
                     preferred_element_type=F32) + run_scr[...]
    r1 = jnp.sum(jnp.where(lane == i1, before, 0.0), axis=1, keepdims=True).astype(jnp.int32)
    r2 = jnp.sum(jnp.where(lane == i2, before, 0.0), axis=1, keepdims=True).astype(jnp.int32)
    ids_ref[...] = jnp.where(lane == 0, i1, jnp.where(lane == 1, i2, jnp.where(lane == 2, r1, jnp.where(lane == 3, r2, 0))))
    total = run_scr[...] + jnp.sum(picks, axis=0, keepdims=True)
    run_scr[...] = total
    cnt_ref[...] = total


def _router(x, g, wr_pad):
    return pl.pallas_call(
        _router_kernel,
        grid=(M // TM,),
        in_specs=[
            pl.BlockSpec((TM, D_MODEL), lambda i: (i, 0)),
            pl.BlockSpec((1, D_MODEL), lambda i: (0, 0)),
            pl.BlockSpec((D_MODEL, LANES), lambda i: (0, 0)),
        ],
        out_specs=[
            pl.BlockSpec((TM, LANES), lambda i: (i, 0)),
            pl.BlockSpec((TM, LANES), lambda i: (i, 0)),
            pl.BlockSpec((1, LANES), lambda i: (0, 0)),
        ],
        out_shape=[
            jax.ShapeDtypeStruct((M, LANES), jnp.int32),
            jax.ShapeDtypeStruct((M, LANES), F32),
            jax.ShapeDtypeStruct((1, LANES), F32),
        ],
        scratch_shapes=[pltpu.VMEM((1, LANES), F32)],
        compiler_params=_cparams(("arbitrary",)),
        name="router",
    )(x, g, wr_pad)


def _row_copy(src_hbm, row, buf, r, sem):
    return pltpu.make_async_copy(src_hbm.at[pl.ds(row, 1), :], buf.at[pl.ds(r, 1), :], sem)


def _slot_gather_kernel(tok_ref, nu_ref, x_hbm, g_ref, xs_ref, buf, sem):
    m = pl.program_id(0)
    n_used = nu_ref[0]
    slot = lax.rem(m, 2)

    def request(tile, into):
        def issue(grp, carry):
            for u in range(8):
                r = grp * 8 + u
                _row_copy(x_hbm, tok_ref[tile * MOE_UP_TM + r], buf.at[into], r, sem.at[into]).start(priority=u % 2)
            return carry
        lax.fori_loop(0, MOE_UP_TM // 8, issue, 0)

    @pl.when(jnp.logical_and(m == 0, n_used > 0))
    def _():
        request(0, 0)

    @pl.when(m + 1 < n_used)
    def _():
        request(m + 1, 1 - slot)

    @pl.when(m < n_used)
    def _():
        def wait(r, carry):
            _row_copy(x_hbm, 0, buf.at[slot], r, sem.at[slot]).wait()
            return carry
        lax.fori_loop(0, MOE_UP_TM, wait, 0, unroll=8)
        xs_ref[...] = _rms(buf[slot], g_ref[...]).astype(BF16)

    @pl.when(m >= n_used)
    def _():
        xs_ref[...] = jnp.zeros_like(xs_ref)


def _slot_gather(slot_token, n_used, x, g):
    return pl.pallas_call(
        _slot_gather_kernel,
        grid_spec=pltpu.PrefetchScalarGridSpec(
            num_scalar_prefetch=2,
            grid=(N_SLOTS // MOE_UP_TM,),
            in_specs=[pl.BlockSpec(memory_space=pl.ANY),
                      pl.BlockSpec((1, D_MODEL), lambda m, tok, nu: (0, 0))],
            out_specs=pl.BlockSpec((MOE_UP_TM, D_MODEL), lambda m, tok, nu: (m, 0)),
            scratch_shapes=[pltpu.VMEM((2, MOE_UP_TM, D_MODEL), F32), pltpu.SemaphoreType.DMA((2,))],
        ),
        out_shape=jax.ShapeDtypeStruct((N_SLOTS, D_MODEL), BF16),
        compiler_params=_cparams(("arbitrary",)),
        name="slot_gather",
    )(slot_token, n_used, x, g)


def _combine_kernel(p1_ref, p2_ref, x_ref, y_hbm, gate_ref, g_ref, o_ref, buf1, buf2, sem, *, final_norm):
    base = pl.program_id(0) * COMBINE_ROWS

    def issue(r, carry):
        _row_copy(y_hbm, p1_ref[base + r], buf1, r, sem).start()
        _row_copy(y_hbm, p2_ref[base + r], buf2, r, sem).start()
        return carry
    lax.fori_loop(0, COMBINE_ROWS, issue, 0, unroll=8)

    def wait(r, carry):
        _row_copy(y_hbm, 0, buf1, r, sem).wait()
        _row_copy(y_hbm, 0, buf2, r, sem).wait()
        return carry
    lax.fori_loop(0, COMBINE_ROWS, wait, 0, unroll=8)
    gates = gate_ref[...]
    out = x_ref[...] + (gates[:, 0:1] * buf1[...] + gates[:, 1:2] * buf2[...])
    o_ref[...] = _rms(out, g_ref[...]) if final_norm else out


def _combine(pos1, pos2, x, y, gates, g_final):
    final_norm = g_final is not None
    g = g_final if final_norm else jnp.ones((1, D_MODEL), F32)
    return pl.pallas_call(
        functools.partial(_combine_kernel, final_norm=final_norm),
        grid_spec=pltpu.PrefetchScalarGridSpec(
            num_scalar_prefetch=2,
            grid=(M // COMBINE_ROWS,),
            in_specs=[pl.BlockSpec((COMBINE_ROWS, D_MODEL), lambda i, p1, p2: (i, 0)),
                      pl.BlockSpec(memory_space=pl.ANY),
                      pl.BlockSpec((COMBINE_ROWS, LANES), lambda i, p1, p2: (i, 0)),
                      pl.BlockSpec((1, D_MODEL), lambda i, p1, p2: (0, 0))],
            out_specs=pl.BlockSpec((COMBINE_ROWS, D_MODEL), lambda i, p1, p2: (i, 0)),
            scratch_shapes=[pltpu.VMEM((COMBINE_ROWS, D_MODEL), F32), pltpu.VMEM((COMBINE_ROWS, D_MODEL), F32),
                            pltpu.SemaphoreType.DMA(())],
        ),
        out_shape=jax.ShapeDtypeStruct((M, D_MODEL), F32),
        compiler_params=_cparams(("arbitrary",)),
        name="combine",
    )(pos1, pos2, x, y, gates, g)


def _moe_up_kernel(te_ref, nu_ref, x_ref, wg_ref, wu_ref, wdf_ref, act_ref, wdb_ref, wg_scr, wu_scr):
    m = pl.program_id(1)
    for cidx in range(D_MODEL // MOE_TN):
        wdb_ref[cidx] = wdf_ref[:, cidx * MOE_TN:(cidx + 1) * MOE_TN].astype(BF16)

    @pl.when(m < nu_ref[0])
    def _():
        prev = te_ref[jnp.maximum(m - 1, 0)]

        @pl.when(jnp.logical_or(m == 0, te_ref[m] != prev))
        def _():
            wg_scr[...] = wg_ref[...].astype(BF16)
            wu_scr[...] = wu_ref[...].astype(BF16)

        x = x_ref[...]
        a = jnp.dot(x, wg_scr[...], preferred_element_type=F32)
        u = jnp.dot(x, wu_scr[...], preferred_element_type=F32)
        act_ref[...] = ((a * _sigmoid(a)) * u).astype(BF16)

    @pl.when(m >= nu_ref[0])
    def _():
        act_ref[...] = jnp.zeros_like(act_ref)


def _side_chunk(total_rows, n_steps):
    chunk = BF16_SUBLANES
    while total_rows % chunk or total_rows // chunk > n_steps:
        chunk += BF16_SUBLANES
    return chunk


def _moe_up(tile_expert, n_used, xs, wg, wu, wd):
    n_j, n_m = E_FF // MOE_TF, N_SLOTS // MOE_UP_TM
    wd_rows = N_EXPERTS * E_FF
    chunk = _side_chunk(wd_rows, n_j * n_m)
    last_chunk = wd_rows // chunk - 1
    side = lambda j, m, te, nu: (jnp.minimum(j * n_m + m, last_chunk), 0)
    return pl.pallas_call(
        _moe_up_kernel,
        grid_spec=pltpu.PrefetchScalarGridSpec(
            num_scalar_prefetch=2,
            grid=(n_j, n_m),
            in_specs=[
                pl.BlockSpec((MOE_UP_TM, D_MODEL), lambda j, m, te, nu: (m, 0)),
                pl.BlockSpec((None, D_MODEL, MOE_TF), lambda j, m, te, nu: (te[m], 0, j)),
                pl.BlockSpec((None, D_MODEL, MOE_TF), lambda j, m, te, nu: (te[m], 0, j)),
                pl.BlockSpec((chunk, D_MODEL), side),
            ],
            out_specs=[pl.BlockSpec((None, MOE_UP_TM, MOE_TF), lambda j, m, te, nu: (j, m, 0)),
                       pl.BlockSpec((D_MODEL // MOE_TN, chunk, MOE_TN),
                                    lambda j, m, te, nu: (0, jnp.minimum(j * n_m + m, last_chunk), 0))],
            scratch_shapes=[pltpu.VMEM((D_MODEL, MOE_TF), BF16), pltpu.VMEM((D_MODEL, MOE_TF), BF16)],
        ),
        out_shape=[jax.ShapeDtypeStruct((n_j, N_SLOTS, MOE_TF), BF16),
                   jax.ShapeDtypeStruct((D_MODEL // MOE_TN, wd_rows, MOE_TN), BF16)],
        compiler_params=_cparams(("arbitrary", "arbitrary")),
        name="moe_up",
    )(tile_expert, n_used, xs, wg, wu, wd.reshape(wd_rows, D_MODEL))


def _moe_down_kernel(te_ref, nu_ref, a_ref, wd_ref, y_ref):
    m = pl.program_id(1)

    @pl.when(m < nu_ref[0])
    def _():
        acc = jnp.dot(a_ref[0], wd_ref[0:MOE_TF, :], preferred_element_type=F32)
        for f in range(1, E_FF // MOE_TF):
            acc = acc + jnp.dot(a_ref[f], wd_ref[f * MOE_TF:(f + 1) * MOE_TF, :], preferred_element_type=F32)
        y_ref[...] = acc

    @pl.when(m >= nu_ref[0])
    def _():
        y_ref[...] = jnp.zeros_like(y_ref)


def _moe_down(tile_expert, n_used, act, wd_tiles):
    return pl.pallas_call(
        _moe_down_kernel,
        grid_spec=pltpu.PrefetchScalarGridSpec(
            num_scalar_prefetch=2,
            grid=(D_MODEL // MOE_TN, N_SLOTS // MOE_DN_TM),
            in_specs=[
                pl.BlockSpec((E_FF // MOE_TF, MOE_DN_TM, MOE_TF), lambda j, m, te, nu: (0, m, 0)),
                pl.BlockSpec((None, E_FF, MOE_TN), lambda j, m, te, nu: (j, te[m], 0)),
            ],
            out_specs=pl.BlockSpec((MOE_DN_TM, MOE_TN), lambda j, m, te, nu: (m, j)),
        ),
        out_shape=jax.ShapeDtypeStruct((N_SLOTS, D_MODEL), F32),
        compiler_params=_cparams(("arbitrary", "arbitrary")),
        name="moe_down",
    )(tile_expert, n_used, act, wd_tiles)


def _final_norm_kernel(x_ref, g_ref, o_ref):
    o_ref[...] = _rms(x_ref[...], g_ref[...])


def _final_norm(x, g):
    spec = pl.BlockSpec((COMBINE_ROWS, D_MODEL), lambda i: (i, 0))
    return pl.pallas_call(
        _final_norm_kernel,
        grid=(M // COMBINE_ROWS,),
        in_specs=[spec, pl.BlockSpec((1, D_MODEL), lambda i: (0, 0))],
        out_specs=spec,
        out_shape=jax.ShapeDtypeStruct((M, D_MODEL), F32),
        compiler_params=_cparams(("parallel",)),
        name="final_norm",
    )(x, g)


def _tile_map(ends, tm):
    n_tiles = N_SLOTS // tm
    n_used = (ends[-1] // tm).astype(jnp.int32)
    tile_start = jnp.arange(n_tiles, dtype=jnp.int32) * tm
    tile_expert = jnp.sum((tile_start[:, None] >= ends[None, :]).astype(jnp.int32), axis=1)
    last = jnp.sum((jnp.maximum(n_used - 1, 0) * tm >= ends).astype(jnp.int32))
    tile_expert = jnp.where(jnp.arange(n_tiles) < n_used, tile_expert, last)
    return jnp.minimum(tile_expert, N_EXPERTS - 1).astype(jnp.int32), n_used.reshape(1)


def _route(ids, counts):
    counts = counts[0, :N_EXPERTS].astype(jnp.int32)
    padded = ((counts + MOE_PAD - 1) // MOE_PAD) * MOE_PAD
    ends = jnp.cumsum(padded)
    starts = ends - padded
    pos1 = starts[ids[:, 0]] + ids[:, 2]
    pos2 = starts[ids[:, 1]] + ids[:, 3]
    tok = jnp.arange(M, dtype=jnp.int32)
    slot_token = jnp.zeros((N_SLOTS,), jnp.int32).at[jnp.concatenate([pos1, pos2])].set(jnp.concatenate([tok, tok]))
    return slot_token, _tile_map(ends, MOE_UP_TM), _tile_map(ends, MOE_DN_TM), pos1, pos2


def _rope_tables():
    half = ROT_DIM // 2
    pos = jnp.concatenate([jnp.tile(jnp.arange(SEQ, dtype=jnp.int32), BATCH),
                           jnp.full((MS,), PAST_LEN, jnp.int32)])
    inv = jnp.power(ROPE_THETA, -jnp.arange(half, dtype=F32) / half)
    ang = pos.astype(F32)[:, None] * inv[None, :]
    cos, sin = jnp.cos(ang), jnp.sin(ang)
    pad = HEAD_DIM - ROT_DIM
    one = jnp.ones((M, pad), F32)
    zero = jnp.zeros((M, pad), F32)
    zh = jnp.zeros((M, half), F32)
    cs = jnp.concatenate([cos, cos, one], axis=1)
    sa = jnp.concatenate([-sin, zh, zero], axis=1)
    sb = jnp.concatenate([zh, sin, zero], axis=1)
    rep = LANES // HEAD_DIM
    return jnp.tile(cs, (1, rep)), jnp.tile(sa, (1, rep)), jnp.tile(sb, (1, rep))


def _pad_lanes(a):
    return jnp.pad(a, ((0, 0), (0, LANES - a.shape[1])))


def kernel(x_prompt, x_sample, state_mlstm_C, state_mlstm_n, state_mlstm_m, cache_swa_k, cache_swa_v, norm_mix_g, w_in, b_igate, b_fgate, mlstm_norm_g, attn_sinks, w_branch_m, w_branch_a, w_out, norm_ffn_g, w_gate_dense, w_up_dense, w_down_dense, w_router, w_gate_moe, w_up_moe, w_down_moe, norm_final_g):
    x = jnp.concatenate([x_prompt.reshape(MP, D_MODEL), x_sample.reshape(MS, D_MODEL)], axis=0)
    cs, sa, sb = _rope_tables()
    state_n = state_mlstm_n.reshape(DEPTH, MS, M_HEADS * M_DQK)
    kbuf = cache_swa_k.reshape(DEPTH, MS, WINDOW, KVW)
    vbuf = cache_swa_v.reshape(DEPTH, MS, WINDOW, KVW)
    outs = {name: [] for name in ("Cp", "np", "mp", "kp", "vp", "ns", "ms")}
    y_final = None
    c_all = k_all = v_all = None

    w_tiles, w_gates = _repack_w_in(w_in)

    for l in range(DEPTH):
        bias = _pad_lanes(jnp.concatenate([b_igate[l], b_fgate[l]])[None, :])
        gn = mlstm_norm_g[l][None, :]
        sinks_b = jnp.broadcast_to(attn_sinks[l][:, None], (N_HEADS, LANES))

        z, kv, gates = _inproj(l, x, norm_mix_g[l][None, :], w_tiles, w_gates, cs, sa, sb)

        hm_p, c_p, n_p, m_p = _mlstm_prompt(z, gates, bias, gn)
        ha_p = _swa_prompt(z, kv, attn_sinks[l])
        kv_p = kv[:MP].reshape(BATCH, SEQ, 2 * KVW)[:, SEQ - WINDOW:].reshape(BATCH, WINDOW, 2, KV_HEADS, HEAD_DIM)
        outs["Cp"].append(c_p)
        outs["np"].append(n_p.reshape(BATCH, M_HEADS, M_DQK))
        outs["mp"].append(m_p[:, :, 0, 0])
        outs["kp"].append(kv_p[:, :, 0])
        outs["vp"].append(kv_p[:, :, 1])

        zs = z[MP:].astype(F32)
        kv_s = kv[MP:]
        qt = zs[:, C_QM:C_KM].T.reshape(M_HEADS, M_DQK, MS)
        kt = zs[:, C_KM:C_VM].T.reshape(M_HEADS, M_DQK, MS)
        m0p = _pad_lanes(state_mlstm_m[l])
        hm_s, c_all, n_s, m_s = _mlstm_decode(l, zs, qt, kt, gates[MP:], bias, m0p, gn, state_mlstm_C, state_n, c_all)
        kn = kv_s[:, :KVW]
        vn = kv_s[:, KVW:]
        ha_s, k_all, v_all = _swa_decode(
            l, zs[:, C_QA:C_KA].reshape(MS, N_HEADS, HEAD_DIM), kn.reshape(MS, KV_HEADS, HEAD_DIM),
            vn.reshape(MS, KV_HEADS, HEAD_DIM), kn, vn, kbuf, vbuf, sinks_b, k_all, v_all)
        outs["ns"].append(n_s.reshape(MS, M_HEADS, M_DQK))
        outs["ms"].append(m_s[:, :M_HEADS])

        hm = jnp.concatenate([hm_p, hm_s.astype(BF16)], axis=0)
        ha = jnp.concatenate([ha_p, ha_s.reshape(MS, N_HEADS * HEAD_DIM).astype(BF16)], axis=0)
        merged = _merge(hm, ha, w_branch_m[l].astype(BF16), w_branch_a[l].astype(BF16), z)
        x = _mm_res(merged, w_out[l].astype(BF16), x, OUT_TM, D_MODEL, "out_proj")

        if l % 2 == 0:
            jd = l // 2
            act = _ffn_up(x, norm_ffn_g[l][None, :], w_gate_dense[jd].astype(BF16), w_up_dense[jd].astype(BF16))
            x = _mm_res(act, w_down_dense[jd].astype(BF16), x, TM, TN, "ffn_down")
        else:
            jm = l // 2
            g_ffn = norm_ffn_g[l][None, :]
            ids, gts, counts = _router(x, g_ffn, _pad_lanes(w_router[jm]))
            slot_token, (te_up, nu_up), (te_dn, nu_dn), pos1, pos2 = _route(ids, counts)
            xs = _slot_gather(slot_token, nu_up, x, g_ffn)
            act, wd_bf16 = _moe_up(te_up, nu_up, xs, w_gate_moe[jm], w_up_moe[jm], w_down_moe[jm])
            y = _moe_down(te_dn, nu_dn, act, wd_bf16)
            if l == DEPTH - 1:
                y_final = _combine(pos1, pos2, x, y, gts, norm_final_g[None, :])
            else:
                x = _combine(pos1, pos2, x, y, gts, None)

    if y_final is None:
        y_final = _final_norm(x, norm_final_g[None, :])
    y_prompt = y_final[:MP].reshape(BATCH, SEQ, D_MODEL)
    y_sample = y_final[MP:].reshape(MS, 1, D_MODEL)
    st = lambda name: jnp.stack(outs[name])
    return (y_prompt, y_sample, st("Cp"), st("np"), st("mp"), st("kp"), st("vp"),
            c_all, st("ns"), st("ms"),
            k_all.reshape(DEPTH, MS, WINDOW, KV_HEADS, HEAD_DIM), v_all.reshape(DEPTH, MS, WINDOW, KV_HEADS, HEAD_DIM))
```

```python
import functools

import jax
import jax.numpy as jnp
import numpy as np
from jax import lax
from jax.experimental import pallas as pl
from jax.experimental.pallas import tpu as pltpu

F32 = jnp.float32
BF16 = jnp.bfloat16

D_MODEL = 2048
BATCH = 4
SEQ = 2048
DEPTH = 2
DEC_BATCH = 128
PAST_LEN = 8192
M_HEADS = 4
M_DQK = 256
M_DV = 512
N_HEADS = 32
KV_HEADS = 4
HEAD_DIM = 64
GROUP = N_HEADS // KV_HEADS
ROT_DIM = HEAD_DIM // 4
ROPE_THETA = 500000.0
WINDOW = 128
ATT_BLOCK = 128
D_FF = 5632
N_EXPERTS = 8
TOP_K = 2
E_FF = 7168
EPS = 1e-6

MP = BATCH * SEQ
MS = DEC_BATCH
M = MP + MS

C_QM = 0
C_KM = C_QM + M_HEADS * M_DQK
C_VM = C_KM + M_HEADS * M_DQK
C_OM = C_VM + M_HEADS * M_DV
C_QA = C_OM + M_HEADS * M_DV
C_KA = C_QA + N_HEADS * HEAD_DIM
C_VA = C_KA + KV_HEADS * HEAD_DIM
C_GM = C_VA + KV_HEADS * HEAD_DIM
C_GA = C_GM + D_MODEL
N_Z = C_GA + D_MODEL
R_GATES = 2 * M_HEADS * M_DQK + 2 * M_HEADS * M_DV
R_QA = R_GATES + 2 * M_HEADS

LANES = 128
BF16_SUBLANES = 16
TM = 1040
TN = 512
OUT_TM = 640
ML = 256
TB = 8
MOE_PAD = 512
MOE_UP_TM = MOE_PAD
MOE_DN_TM = MOE_PAD
MOE_TF = 512
MOE_TN = 1024
COMBINE_ROWS = TM // 2
N_SLOTS = -(-(TOP_K * M + N_EXPERTS * (MOE_PAD - 1)) // MOE_PAD) * MOE_PAD
VMEM_LIMIT = 56 * 1024 * 1024


def _cparams(sem, vmem=VMEM_LIMIT):
    return pltpu.CompilerParams(dimension_semantics=sem, vmem_limit_bytes=vmem)


NT_DIMS = (((1,), (1,)), ((), ()))
LOG2_E = 1.4426950408889634


def _rms(x, g):
    ms = jnp.mean(x * x, axis=-1, keepdims=True)
    return (x * lax.rsqrt(ms + EPS)) * g


def _sigmoid(x):
    return 1.0 / (1.0 + jnp.exp(-x))


def _log_sigmoid(x):
    return jnp.minimum(x, 0.0) - jnp.log(1.0 + jnp.exp(-jnp.abs(x)))


IN_TN = 1024
KV_COLS = C_GM - C_KA
J_QA0 = C_QA // IN_TN
J_KV = C_KA // IN_TN
N_ZP = -(-N_Z // IN_TN) * IN_TN
assert C_QA % IN_TN == 0 and C_KA % IN_TN == 0 and KV_COLS <= IN_TN


def _rope(acc, cs, sa, sb):
    n = acc.shape[1]
    return acc * cs + pltpu.roll(acc, n - ROT_DIM // 2, 1) * sa + pltpu.roll(acc, ROT_DIM // 2, 1) * sb


def _inproj_kernel(x_ref, g_ref, w_ref, wgate_ref, cs_ref, sa_ref, sb_ref, z_ref, kv_ref, gt_ref, h_scr):
    j = pl.program_id(1)

    @pl.when(j == 0)
    def _():
        h = _rms(x_ref[...], g_ref[...]).astype(BF16)
        h_scr[...] = h
        gt_ref[...] = lax.dot_general(h, wgate_ref[...], NT_DIMS, preferred_element_type=F32)

    acc = lax.dot_general(h_scr[...], w_ref[...], NT_DIMS, preferred_element_type=F32)
    z_ref[...] = acc.astype(BF16)
    reps = IN_TN // LANES
    is_q = jnp.logical_and(j >= J_QA0, j < J_KV)
    is_kv = j == J_KV

    @pl.when(is_q)
    def _():
        cs = jnp.tile(cs_ref[...], (1, reps))
        sa = jnp.tile(sa_ref[...], (1, reps))
        sb = jnp.tile(sb_ref[...], (1, reps))
        z_ref[...] = _rope(acc, cs, sa, sb).astype(BF16)

    @pl.when(is_kv)
    def _():
        is_k = lax.broadcasted_iota(jnp.int32, acc.shape, 1) < (C_VA - C_KA)
        cs = jnp.where(is_k, jnp.tile(cs_ref[...], (1, reps)), 1.0)
        sa = jnp.where(is_k, jnp.tile(sa_ref[...], (1, reps)), 0.0)
        sb = jnp.where(is_k, jnp.tile(sb_ref[...], (1, reps)), 0.0)
        r = _rope(acc, cs, sa, sb)
        kv_ref[...] = r[:, :KV_COLS]
        z_ref[...] = r.astype(BF16)


N_IN_TILES = N_ZP // IN_TN
N_LEAD_TILES = R_GATES // IN_TN
TAIL_ROWS = N_Z - (N_IN_TILES - 1 - N_LEAD_TILES) * IN_TN - R_GATES
GATE_COLS = R_QA - R_GATES
assert R_GATES % IN_TN == 0 and 0 < TAIL_ROWS <= IN_TN and TAIL_ROWS % 8 == 0 and R_QA % 8 == 0


def _repack_kernel(wt_hbm, w3_ref, wg_ref, buf, gbuf, sem, gsem):
    layer = pl.program_id(0)
    t = pl.program_id(1)
    step = layer * N_IN_TILES + t
    slot = lax.rem(step, 2)

    def src_row(tile):
        return jnp.where(tile < N_LEAD_TILES, tile * IN_TN, R_QA + (tile - N_LEAD_TILES) * IN_TN)

    def full_copy(lyr, tile, into):
        return pltpu.make_async_copy(wt_hbm.at[lyr, pl.ds(pl.multiple_of(src_row(tile), 8), IN_TN), :],
                                     buf.at[into], sem.at[into])

    def tail_copy(lyr, into):
        start = R_QA + (N_IN_TILES - 1 - N_LEAD_TILES) * IN_TN
        return pltpu.make_async_copy(wt_hbm.at[lyr, pl.ds(start, TAIL_ROWS), :],
                                     buf.at[into, pl.ds(0, TAIL_ROWS), :], sem.at[into])

    def fetch(lyr, tile, into):
        @pl.when(tile < N_IN_TILES - 1)
        def _():
            full_copy(lyr, tile, into).start()

        @pl.when(tile == N_IN_TILES - 1)
        def _():
            tail_copy(lyr, into).start()

    @pl.when(step == 0)
    def _():
        fetch(layer, t, slot)

    @pl.when(step + 1 < DEPTH * N_IN_TILES)
    def _():
        nxt = step + 1
        fetch(nxt // N_IN_TILES, lax.rem(nxt, N_IN_TILES), 1 - slot)

    @pl.when(t == 0)
    def _():
        gates = pltpu.make_async_copy(wt_hbm.at[layer, pl.ds(R_GATES, GATE_COLS), :], gbuf, gsem)
        gates.start()
        gates.wait()
        wg_ref[...] = jnp.concatenate([gbuf[...], jnp.zeros((LANES - GATE_COLS, D_MODEL), F32)], axis=0).astype(BF16)

    @pl.when(t < N_IN_TILES - 1)
    def _():
        full_copy(layer, t, slot).wait()
        w3_ref[...] = buf[slot].astype(BF16)

    @pl.when(t == N_IN_TILES - 1)
    def _():
        tail_copy(layer, slot).wait()
        valid = lax.broadcasted_iota(jnp.int32, (IN_TN, D_MODEL), 0) < TAIL_ROWS
        w3_ref[...] = jnp.where(valid, buf[slot], 0.0).astype(BF16)


def _repack_w_in(w_in):
    wt = jnp.swapaxes(w_in, 1, 2)
    return pl.pallas_call(
        _repack_kernel,
        grid=(DEPTH, N_IN_TILES),
        in_specs=[pl.BlockSpec(memory_space=pl.ANY)],
        out_specs=[
            pl.BlockSpec((None, None, IN_TN, D_MODEL), lambda l, t: (l, t, 0, 0)),
            pl.BlockSpec((None, LANES, D_MODEL), lambda l, t: (l, 0, 0)),
        ],
        out_shape=[
            jax.ShapeDtypeStruct((DEPTH, N_IN_TILES, IN_TN, D_MODEL), BF16),
            jax.ShapeDtypeStruct((DEPTH, LANES, D_MODEL), BF16),
        ],
        scratch_shapes=[pltpu.VMEM((2, IN_TN, D_MODEL), F32), pltpu.VMEM((GATE_COLS, D_MODEL), F32),
                        pltpu.SemaphoreType.DMA((2,)), pltpu.SemaphoreType.DMA(())],
        compiler_params=_cparams(("arbitrary", "arbitrary")),
        name="repack_w_in",
    )(wt)


def _inproj(layer, x, g, w_tiles, w_gates, cs, sa, sb):
    return pl.pallas_call(
        _inproj_kernel,
        grid=(M // TM, N_ZP // IN_TN),
        in_specs=[
            pl.BlockSpec((TM, D_MODEL), lambda i, j: (i, 0)),
            pl.BlockSpec((1, D_MODEL), lambda i, j: (0, 0)),
            pl.BlockSpec((None, None, IN_TN, D_MODEL), lambda i, j: (layer, j, 0, 0)),
            pl.BlockSpec((None, LANES, D_MODEL), lambda i, j: (layer, 0, 0)),
            pl.BlockSpec((TM, LANES), lambda i, j: (i, 0)),
            pl.BlockSpec((TM, LANES), lambda i, j: (i, 0)),
            pl.BlockSpec((TM, LANES), lambda i, j: (i, 0)),
        ],
        out_specs=[
            pl.BlockSpec((TM, IN_TN), lambda i, j: (i, j)),
            pl.BlockSpec((TM, KV_COLS), lambda i, j: (i, 0)),
            pl.BlockSpec((TM, LANES), lambda i, j: (i, 0)),
        ],
        out_shape=[
            jax.ShapeDtypeStruct((M, N_ZP), BF16),
            jax.ShapeDtypeStruct((M, KV_COLS), F32),
            jax.ShapeDtypeStruct((M, LANES), F32),
        ],
        scratch_shapes=[pltpu.VMEM((TM, D_MODEL), BF16)],
        compiler_params=_cparams(("parallel", "arbitrary")),
        name="inproj",
    )(x, g, w_tiles, w_gates, cs, sa, sb)


NC = SEQ // ML
MLSTM_HPS = 4


def _mlstm_prompt_kernel(q_ref, k_ref, v_ref, o_ref, gt_ref, bias_ref, gn_ref,
                         hm_ref, c_ref, n_ref, m_ref, ct_scr, n_scr, m_scr):
    hp = pl.program_id(1)
    c = pl.program_id(2)

    @pl.when(c == 0)
    def _():
        ct_scr[...] = jnp.zeros_like(ct_scr)
        n_scr[...] = jnp.zeros_like(n_scr)
        m_scr[...] = jnp.zeros_like(m_scr)

    gates_t = (gt_ref[...] + bias_ref[...]).T
    sub = lax.broadcasted_iota(jnp.int32, gates_t.shape, 0)
    src = lax.broadcasted_iota(jnp.int32, (ML, ML), 0)
    tgt = lax.broadcasted_iota(jnp.int32, (ML, ML), 1)
    causal = src <= tgt
    causal_f = causal.astype(F32)
    first_row = lax.broadcasted_iota(jnp.int32, (8, ML), 0) == 0
    qscale = M_DQK ** -0.5
    nt = (((1,), (1,)), ((), ()))
    finals = []
    for hh in range(MLSTM_HPS):
        h = hp * MLSTM_HPS + hh
        qcols = slice(hh * M_DQK, (hh + 1) * M_DQK)
        vcols = slice(hh * M_DV, (hh + 1) * M_DV)
        i_row = jnp.sum(jnp.where(sub == h, gates_t, 0.0), axis=0, keepdims=True)
        f_row = jnp.sum(jnp.where(sub == h + M_HEADS, gates_t, 0.0), axis=0, keepdims=True)
        lf8 = jnp.where(first_row, _log_sigmoid(f_row), 0.0)
        b_row = jnp.dot(lf8, causal_f, preferred_element_type=F32,
                        precision=lax.Precision.HIGHEST)[0:1, :]
        c_row = i_row - b_row
        c_col = jnp.where(sub == 0, c_row, 0.0).T[:, 0:1]

        m_prev = m_scr[hh]
        a_row = b_row + m_prev
        dmat = jnp.where(causal, b_row + c_col, -jnp.inf)
        m_row = jnp.maximum(a_row, jnp.max(dmat, axis=0, keepdims=True))
        w_intra = jnp.exp(dmat - m_row)
        w_inter = jnp.exp(a_row - m_row)

        q = q_ref[:, qcols]
        k = k_ref[:, qcols]
        v_t = v_ref[:, vcols].astype(F32).T.astype(BF16)
        ct = ct_scr[hh]
        n_prev = n_scr[hh]
        s_t = lax.dot_general(k, q, nt, preferred_element_type=F32) * qscale * w_intra
        inter = lax.dot_general(ct.astype(BF16), q, nt, preferred_element_type=F32) * qscale
        num = w_inter * inter + jnp.dot(v_t, s_t.astype(BF16), preferred_element_type=F32)
        n8 = jnp.broadcast_to(n_prev, (8, M_DQK)).astype(BF16)
        qn = lax.dot_general(n8, q, nt, preferred_element_type=F32)[0:1, :] * qscale
        den = w_inter * qn + jnp.sum(s_t, axis=0, keepdims=True)
        hd = num / jnp.maximum(jnp.abs(den), jnp.exp(-m_row))
        ms = jnp.mean(hd * hd, axis=0, keepdims=True)
        y_t = (hd * lax.rsqrt(ms + EPS)) * jnp.tile(gn_ref[hh], (1, ML // LANES))
        hm_ref[:, vcols] = (y_t.T * _sigmoid(o_ref[:, vcols].astype(F32))).astype(BF16)

        m_new = m_row[:, ML - 1:ML]
        b_last = b_row[:, ML - 1:ML]
        w_state = jnp.exp(c_col + (b_last - m_new))
        decay = jnp.exp(b_last + m_prev - m_new)
        kw = k.astype(F32) * w_state
        ct_new = decay * ct + jnp.dot(v_t, kw.astype(BF16), preferred_element_type=F32)
        n_new = decay * n_prev + jnp.sum(kw, axis=0, keepdims=True)
        ct_scr[hh] = ct_new
        n_scr[hh] = n_new
        m_scr[hh] = m_new
        finals.append((ct_new, n_new, m_new))

    @pl.when(c == NC - 1)
    def _():
        for hh, (ct_new, n_new, m_new) in enumerate(finals):
            c_ref[hh] = ct_new.T
            n_ref[hh] = n_new
            m_ref[hh] = jnp.broadcast_to(m_new, (1, LANES))


def _mlstm_prompt(z, gates, bias, gn):
    hps = MLSTM_HPS
    qb, vb = hps * M_DQK, hps * M_DV
    gn_cols = jnp.broadcast_to(gn.reshape(M_HEADS, M_DV, 1), (M_HEADS, M_DV, LANES))
    return pl.pallas_call(
        _mlstm_prompt_kernel,
        grid=(BATCH, M_HEADS // hps, NC),
        in_specs=[
            pl.BlockSpec((ML, qb), lambda b, h, c: (b * NC + c, C_QM // qb + h)),
            pl.BlockSpec((ML, qb), lambda b, h, c: (b * NC + c, C_KM // qb + h)),
            pl.BlockSpec((ML, vb), lambda b, h, c: (b * NC + c, C_VM // vb + h)),
            pl.BlockSpec((ML, vb), lambda b, h, c: (b * NC + c, C_OM // vb + h)),
            pl.BlockSpec((ML, LANES), lambda b, h, c: (b * NC + c, 0)),
            pl.BlockSpec((1, LANES), lambda b, h, c: (0, 0)),
            pl.BlockSpec((hps, M_DV, LANES), lambda b, h, c: (h, 0, 0)),
        ],
        out_specs=[
            pl.BlockSpec((ML, vb), lambda b, h, c: (b * NC + c, h)),
            pl.BlockSpec((None, hps, M_DQK, M_DV), lambda b, h, c: (b, h, 0, 0)),
            pl.BlockSpec((None, hps, 1, M_DQK), lambda b, h, c: (b, h, 0, 0)),
            pl.BlockSpec((None, hps, 1, LANES), lambda b, h, c: (b, h, 0, 0)),
        ],
        out_shape=[
            jax.ShapeDtypeStruct((MP, M_HEADS * M_DV), BF16),
            jax.ShapeDtypeStruct((BATCH, M_HEADS, M_DQK, M_DV), F32),
            jax.ShapeDtypeStruct((BATCH, M_HEADS, 1, M_DQK), F32),
            jax.ShapeDtypeStruct((BATCH, M_HEADS, 1, LANES), F32),
        ],
        scratch_shapes=[pltpu.VMEM((hps, M_DV, M_DQK), F32), pltpu.VMEM((hps, 1, M_DQK), F32),
                        pltpu.VMEM((hps, 1, 1), F32)],
        compiler_params=_cparams(("parallel", "parallel", "arbitrary")),
        name="mlstm_prompt",
    )(z, z, z, z, gates, bias, gn_cols)


def _mlstm_decode_kernel(q_ref, k_ref, v_ref, o_ref, qt_ref, kt_ref, gt_ref, bias_ref, m0_ref, gn_ref,
                         c0_ref, n0_ref, *rest, layer):
    if layer == 0:
        hm_ref, c_ref, n_ref, m_ref = rest
    else:
        _, hm_ref, c_ref, n_ref, m_ref = rest
    i = pl.program_id(0)
    h = pl.program_id(1)
    gates = gt_ref[...] + bias_ref[...]
    lane = lax.broadcasted_iota(jnp.int32, gates.shape, 1)
    log_f = pltpu.roll(_log_sigmoid(gates), LANES - M_HEADS, 1)
    a = log_f + m0_ref[...]
    m_t = jnp.maximum(a, gates)
    w_intra_all = jnp.exp(gates - m_t)
    w_inter_all = jnp.exp(a - m_t)
    floor_all = jnp.exp(-m_t)

    @pl.when(h == 0)
    def _():
        m_ref[...] = m_t

    def pick(arr):
        return jnp.sum(jnp.where(lane == h, arr, 0.0), axis=1, keepdims=True)

    wi = pick(w_intra_all)
    we = pick(w_inter_all)
    fl = pick(floor_all)
    qscale = M_DQK ** -0.5
    q = q_ref[...] * qscale
    k = k_ref[...]
    v = v_ref[...]
    n0 = n0_ref[...]
    s = jnp.sum(q * k, axis=1, keepdims=True) * wi
    den = we * jnp.sum(q * n0, axis=1, keepdims=True) + s
    dd = jnp.maximum(jnp.abs(den), fl)
    n_ref[...] = we * n0 + wi * k

    shift = lax.rem(LANES - i * TB, LANES)
    qt = pltpu.roll(qt_ref[...], shift, 1) * qscale
    kt = pltpu.roll(kt_ref[...], shift, 1)
    gn = gn_ref[...]
    sig_o = _sigmoid(o_ref[...])
    for j in range(TB):
        qc = qt[:, j:j + 1]
        kc = kt[:, j:j + 1]
        c0 = c0_ref[j]
        vj = v[j:j + 1, :]
        qc0 = jnp.sum(qc * c0, axis=0, keepdims=True)
        hrow = (we[j:j + 1, :] * qc0 + s[j:j + 1, :] * vj) / dd[j:j + 1, :]
        hm_ref[j:j + 1, :] = _rms(hrow, gn) * sig_o[j:j + 1, :]
        c_new = we[j:j + 1, :] * c0 + (wi[j:j + 1, :] * kc) * vj
        if layer == 0:
            c_ref[0, j] = c_new
            for d in range(1, DEPTH):
                c_ref[d, j] = jnp.zeros_like(c_new)
        else:
            c_ref[j] = c_new


def _mlstm_decode(layer, zs, qt, kt, gates_s, bias, m0p, gn, state_c, state_n, c_all):
    qb, vb = M_DQK, M_DV
    if layer == 0:
        c_spec = pl.BlockSpec((DEPTH, TB, None, M_DQK, M_DV), lambda i, h: (0, i, h, 0, 0))
        extra_in, extra_specs, aliases = (), [], {}
    else:
        c_spec = pl.BlockSpec((None, TB, None, M_DQK, M_DV), lambda i, h: (layer, i, h, 0, 0))
        extra_in, extra_specs, aliases = (c_all,), [pl.BlockSpec(memory_space=pl.ANY)], {12: 1}
    return pl.pallas_call(
        functools.partial(_mlstm_decode_kernel, layer=layer),
        grid=(MS // TB, M_HEADS),
        input_output_aliases=aliases,
        in_specs=[
            pl.BlockSpec((TB, qb), lambda i, h: (i, C_QM // qb + h)),
            pl.BlockSpec((TB, qb), lambda i, h: (i, C_KM // qb + h)),
            pl.BlockSpec((TB, vb), lambda i, h: (i, C_VM // vb + h)),
            pl.BlockSpec((TB, vb), lambda i, h: (i, C_OM // vb + h)),
            pl.BlockSpec((None, M_DQK, MS), lambda i, h: (h, 0, 0)),
            pl.BlockSpec((None, M_DQK, MS), lambda i, h: (h, 0, 0)),
            pl.BlockSpec((TB, LANES), lambda i, h: (i, 0)),
            pl.BlockSpec((1, LANES), lambda i, h: (0, 0)),
            pl.BlockSpec((TB, LANES), lambda i, h: (i, 0)),
            pl.BlockSpec((1, vb), lambda i, h: (0, h)),
            pl.BlockSpec((None, TB, None, M_DQK, M_DV), lambda i, h: (layer, i, h, 0, 0)),
            pl.BlockSpec((None, TB, M_DQK), lambda i, h: (layer, i, h)),
        ] + extra_specs,
        out_specs=[
            pl.BlockSpec((TB, vb), lambda i, h: (i, h)),
            c_spec,
            pl.BlockSpec((TB, M_DQK), lambda i, h: (i, h)),
            pl.BlockSpec((TB, LANES), lambda i, h: (i, 0)),
        ],
        out_shape=[
            jax.ShapeDtypeStruct((MS, M_HEADS * M_DV), F32),
            jax.ShapeDtypeStruct((DEPTH, MS, M_HEADS, M_DQK, M_DV), F32),
            jax.ShapeDtypeStruct((MS, M_HEADS * M_DQK), F32),
            jax.ShapeDtypeStruct((MS, LANES), F32),
        ],
        compiler_params=_cparams(("parallel", "arbitrary")),
        name="mlstm_decode",
    )(zs, zs, zs, zs, qt, kt, gates_s, bias, m0p, gn, state_c, state_n, *extra_in)


NB = SEQ // ATT_BLOCK
KVW = KV_HEADS * HEAD_DIM


def _swa_prompt_kernel(sink_ref, q_ref, kc_ref, kp_ref, vc_ref, vp_ref, o_ref):
    nb = pl.program_id(1)
    cols = GROUP * ATT_BLOCK
    sidx = lax.broadcasted_iota(jnp.int32, (2 * ATT_BLOCK, cols), 0)
    t = lax.broadcasted_iota(jnp.int32, (2 * ATT_BLOCK, cols), 1) & (ATT_BLOCK - 1)
    rel = t + ATT_BLOCK - sidx
    visible = jnp.logical_and(jnp.logical_and(rel >= 0, rel <= WINDOW),
                              jnp.logical_or(sidx >= ATT_BLOCK, nb > 0))
    bias = jnp.where(visible, 0.0, -jnp.inf)
    q = q_ref[...] * (HEAD_DIM ** -0.5)
    kk = (jnp.concatenate([kp_ref[...], kc_ref[...]], axis=0) * LOG2_E).astype(BF16)
    vv_t = jnp.concatenate([vp_ref[...], vc_ref[...]], axis=0).T.astype(BF16)
    for g in range(KV_HEADS):
        qg = jnp.concatenate(
            [q[:, (g * GROUP + hh) * HEAD_DIM:(g * GROUP + hh + 1) * HEAD_DIM] for hh in range(GROUP)], axis=0)
        kg = kk[:, g * HEAD_DIM:(g + 1) * HEAD_DIM]
        vg_t = vv_t[g * HEAD_DIM:(g + 1) * HEAD_DIM, :]
        sink = jnp.concatenate(
            [jnp.full((1, ATT_BLOCK), sink_ref[g * GROUP + hh] * LOG2_E, F32) for hh in range(GROUP)], axis=1)
        s = lax.dot_general(kg, qg, (((1,), (1,)), ((), ())), preferred_element_type=F32) + bias
        mx = jnp.maximum(jnp.max(s, axis=0, keepdims=True), sink)
        p = jnp.exp2(s - mx)
        denom = jnp.sum(p, axis=0, keepdims=True) + jnp.exp2(sink - mx)
        o_t = jnp.dot(vg_t, p.astype(BF16), preferred_element_type=F32) / denom
        og = jnp.concatenate([o_t[:, hh * ATT_BLOCK:(hh + 1) * ATT_BLOCK].T for hh in range(GROUP)], axis=1)
        o_ref[:, g * GROUP * HEAD_DIM:(g + 1) * GROUP * HEAD_DIM] = og.astype(BF16)


def _swa_prompt(z, kv, sinks):
    qw = N_HEADS * HEAD_DIM
    return pl.pallas_call(
        _swa_prompt_kernel,
        grid=(BATCH, NB),
        in_specs=[
            pl.BlockSpec(memory_space=pltpu.SMEM),
            pl.BlockSpec((ATT_BLOCK, qw), lambda b, n: (b * NB + n, C_QA // qw)),
            pl.BlockSpec((ATT_BLOCK, KVW), lambda b, n: (b * NB + n, 0)),
            pl.BlockSpec((ATT_BLOCK, KVW), lambda b, n: (b * NB + jnp.maximum(n - 1, 0), 0)),
            pl.BlockSpec((ATT_BLOCK, KVW), lambda b, n: (b * NB + n, 1)),
            pl.BlockSpec((ATT_BLOCK, KVW), lambda b, n: (b * NB + jnp.maximum(n - 1, 0), 1)),
        ],
        out_specs=pl.BlockSpec((ATT_BLOCK, qw), lambda b, n: (b * NB + n, 0)),
        out_shape=jax.ShapeDtypeStruct((MP, qw), BF16),
        compiler_params=_cparams(("parallel", "arbitrary")),
        name="swa_prompt",
    )(sinks, z, kv, kv, kv, kv)


def _swa_decode_kernel(q_ref, kn_ref, vn_ref, knf_ref, vnf_ref, kb_ref, vb_ref, sink_ref, *rest, layer):
    if layer == 0:
        o_ref, kc_ref, vc_ref = rest
    else:
        _, _, o_ref, kc_ref, vc_ref = rest

    def put(ref, j, rows, val):
        if layer == 0:
            ref[0, j, rows, :] = val
        else:
            ref[j, rows, :] = val

    for j in range(TB):
        put(kc_ref, j, slice(0, WINDOW - 1), kb_ref[j, 1:WINDOW, :])
        put(kc_ref, j, slice(WINDOW - 1, WINDOW), knf_ref[j:j + 1, :])
        put(vc_ref, j, slice(0, WINDOW - 1), vb_ref[j, 1:WINDOW, :])
        put(vc_ref, j, slice(WINDOW - 1, WINDOW), vnf_ref[j:j + 1, :])
    if layer == 0:
        for d in range(1, DEPTH):
            kc_ref[d] = jnp.zeros(kc_ref.shape[1:], F32)
            vc_ref[d] = jnp.zeros(vc_ref.shape[1:], F32)

    rows, cols = TB * GROUP, TB * WINDOW
    own = (lax.broadcasted_iota(jnp.int32, (rows, cols), 0) // GROUP
           == lax.broadcasted_iota(jnp.int32, (rows, cols), 1) // WINDOW)
    bias = jnp.where(own, 0.0, -jnp.inf)
    scale = HEAD_DIM ** -0.5
    kstack = kb_ref[...].reshape(cols, KVW).astype(BF16)
    vstack = vb_ref[...].reshape(cols, KVW).astype(BF16)
    for g in range(KV_HEADS):
        qg = q_ref[:, g * GROUP:(g + 1) * GROUP, :].reshape(rows, HEAD_DIM) * scale
        kn = jnp.broadcast_to(kn_ref[:, g:g + 1, :], (TB, GROUP, HEAD_DIM)).reshape(rows, HEAD_DIM)
        vn = jnp.broadcast_to(vn_ref[:, g:g + 1, :], (TB, GROUP, HEAD_DIM)).reshape(rows, HEAD_DIM)
        sink = jnp.tile(sink_ref[g * GROUP:(g + 1) * GROUP, 0:1], (TB, 1))
        s_c = lax.dot_general(qg.astype(BF16), kstack[:, g * HEAD_DIM:(g + 1) * HEAD_DIM],
                              (((1,), (1,)), ((), ())), preferred_element_type=F32) + bias
        s_n = jnp.sum(qg * kn, axis=1, keepdims=True)
        mx = jnp.maximum(jnp.maximum(jnp.max(s_c, axis=1, keepdims=True), s_n), sink)
        p_c = jnp.exp(s_c - mx)
        p_n = jnp.exp(s_n - mx)
        denom = jnp.sum(p_c, axis=1, keepdims=True) + p_n + jnp.exp(sink - mx)
        o = jnp.dot(p_c.astype(BF16), vstack[:, g * HEAD_DIM:(g + 1) * HEAD_DIM], preferred_element_type=F32)
        o = (o + p_n * vn) / denom
        o_ref[:, g * GROUP:(g + 1) * GROUP, :] = o.reshape(TB, GROUP, HEAD_DIM)


def _swa_decode(layer, q3, kn3, vn3, knf, vnf, kbuf, vbuf, sinks_b, k_all, v_all):
    if layer == 0:
        cache_spec = pl.BlockSpec((DEPTH, TB, WINDOW, KVW), lambda i: (0, i, 0, 0))
        extra_in, extra_specs, aliases = (), [], {}
    else:
        cache_spec = pl.BlockSpec((None, TB, WINDOW, KVW), lambda i: (layer, i, 0, 0))
        extra_in = (k_all, v_all)
        extra_specs = [pl.BlockSpec(memory_space=pl.ANY), pl.BlockSpec(memory_space=pl.ANY)]
        aliases = {8: 1, 9: 2}
    return pl.pallas_call(
        functools.partial(_swa_decode_kernel, layer=layer),
        grid=(MS // TB,),
        input_output_aliases=aliases,
        in_specs=[
            pl.BlockSpec((TB, N_HEADS, HEAD_DIM), lambda i: (i, 0, 0)),
            pl.BlockSpec((TB, KV_HEADS, HEAD_DIM), lambda i: (i, 0, 0)),
            pl.BlockSpec((TB, KV_HEADS, HEAD_DIM), lambda i: (i, 0, 0)),
            pl.BlockSpec((TB, KVW), lambda i: (i, 0)),
            pl.BlockSpec((TB, KVW), lambda i: (i, 0)),
            pl.BlockSpec((None, TB, WINDOW, KVW), lambda i: (layer, i, 0, 0)),
            pl.BlockSpec((None, TB, WINDOW, KVW), lambda i: (layer, i, 0, 0)),
            pl.BlockSpec((N_HEADS, LANES), lambda i: (0, 0)),
        ] + extra_specs,
        out_specs=[
            pl.BlockSpec((TB, N_HEADS, HEAD_DIM), lambda i: (i, 0, 0)),
            cache_spec,
            cache_spec,
        ],
        out_shape=[
            jax.ShapeDtypeStruct((MS, N_HEADS, HEAD_DIM), F32),
            jax.ShapeDtypeStruct((DEPTH, MS, WINDOW, KVW), F32),
            jax.ShapeDtypeStruct((DEPTH, MS, WINDOW, KVW), F32),
        ],
        compiler_params=_cparams(("parallel",)),
        name="swa_decode",
    )(q3, kn3, vn3, knf, vnf, kbuf, vbuf, sinks_b, *extra_in)


def _merge_kernel(hm_ref, ha_ref, wbm_ref, wba_ref, gm_ref, ga_ref, o_ref):
    a = jnp.dot(hm_ref[...], wbm_ref[...], preferred_element_type=F32)
    b = jnp.dot(ha_ref[...], wba_ref[...], preferred_element_type=F32)
    o_ref[...] = (_sigmoid(gm_ref[...].astype(F32)) * a + _sigmoid(ga_ref[...].astype(F32)) * b).astype(BF16)


def _merge(hm, ha, wbm, wba, z):
    return pl.pallas_call(
        _merge_kernel,
        grid=(M // TM, D_MODEL // TN),
        in_specs=[
            pl.BlockSpec((TM, D_MODEL), lambda i, j: (i, 0)),
            pl.BlockSpec((TM, D_MODEL), lambda i, j: (i, 0)),
            pl.BlockSpec((D_MODEL, TN), lambda i, j: (0, j)),
            pl.BlockSpec((D_MODEL, TN), lambda i, j: (0, j)),
            pl.BlockSpec((TM, TN), lambda i, j: (i, C_GM // TN + j)),
            pl.BlockSpec((TM, TN), lambda i, j: (i, C_GA // TN + j)),
        ],
        out_specs=pl.BlockSpec((TM, TN), lambda i, j: (i, j)),
        out_shape=jax.ShapeDtypeStruct((M, D_MODEL), BF16),
        compiler_params=_cparams(("parallel", "arbitrary")),
        name="merge",
    )(hm, ha, wbm, wba, z, z)


def _mm_res_kernel(a_ref, w_ref, r_ref, o_ref):
    o_ref[...] = r_ref[...] + jnp.dot(a_ref[...], w_ref[...], preferred_element_type=F32)


def _mm_res(a, w, res, tm, tn, name):
    kdim = a.shape[1]
    n = w.shape[1]
    return pl.pallas_call(
        _mm_res_kernel,
        grid=(M // tm, n // tn),
        in_specs=[
            pl.BlockSpec((tm, kdim), lambda i, j: (i, 0)),
            pl.BlockSpec((kdim, tn), lambda i, j: (0, j)),
            pl.BlockSpec((tm, tn), lambda i, j: (i, j)),
        ],
        out_specs=pl.BlockSpec((tm, tn), lambda i, j: (i, j)),
        out_shape=jax.ShapeDtypeStruct((M, n), F32),
        compiler_params=_cparams(("parallel", "arbitrary")),
        name=name,
    )(a, w, res)


def _ffn_up_kernel(x_ref, g_ref, wg_ref, wu_ref, act_ref, h_scr):
    @pl.when(pl.program_id(1) == 0)
    def _():
        h_scr[...] = _rms(x_ref[...], g_ref[...]).astype(BF16)

    h = h_scr[...]
    a = jnp.dot(h, wg_ref[...], preferred_element_type=F32)
    u = jnp.dot(h, wu_ref[...], preferred_element_type=F32)
    act_ref[...] = ((a * _sigmoid(a)) * u).astype(BF16)


def _ffn_up(x, g, wg, wu):
    return pl.pallas_call(
        _ffn_up_kernel,
        grid=(M // TM, D_FF // TN),
        in_specs=[
            pl.BlockSpec((TM, D_MODEL), lambda i, j: (i, 0)),
            pl.BlockSpec((1, D_MODEL), lambda i, j: (0, 0)),
            pl.BlockSpec((D_MODEL, TN), lambda i, j: (0, j)),
            pl.BlockSpec((D_MODEL, TN), lambda i, j: (0, j)),
        ],
        out_specs=pl.BlockSpec((TM, TN), lambda i, j: (i, j)),
        out_shape=jax.ShapeDtypeStruct((M, D_FF), BF16),
        scratch_shapes=[pltpu.VMEM((TM, D_MODEL), BF16)],
        compiler_params=_cparams(("parallel", "arbitrary")),
        name="ffn_up",
    )(x, g, wg, wu)


def _router_kernel(x_ref, g_ref, wr_ref, ids_ref, gates_ref, cnt_ref, run_scr):
    @pl.when(pl.program_id(0) == 0)
    def _():
        run_scr[...] = jnp.zeros_like(run_scr)

    h = _rms(x_ref[...], g_ref[...])
    logits = jnp.dot(h, wr_ref[...], preferred_element_type=F32, precision=lax.Precision.HIGHEST)
    lane = lax.broadcasted_iota(jnp.int32, logits.shape, 1)
    lg = jnp.where(lane < N_EXPERTS, logits, -jnp.inf)
    v1 = jnp.max(lg, axis=1, keepdims=True)
    i1 = jnp.min(jnp.where(lg == v1, lane, LANES), axis=1, keepdims=True)
    lg2 = jnp.where(lane == i1, -jnp.inf, lg)
    v2 = jnp.max(lg2, axis=1, keepdims=True)
    i2 = jnp.min(jnp.where(lg2 == v2, lane, LANES), axis=1, keepdims=True)
    e2 = jnp.exp(v2 - v1)
    g1 = 1.0 / (1.0 + e2)
    g2 = e2 / (1.0 + e2)
    gates_ref[...] = jnp.where(lane == 0, g1, jnp.where(lane == 1, g2, 0.0))

    picks = jnp.where(jnp.logical_or(lane == i1, lane == i2), 1.0, 0.0)
    tm = picks.shape[0]
    earlier = lax.broadcasted_iota(jnp.int32, (tm, tm), 1) < lax.broadcasted_iota(jnp.int32, (tm, tm), 0)
    before = jnp.dot(jnp.where(earlier, 1.0, 0.0).astype(BF16), picks.astype(BF16),
                     preferred_element_type=F32) + run_scr[...]
    r1 = jnp.sum(jnp.where(lane == i1, before, 0.0), axis=1, keepdims=True).astype(jnp.int32)
    r2 = jnp.sum(jnp.where(lane == i2, before, 0.0), axis=1, keepdims=True).astype(jnp.int32)
    ids_ref[...] = jnp.where(lane == 0, i1, jnp.where(lane == 1, i2, jnp.where(lane == 2, r1, jnp.where(lane == 3, r2, 0))))
    total = run_scr[...] + jnp.sum(picks, axis=0, keepdims=True)
    run_scr[...] = total
    cnt_ref[...] = total


def _router(x, g, wr_pad):
    return pl.pallas_call(
        _router_kernel,
        grid=(M // TM,),
        in_specs=[
            pl.BlockSpec((TM, D_MODEL), lambda i: (i, 0)),
            pl.BlockSpec((1, D_MODEL), lambda i: (0, 0)),
            pl.BlockSpec((D_MODEL, LANES), lambda i: (0, 0)),
        ],
        out_specs=[
            pl.BlockSpec((TM, LANES), lambda i: (i, 0)),
            pl.BlockSpec((TM, LANES), lambda i: (i, 0)),
            pl.BlockSpec((1, LANES), lambda i: (0, 0)),
        ],
        out_shape=[
            jax.ShapeDtypeStruct((M, LANES), jnp.int32),
            jax.ShapeDtypeStruct((M, LANES), F32),
            jax.ShapeDtypeStruct((1, LANES), F32),
        ],
        scratch_shapes=[pltpu.VMEM((1, LANES), F32)],
        compiler_params=_cparams(("arbitrary",)),
        name="router",
    )(x, g, wr_pad)


def _row_copy(src_hbm, row, buf, r, sem):
    return pltpu.make_async_copy(src_hbm.at[pl.ds(row, 1), :], buf.at[pl.ds(r, 1), :], sem)


def _slot_gather_kernel(tok_ref, nu_ref, x_hbm, g_ref, xs_ref, buf, sem):
    m = pl.program_id(0)
    n_used = nu_ref[0]
    slot = lax.rem(m, 2)

    def request(tile, into):
        def issue(grp, carry):
            for u in range(8):
                r = grp * 8 + u
                _row_copy(x_hbm, tok_ref[tile * MOE_UP_TM + r], buf.at[into], r, sem.at[into]).start(priority=u % 2)
            return carry
        lax.fori_loop(0, MOE_UP_TM // 8, issue, 0)

    @pl.when(jnp.logical_and(m == 0, n_used > 0))
    def _():
        request(0, 0)

    @pl.when(m + 1 < n_used)
    def _():
        request(m + 1, 1 - slot)

    @pl.when(m < n_used)
    def _():
        def wait(r, carry):
            _row_copy(x_hbm, 0, buf.at[slot], r, sem.at[slot]).wait()
            return carry
        lax.fori_loop(0, MOE_UP_TM, wait, 0, unroll=8)
        xs_ref[...] = _rms(buf[slot], g_ref[...]).astype(BF16)

    @pl.when(m >= n_used)
    def _():
        xs_ref[...] = jnp.zeros_like(xs_ref)


def _slot_gather(slot_token, n_used, x, g):
    return pl.pallas_call(
        _slot_gather_kernel,
        grid_spec=pltpu.PrefetchScalarGridSpec(
            num_scalar_prefetch=2,
            grid=(N_SLOTS // MOE_UP_TM,),
            in_specs=[pl.BlockSpec(memory_space=pl.ANY),
                      pl.BlockSpec((1, D_MODEL), lambda m, tok, nu: (0, 0))],
            out_specs=pl.BlockSpec((MOE_UP_TM, D_MODEL), lambda m, tok, nu: (m, 0)),
            scratch_shapes=[pltpu.VMEM((2, MOE_UP_TM, D_MODEL), F32), pltpu.SemaphoreType.DMA((2,))],
        ),
        out_shape=jax.ShapeDtypeStruct((N_SLOTS, D_MODEL), BF16),
        compiler_params=_cparams(("arbitrary",)),
        name="slot_gather",
    )(slot_token, n_used, x, g)


def _combine_kernel(p1_ref, p2_ref, x_ref, y_hbm, gate_ref, g_ref, o_ref, buf1, buf2, sem, *, final_norm):
    base = pl.program_id(0) * COMBINE_ROWS

    def issue(r, carry):
        _row_copy(y_hbm, p1_ref[base + r], buf1, r, sem).start()
        _row_copy(y_hbm, p2_ref[base + r], buf2, r, sem).start()
        return carry
    lax.fori_loop(0, COMBINE_ROWS, issue, 0, unroll=8)

    def wait(r, carry):
        _row_copy(y_hbm, 0, buf1, r, sem).wait()
        _row_copy(y_hbm, 0, buf2, r, sem).wait()
        return carry
    lax.fori_loop(0, COMBINE_ROWS, wait, 0, unroll=8)
    gates = gate_ref[...]
    out = x_ref[...] + (gates[:, 0:1] * buf1[...] + gates[:, 1:2] * buf2[...])
    o_ref[...] = _rms(out, g_ref[...]) if final_norm else out


def _combine(pos1, pos2, x, y, gates, g_final):
    final_norm = g_final is not None
    g = g_final if final_norm else jnp.ones((1, D_MODEL), F32)
    return pl.pallas_call(
        functools.partial(_combine_kernel, final_norm=final_norm),
        grid_spec=pltpu.PrefetchScalarGridSpec(
            num_scalar_prefetch=2,
            grid=(M // COMBINE_ROWS,),
            in_specs=[pl.BlockSpec((COMBINE_ROWS, D_MODEL), lambda i, p1, p2: (i, 0)),
                      pl.BlockSpec(memory_space=pl.ANY),
                      pl.BlockSpec((COMBINE_ROWS, LANES), lambda i, p1, p2: (i, 0)),
                      pl.BlockSpec((1, D_MODEL), lambda i, p1, p2: (0, 0))],
            out_specs=pl.BlockSpec((COMBINE_ROWS, D_MODEL), lambda i, p1, p2: (i, 0)),
            scratch_shapes=[pltpu.VMEM((COMBINE_ROWS, D_MODEL), F32), pltpu.VMEM((COMBINE_ROWS, D_MODEL), F32),
                            pltpu.SemaphoreType.DMA(())],
        ),
        out_shape=jax.ShapeDtypeStruct((M, D_MODEL), F32),
        compiler_params=_cparams(("arbitrary",)),
        name="combine",
    )(pos1, pos2, x, y, gates, g)


def _moe_up_kernel(te_ref, nu_ref, x_ref, wg_ref, wu_ref, wdf_ref, act_ref, wdb_ref, wg_scr, wu_scr):
    m = pl.program_id(1)
    for cidx in range(D_MODEL // MOE_TN):
        wdb_ref[cidx] = wdf_ref[:, cidx * MOE_TN:(cidx + 1) * MOE_TN].astype(BF16)

    @pl.when(m < nu_ref[0])
    def _():
        prev = te_ref[jnp.maximum(m - 1, 0)]

        @pl.when(jnp.logical_or(m == 0, te_ref[m] != prev))
        def _():
            wg_scr[...] = wg_ref[...].astype(BF16)
            wu_scr[...] = wu_ref[...].astype(BF16)

        x = x_ref[...]
        a = jnp.dot(x, wg_scr[...], preferred_element_type=F32)
        u = jnp.dot(x, wu_scr[...], preferred_element_type=F32)
        act_ref[...] = ((a * _sigmoid(a)) * u).astype(BF16)

    @pl.when(m >= nu_ref[0])
    def _():
        act_ref[...] = jnp.zeros_like(act_ref)


def _side_chunk(total_rows, n_steps):
    chunk = BF16_SUBLANES
    while total_rows % chunk or total_rows // chunk > n_steps:
        chunk += BF16_SUBLANES
    return chunk


def _moe_up(tile_expert, n_used, xs, wg, wu, wd):
    n_j, n_m = E_FF // MOE_TF, N_SLOTS // MOE_UP_TM
    wd_rows = N_EXPERTS * E_FF
    chunk = _side_chunk(wd_rows, n_j * n_m)
    last_chunk = wd_rows // chunk - 1
    side = lambda j, m, te, nu: (jnp.minimum(j * n_m + m, last_chunk), 0)
    return pl.pallas_call(
        _moe_up_kernel,
        grid_spec=pltpu.PrefetchScalarGridSpec(
            num_scalar_prefetch=2,
            grid=(n_j, n_m),
            in_specs=[
                pl.BlockSpec((MOE_UP_TM, D_MODEL), lambda j, m, te, nu: (m, 0)),
                pl.BlockSpec((None, D_MODEL, MOE_TF), lambda j, m, te, nu: (te[m], 0, j)),
                pl.BlockSpec((None, D_MODEL, MOE_TF), lambda j, m, te, nu: (te[m], 0, j)),
                pl.BlockSpec((chunk, D_MODEL), side),
            ],
            out_specs=[pl.BlockSpec((None, MOE_UP_TM, MOE_TF), lambda j, m, te, nu: (j, m, 0)),
                       pl.BlockSpec((D_MODEL // MOE_TN, chunk, MOE_TN),
                                    lambda j, m, te, nu: (0, jnp.minimum(j * n_m + m, last_chunk), 0))],
            scratch_shapes=[pltpu.VMEM((D_MODEL, MOE_TF), BF16), pltpu.VMEM((D_MODEL, MOE_TF), BF16)],
        ),
        out_shape=[jax.ShapeDtypeStruct((n_j, N_SLOTS, MOE_TF), BF16),
                   jax.ShapeDtypeStruct((D_MODEL // MOE_TN, wd_rows, MOE_TN), BF16)],
        compiler_params=_cparams(("arbitrary", "arbitrary")),
        name="moe_up",
    )(tile_expert, n_used, xs, wg, wu, wd.reshape(wd_rows, D_MODEL))


def _moe_down_kernel(te_ref, nu_ref, a_ref, wd_ref, y_ref):
    m = pl.program_id(1)

    @pl.when(m < nu_ref[0])
    def _():
        acc = jnp.dot(a_ref[0], wd_ref[0:MOE_TF, :], preferred_element_type=F32)
        for f in range(1, E_FF // MOE_TF):
            acc = acc + jnp.dot(a_ref[f], wd_ref[f * MOE_TF:(f + 1) * MOE_TF, :], preferred_element_type=F32)
        y_ref[...] = acc

    @pl.when(m >= nu_ref[0])
    def _():
        y_ref[...] = jnp.zeros_like(y_ref)


def _moe_down(tile_expert, n_used, act, wd_tiles):
    return pl.pallas_call(
        _moe_down_kernel,
        grid_spec=pltpu.PrefetchScalarGridSpec(
            num_scalar_prefetch=2,
            grid=(D_MODEL // MOE_TN, N_SLOTS // MOE_DN_TM),
            in_specs=[
                pl.BlockSpec((E_FF // MOE_TF, MOE_DN_TM, MOE_TF), lambda j, m, te, nu: (0, m, 0)),
                pl.BlockSpec((None, E_FF, MOE_TN), lambda j, m, te, nu: (j, te[m], 0)),
            ],
            out_specs=pl.BlockSpec((MOE_DN_TM, MOE_TN), lambda j, m, te, nu: (m, j)),
        ),
        out_shape=jax.ShapeDtypeStruct((N_SLOTS, D_MODEL), F32),
        compiler_params=_cparams(("arbitrary", "arbitrary")),
        name="moe_down",
    )(tile_expert, n_used, act, wd_tiles)


def _final_norm_kernel(x_ref, g_ref, o_ref):
    o_ref[...] = _rms(x_ref[...], g_ref[...])


def _final_norm(x, g):
    spec = pl.BlockSpec((COMBINE_ROWS, D_MODEL), lambda i: (i, 0))
    return pl.pallas_call(
        _final_norm_kernel,
        grid=(M // COMBINE_ROWS,),
        in_specs=[spec, pl.BlockSpec((1, D_MODEL), lambda i: (0, 0))],
        out_specs=spec,
        out_shape=jax.ShapeDtypeStruct((M, D_MODEL), F32),
        compiler_params=_cparams(("parallel",)),
        name="final_norm",
    )(x, g)


def _tile_map(ends, tm):
    n_tiles = N_SLOTS // tm
    n_used = (ends[-1] // tm).astype(jnp.int32)
    tile_start = jnp.arange(n_tiles, dtype=jnp.int32) * tm
    tile_expert = jnp.sum((tile_start[:, None] >= ends[None, :]).astype(jnp.int32), axis=1)
    last = jnp.sum((jnp.maximum(n_used - 1, 0) * tm >= ends).astype(jnp.int32))
    tile_expert = jnp.where(jnp.arange(n_tiles) < n_used, tile_expert, last)
    return jnp.minimum(tile_expert, N_EXPERTS - 1).astype(jnp.int32), n_used.reshape(1)


def _route(ids, counts):
    counts = counts[0, :N_EXPERTS].astype(jnp.int32)
    padded = ((counts + MOE_PAD - 1) // MOE_PAD) * MOE_PAD
    ends = jnp.cumsum(padded)
    starts = ends - padded
    pos1 = starts[ids[:, 0]] + ids[:, 2]
    pos2 = starts[ids[:, 1]] + ids[:, 3]
    tok = jnp.arange(M, dtype=jnp.int32)
    slot_token = jnp.zeros((N_SLOTS,), jnp.int32).at[jnp.concatenate([pos1, pos2])].set(jnp.concatenate([tok, tok]))
    return slot_token, _tile_map(ends, MOE_UP_TM), _tile_map(ends, MOE_DN_TM), pos1, pos2


def _rope_tables():
    half = ROT_DIM // 2
    pos = jnp.concatenate([jnp.tile(jnp.arange(SEQ, dtype=jnp.int32), BATCH),
                           jnp.full((MS,), PAST_LEN, jnp.int32)])
    inv = jnp.power(ROPE_THETA, -jnp.arange(half, dtype=F32) / half)
    ang = pos.astype(F32)[:, None] * inv[None, :]
    cos, sin = jnp.cos(ang), jnp.sin(ang)
    pad = HEAD_DIM - ROT_DIM
    one = jnp.ones((M, pad), F32)
    zero = jnp.zeros((M, pad), F32)
    zh = jnp.zeros((M, half), F32)
    cs = jnp.concatenate([cos, cos, one], axis=1)
    sa = jnp.concatenate([-sin, zh, zero], axis=1)
    sb = jnp.concatenate([zh, sin, zero], axis=1)
    rep = LANES // HEAD_DIM
    return jnp.tile(cs, (1, rep)), jnp.tile(sa, (1, rep)), jnp.tile(sb, (1, rep))


def _pad_lanes(a):
    return jnp.pad(a, ((0, 0), (0, LANES - a.shape[1])))


def kernel(x_prompt, x_sample, state_mlstm_C, state_mlstm_n, state_mlstm_m, cache_swa_k, cache_swa_v, norm_mix_g, w_in, b_igate, b_fgate, mlstm_norm_g, attn_sinks, w_branch_m, w_branch_a, w_out, norm_ffn_g, w_gate_dense, w_up_dense, w_down_dense, w_router, w_gate_moe, w_up_moe, w_down_moe, norm_final_g):
    x = jnp.concatenate([x_prompt.reshape(MP, D_MODEL), x_sample.reshape(MS, D_MODEL)], axis=0)
    cs, sa, sb = _rope_tables()
    state_n = state_mlstm_n.reshape(DEPTH, MS, M_HEADS * M_DQK)
    kbuf = cache_swa_k.reshape(DEPTH, MS, WINDOW, KVW)
    vbuf = cache_swa_v.reshape(DEPTH, MS, WINDOW, KVW)
    outs = {name: [] for name in ("Cp", "np", "mp", "kp", "vp", "ns", "ms")}
    y_final = None
    c_all = k_all = v_all = None

    w_tiles, w_gates = _repack_w_in(w_in)

    for l in range(DEPTH):
        bias = _pad_lanes(jnp.concatenate([b_igate[l], b_fgate[l]])[None, :])
        gn = mlstm_norm_g[l][None, :]
        sinks_b = jnp.broadcast_to(attn_sinks[l][:, None], (N_HEADS, LANES))

        z, kv, gates = _inproj(l, x, norm_mix_g[l][None, :], w_tiles, w_gates, cs, sa, sb)

        hm_p, c_p, n_p, m_p = _mlstm_prompt(z, gates, bias, gn)
        ha_p = _swa_prompt(z, kv, attn_sinks[l])
        kv_p = jnp.stack([kv[(b + 1) * SEQ - WINDOW:(b + 1) * SEQ] for b in range(BATCH)])
        kv_p = kv_p.reshape(BATCH, WINDOW, 2, KV_HEADS, HEAD_DIM)
        outs["Cp"].append(c_p)
        outs["np"].append(n_p.reshape(BATCH, M_HEADS, M_DQK))
        outs["mp"].append(m_p[:, :, 0, 0])
        outs["kp"].append(kv_p[:, :, 0])
        outs["vp"].append(kv_p[:, :, 1])

        zs = z[MP:].astype(F32)
        kv_s = kv[MP:]
        qt = zs[:, C_QM:C_KM].T.reshape(M_HEADS, M_DQK, MS)
        kt = zs[:, C_KM:C_VM].T.reshape(M_HEADS, M_DQK, MS)
        m0p = _pad_lanes(state_mlstm_m[l])
        hm_s, c_all, n_s, m_s = _mlstm_decode(l, zs, qt, kt, gates[MP:], bias, m0p, gn, state_mlstm_C, state_n, c_all)
        kn = kv_s[:, :KVW]
        vn = kv_s[:, KVW:]
        ha_s, k_all, v_all = _swa_decode(
            l, zs[:, C_QA:C_KA].reshape(MS, N_HEADS, HEAD_DIM), kn.reshape(MS, KV_HEADS, HEAD_DIM),
            vn.reshape(MS, KV_HEADS, HEAD_DIM), kn, vn, kbuf, vbuf, sinks_b, k_all, v_all)
        outs["ns"].append(n_s.reshape(MS, M_HEADS, M_DQK))
        outs["ms"].append(m_s[:, :M_HEADS])

        hm = jnp.concatenate([hm_p, hm_s.astype(BF16)], axis=0)
        ha = jnp.concatenate([ha_p, ha_s.reshape(MS, N_HEADS * HEAD_DIM).astype(BF16)], axis=0)
        merged = _merge(hm, ha, w_branch_m[l].astype(BF16), w_branch_a[l].astype(BF16), z)
        x = _mm_res(merged, w_out[l].astype(BF16), x, OUT_TM, D_MODEL, "out_proj")

        if l % 2 == 0:
            jd = l // 2
            act = _ffn_up(x, norm_ffn_g[l][None, :], w_gate_dense[jd].astype(BF16), w_up_dense[jd].astype(BF16))
            x = _mm_res(act, w_down_dense[jd].astype(BF16), x, TM, TN, "ffn_down")
        else:
            jm = l // 2
            g_ffn = norm_ffn_g[l][None, :]
            ids, gts, counts = _router(x, g_ffn, _pad_lanes(w_router[jm]))
            slot_token, (te_up, nu_up), (te_dn, nu_dn), pos1, pos2 = _route(ids, counts)
            xs = _slot_gather(slot_token, nu_up, x, g_ffn)
            act, wd_bf16 = _moe_up(te_up, nu_up, xs, w_gate_moe[jm], w_up_moe[jm], w_down_moe[jm])
            y = _moe_down(te_dn, nu_dn, act, wd_bf16)
            if l == DEPTH - 1:
                y_final = _combine(pos1, pos2, x, y, gts, norm_final_g[None, :])
            else:
                x = _combine(pos1, pos2, x, y, gts, None)

    if y_final is None:
        y_final = _final_norm(x, norm_final_g[None, :])
    y_prompt = y_final[:MP].reshape(BATCH, SEQ, D_MODEL)
    y_sample = y_final[MP:].reshape(MS, 1, D_MODEL)
    st = lambda name: jnp.stack(outs[name])
    return (y_prompt, y_sample, st("Cp"), st("np"), st("mp"), st("kp"), st("vp"),
            c_all, st("ns"), st("ms"),
            k_all.reshape(DEPTH, MS, WINDOW, KV_HEADS, HEAD_DIM), v_all.reshape(DEPTH, MS, WINDOW, KV_HEADS, HEAD_DIM))
```

```python
import functools

import jax
import jax.numpy as jnp
import numpy as np
from jax import lax
from jax.experimental import pallas as pl
from jax.experimental.pallas import tpu as pltpu

F32 = jnp.float32
BF16 = jnp.bfloat16

D_MODEL = 2048
BATCH = 4
SEQ = 2048
DEPTH = 2
DEC_BATCH = 128
PAST_LEN = 8192
M_HEADS = 4
M_DQK = 256
M_DV = 512
N_HEADS = 32
KV_HEADS = 4
HEAD_DIM = 64
GROUP = N_HEADS // KV_HEADS
ROT_DIM = HEAD_DIM // 4
ROPE_THETA = 500000.0
WINDOW = 128
ATT_BLOCK = 128
D_FF = 5632
N_EXPERTS = 8
TOP_K = 2
E_FF = 7168
EPS = 1e-6

MP = BATCH * SEQ
MS = DEC_BATCH
M = MP + MS

C_QM = 0
C_KM = C_QM + M_HEADS * M_DQK
C_VM = C_KM + M_HEADS * M_DQK
C_OM = C_VM + M_HEADS * M_DV
C_QA = C_OM + M_HEADS * M_DV
C_KA = C_QA + N_HEADS * HEAD_DIM
C_VA = C_KA + KV_HEADS * HEAD_DIM
C_GM = C_VA + KV_HEADS * HEAD_DIM
C_GA = C_GM + D_MODEL
N_Z = C_GA + D_MODEL
R_GATES = 2 * M_HEADS * M_DQK + 2 * M_HEADS * M_DV
R_QA = R_GATES + 2 * M_HEADS

LANES = 128
BF16_SUBLANES = 16
TM = 1040
TN = 512
OUT_TM = 640
ML = 256
TB = 8
MOE_PAD = 512
MOE_UP_TM = MOE_PAD
MOE_DN_TM = MOE_PAD
MOE_TF = 512
MOE_TN = 1024
COMBINE_ROWS = TM // 2
COMBINE_PROMPT_ROWS = 512
N_SLOTS = -(-(TOP_K * M + N_EXPERTS * (MOE_PAD - 1)) // MOE_PAD) * MOE_PAD
VMEM_LIMIT = 56 * 1024 * 1024


def _cparams(sem, vmem=VMEM_LIMIT):
    return pltpu.CompilerParams(dimension_semantics=sem, vmem_limit_bytes=vmem)


NT_DIMS = (((1,), (1,)), ((), ()))
LOG2_E = 1.4426950408889634


def _rms(x, g):
    ms = jnp.mean(x * x, axis=-1, keepdims=True)
    return (x * lax.rsqrt(ms + EPS)) * g


def _sigmoid(x):
    return 1.0 / (1.0 + jnp.exp(-x))


def _log_sigmoid(x):
    return jnp.minimum(x, 0.0) - jnp.log(1.0 + jnp.exp(-jnp.abs(x)))


IN_TN = 1024
KV_COLS = C_GM - C_KA
J_QA0 = C_QA // IN_TN
J_KV = C_KA // IN_TN
N_ZP = -(-N_Z // IN_TN) * IN_TN
assert C_QA % IN_TN == 0 and C_KA % IN_TN == 0 and KV_COLS <= IN_TN


def _rope(acc, cs, sa, sb):
    n = acc.shape[1]
    return acc * cs + pltpu.roll(acc, n - ROT_DIM // 2, 1) * sa + pltpu.roll(acc, ROT_DIM // 2, 1) * sb


def _inproj_kernel(x_ref, g_ref, w_ref, wgate_ref, cs_ref, sa_ref, sb_ref, z_ref, kv_ref, gt_ref, h_scr):
    j = pl.program_id(1)

    @pl.when(j == 0)
    def _():
        h = _rms(x_ref[...], g_ref[...]).astype(BF16)
        h_scr[...] = h
        gt_ref[...] = lax.dot_general(h, wgate_ref[...], NT_DIMS, preferred_element_type=F32)

    acc = lax.dot_general(h_scr[...], w_ref[...], NT_DIMS, preferred_element_type=F32)
    z_ref[...] = acc.astype(BF16)
    reps = IN_TN // LANES
    is_q = jnp.logical_and(j >= J_QA0, j < J_KV)
    is_kv = j == J_KV

    @pl.when(is_q)
    def _():
        cs = jnp.tile(cs_ref[...], (1, reps))
        sa = jnp.tile(sa_ref[...], (1, reps))
        sb = jnp.tile(sb_ref[...], (1, reps))
        z_ref[...] = _rope(acc, cs, sa, sb).astype(BF16)

    @pl.when(is_kv)
    def _():
        is_k = lax.broadcasted_iota(jnp.int32, acc.shape, 1) < (C_VA - C_KA)
        cs = jnp.where(is_k, jnp.tile(cs_ref[...], (1, reps)), 1.0)
        sa = jnp.where(is_k, jnp.tile(sa_ref[...], (1, reps)), 0.0)
        sb = jnp.where(is_k, jnp.tile(sb_ref[...], (1, reps)), 0.0)
        r = _rope(acc, cs, sa, sb)
        kv_ref[...] = r[:, :KV_COLS]
        z_ref[...] = r.astype(BF16)


N_IN_TILES = N_ZP // IN_TN
N_LEAD_TILES = R_GATES // IN_TN
TAIL_ROWS = N_Z - (N_IN_TILES - 1 - N_LEAD_TILES) * IN_TN - R_GATES
GATE_COLS = R_QA - R_GATES
assert R_GATES % IN_TN == 0 and 0 < TAIL_ROWS <= IN_TN and TAIL_ROWS % 8 == 0 and R_QA % 8 == 0


def _repack_kernel(wt_hbm, w3_ref, wg_ref, buf, gbuf, sem, gsem):
    layer = pl.program_id(0)
    t = pl.program_id(1)
    step = layer * N_IN_TILES + t
    slot = lax.rem(step, 2)

    def src_row(tile):
        return jnp.where(tile < N_LEAD_TILES, tile * IN_TN, R_QA + (tile - N_LEAD_TILES) * IN_TN)

    def full_copy(lyr, tile, into):
        return pltpu.make_async_copy(wt_hbm.at[lyr, pl.ds(pl.multiple_of(src_row(tile), 8), IN_TN), :],
                                     buf.at[into], sem.at[into])

    def tail_copy(lyr, into):
        start = R_QA + (N_IN_TILES - 1 - N_LEAD_TILES) * IN_TN
        return pltpu.make_async_copy(wt_hbm.at[lyr, pl.ds(start, TAIL_ROWS), :],
                                     buf.at[into, pl.ds(0, TAIL_ROWS), :], sem.at[into])

    def fetch(lyr, tile, into):
        @pl.when(tile < N_IN_TILES - 1)
        def _():
            full_copy(lyr, tile, into).start()

        @pl.when(tile == N_IN_TILES - 1)
        def _():
            tail_copy(lyr, into).start()

    @pl.when(step == 0)
    def _():
        fetch(layer, t, slot)

    @pl.when(step + 1 < DEPTH * N_IN_TILES)
    def _():
        nxt = step + 1
        fetch(nxt // N_IN_TILES, lax.rem(nxt, N_IN_TILES), 1 - slot)

    @pl.when(t == 0)
    def _():
        gates = pltpu.make_async_copy(wt_hbm.at[layer, pl.ds(R_GATES, GATE_COLS), :], gbuf, gsem)
        gates.start()
        gates.wait()
        wg_ref[...] = jnp.concatenate([gbuf[...], jnp.zeros((LANES - GATE_COLS, D_MODEL), F32)], axis=0).astype(BF16)

    @pl.when(t < N_IN_TILES - 1)
    def _():
        full_copy(layer, t, slot).wait()
        w3_ref[...] = buf[slot].astype(BF16)

    @pl.when(t == N_IN_TILES - 1)
    def _():
        tail_copy(layer, slot).wait()
        valid = lax.broadcasted_iota(jnp.int32, (IN_TN, D_MODEL), 0) < TAIL_ROWS
        w3_ref[...] = jnp.where(valid, buf[slot], 0.0).astype(BF16)


def _repack_w_in(w_in):
    wt = jnp.swapaxes(w_in, 1, 2)
    return pl.pallas_call(
        _repack_kernel,
        grid=(DEPTH, N_IN_TILES),
        in_specs=[pl.BlockSpec(memory_space=pl.ANY)],
        out_specs=[
            pl.BlockSpec((None, None, IN_TN, D_MODEL), lambda l, t: (l, t, 0, 0)),
            pl.BlockSpec((None, LANES, D_MODEL), lambda l, t: (l, 0, 0)),
        ],
        out_shape=[
            jax.ShapeDtypeStruct((DEPTH, N_IN_TILES, IN_TN, D_MODEL), BF16),
            jax.ShapeDtypeStruct((DEPTH, LANES, D_MODEL), BF16),
        ],
        scratch_shapes=[pltpu.VMEM((2, IN_TN, D_MODEL), F32), pltpu.VMEM((GATE_COLS, D_MODEL), F32),
                        pltpu.SemaphoreType.DMA((2,)), pltpu.SemaphoreType.DMA(())],
        compiler_params=_cparams(("arbitrary", "arbitrary")),
        name="repack_w_in",
    )(wt)


def _inproj(layer, x, g, w_tiles, w_gates, cs, sa, sb):
    return pl.pallas_call(
        _inproj_kernel,
        grid=(M // TM, N_ZP // IN_TN),
        in_specs=[
            pl.BlockSpec((TM, D_MODEL), lambda i, j: (i, 0)),
            pl.BlockSpec((1, D_MODEL), lambda i, j: (0, 0)),
            pl.BlockSpec((None, None, IN_TN, D_MODEL), lambda i, j: (layer, j, 0, 0)),
            pl.BlockSpec((None, LANES, D_MODEL), lambda i, j: (layer, 0, 0)),
            pl.BlockSpec((TM, LANES), lambda i, j: (i, 0)),
            pl.BlockSpec((TM, LANES), lambda i, j: (i, 0)),
            pl.BlockSpec((TM, LANES), lambda i, j: (i, 0)),
        ],
        out_specs=[
            pl.BlockSpec((TM, IN_TN), lambda i, j: (i, j)),
            pl.BlockSpec((TM, KV_COLS), lambda i, j: (i, 0)),
            pl.BlockSpec((TM, LANES), lambda i, j: (i, 0)),
        ],
        out_shape=[
            jax.ShapeDtypeStruct((M, N_ZP), BF16),
            jax.ShapeDtypeStruct((M, KV_COLS), F32),
            jax.ShapeDtypeStruct((M, LANES), F32),
        ],
        scratch_shapes=[pltpu.VMEM((TM, D_MODEL), BF16)],
        compiler_params=_cparams(("parallel", "arbitrary")),
        name="inproj",
    )(x, g, w_tiles, w_gates, cs, sa, sb)


NC = SEQ // ML
MLSTM_HPS = 4


def _mlstm_prompt_kernel(q_ref, k_ref, v_ref, o_ref, gt_ref, bias_ref, gn_ref,
                         hm_ref, c_ref, n_ref, m_ref, ct_scr, n_scr, m_scr):
    hp = pl.program_id(1)
    c = pl.program_id(2)

    @pl.when(c == 0)
    def _():
        ct_scr[...] = jnp.zeros_like(ct_scr)
        n_scr[...] = jnp.zeros_like(n_scr)
        m_scr[...] = jnp.zeros_like(m_scr)

    gates_t = (gt_ref[...] + bias_ref[...]).T
    sub = lax.broadcasted_iota(jnp.int32, gates_t.shape, 0)
    src = lax.broadcasted_iota(jnp.int32, (ML, ML), 0)
    tgt = lax.broadcasted_iota(jnp.int32, (ML, ML), 1)
    causal = src <= tgt
    causal_f = causal.astype(F32)
    first_row = lax.broadcasted_iota(jnp.int32, (8, ML), 0) == 0
    qscale = M_DQK ** -0.5
    nt = (((1,), (1,)), ((), ()))
    finals = []
    for hh in range(MLSTM_HPS):
        h = hp * MLSTM_HPS + hh
        qcols = slice(hh * M_DQK, (hh + 1) * M_DQK)
        vcols = slice(hh * M_DV, (hh + 1) * M_DV)
        i_row = jnp.sum(jnp.where(sub == h, gates_t, 0.0), axis=0, keepdims=True)
        f_row = jnp.sum(jnp.where(sub == h + M_HEADS, gates_t, 0.0), axis=0, keepdims=True)
        lf8 = jnp.where(first_row, _log_sigmoid(f_row), 0.0)
        b_row = jnp.dot(lf8, causal_f, preferred_element_type=F32,
                        precision=lax.Precision.HIGHEST)[0:1, :]
        c_row = i_row - b_row
        c_col = jnp.where(sub == 0, c_row, 0.0).T[:, 0:1]

        m_prev = m_scr[hh]
        a_row = b_row + m_prev
        dmat = jnp.where(causal, b_row + c_col, -jnp.inf)
        m_row = jnp.maximum(a_row, jnp.max(dmat, axis=0, keepdims=True))
        w_intra = jnp.exp(dmat - m_row)
        w_inter = jnp.exp(a_row - m_row)

        q = q_ref[:, qcols]
        k = k_ref[:, qcols]
        v_t = v_ref[:, vcols].astype(F32).T.astype(BF16)
        ct = ct_scr[hh]
        n_prev = n_scr[hh]
        s_t = lax.dot_general(k, q, nt, preferred_element_type=F32) * qscale * w_intra
        inter = lax.dot_general(ct.astype(BF16), q, nt, preferred_element_type=F32) * qscale
        num = w_inter * inter + jnp.dot(v_t, s_t.astype(BF16), preferred_element_type=F32)
        n8 = jnp.broadcast_to(n_prev, (8, M_DQK)).astype(BF16)
        qn = lax.dot_general(n8, q, nt, preferred_element_type=F32)[0:1, :] * qscale
        den = w_inter * qn + jnp.sum(s_t, axis=0, keepdims=True)
        hd = num / jnp.maximum(jnp.abs(den), jnp.exp(-m_row))
        ms = jnp.mean(hd * hd, axis=0, keepdims=True)
        y_t = (hd * lax.rsqrt(ms + EPS)) * jnp.tile(gn_ref[hh], (1, ML // LANES))
        hm_ref[:, vcols] = (y_t.T * _sigmoid(o_ref[:, vcols].astype(F32))).astype(BF16)

        m_new = m_row[:, ML - 1:ML]
        b_last = b_row[:, ML - 1:ML]
        w_state = jnp.exp(c_col + (b_last - m_new))
        decay = jnp.exp(b_last + m_prev - m_new)
        kw = k.astype(F32) * w_state
        ct_new = decay * ct + jnp.dot(v_t, kw.astype(BF16), preferred_element_type=F32)
        n_new = decay * n_prev + jnp.sum(kw, axis=0, keepdims=True)
        ct_scr[hh] = ct_new
        n_scr[hh] = n_new
        m_scr[hh] = m_new
        finals.append((ct_new, n_new, m_new))

    @pl.when(c == NC - 1)
    def _():
        for hh, (ct_new, n_new, m_new) in enumerate(finals):
            c_ref[hh] = ct_new.T
            n_ref[hh] = n_new
            m_ref[hh] = jnp.broadcast_to(m_new, (1, LANES))


def _mlstm_prompt(z, gates, bias, gn):
    hps = MLSTM_HPS
    qb, vb = hps * M_DQK, hps * M_DV
    gn_cols = jnp.broadcast_to(gn.reshape(M_HEADS, M_DV, 1), (M_HEADS, M_DV, LANES))
    return pl.pallas_call(
        _mlstm_prompt_kernel,
        grid=(BATCH, M_HEADS // hps, NC),
        in_specs=[
            pl.BlockSpec((ML, qb), lambda b, h, c: (b * NC + c, C_QM // qb + h)),
            pl.BlockSpec((ML, qb), lambda b, h, c: (b * NC + c, C_KM // qb + h)),
            pl.BlockSpec((ML, vb), lambda b, h, c: (b * NC + c, C_VM // vb + h)),
            pl.BlockSpec((ML, vb), lambda b, h, c: (b * NC + c, C_OM // vb + h)),
            pl.BlockSpec((ML, LANES), lambda b, h, c: (b * NC + c, 0)),
            pl.BlockSpec((1, LANES), lambda b, h, c: (0, 0)),
            pl.BlockSpec((hps, M_DV, LANES), lambda b, h, c: (h, 0, 0)),
        ],
        out_specs=[
            pl.BlockSpec((ML, vb), lambda b, h, c: (b * NC + c, h)),
            pl.BlockSpec((None, hps, M_DQK, M_DV), lambda b, h, c: (b, h, 0, 0)),
            pl.BlockSpec((None, hps, 1, M_DQK), lambda b, h, c: (b, h, 0, 0)),
            pl.BlockSpec((None, hps, 1, LANES), lambda b, h, c: (b, h, 0, 0)),
        ],
        out_shape=[
            jax.ShapeDtypeStruct((MP, M_HEADS * M_DV), BF16),
            jax.ShapeDtypeStruct((BATCH, M_HEADS, M_DQK, M_DV), F32),
            jax.ShapeDtypeStruct((BATCH, M_HEADS, 1, M_DQK), F32),
            jax.ShapeDtypeStruct((BATCH, M_HEADS, 1, LANES), F32),
        ],
        scratch_shapes=[pltpu.VMEM((hps, M_DV, M_DQK), F32), pltpu.VMEM((hps, 1, M_DQK), F32),
                        pltpu.VMEM((hps, 1, 1), F32)],
        compiler_params=_cparams(("parallel", "parallel", "arbitrary")),
        name="mlstm_prompt",
    )(z, z, z, z, gates, bias, gn_cols)


def _mlstm_decode_kernel(q_ref, k_ref, v_ref, o_ref, qt_ref, kt_ref, gt_ref, bias_ref, m0_ref, gn_ref,
                         c0_ref, n0_ref, *rest, layer):
    if layer == 0:
        hm_ref, c_ref, n_ref, m_ref = rest
    else:
        _, hm_ref, c_ref, n_ref, m_ref = rest
    i = pl.program_id(0)
    h = pl.program_id(1)
    gates = gt_ref[...] + bias_ref[...]
    lane = lax.broadcasted_iota(jnp.int32, gates.shape, 1)
    log_f = pltpu.roll(_log_sigmoid(gates), LANES - M_HEADS, 1)
    a = log_f + m0_ref[...]
    m_t = jnp.maximum(a, gates)
    w_intra_all = jnp.exp(gates - m_t)
    w_inter_all = jnp.exp(a - m_t)
    floor_all = jnp.exp(-m_t)

    @pl.when(h == 0)
    def _():
        m_ref[...] = m_t

    def pick(arr):
        return jnp.sum(jnp.where(lane == h, arr, 0.0), axis=1, keepdims=True)

    wi = pick(w_intra_all)
    we = pick(w_inter_all)
    fl = pick(floor_all)
    qscale = M_DQK ** -0.5
    q = q_ref[...] * qscale
    k = k_ref[...]
    v = v_ref[...]
    n0 = n0_ref[...]
    s = jnp.sum(q * k, axis=1, keepdims=True) * wi
    den = we * jnp.sum(q * n0, axis=1, keepdims=True) + s
    dd = jnp.maximum(jnp.abs(den), fl)
    n_ref[...] = we * n0 + wi * k

    shift = lax.rem(LANES - i * TB, LANES)
    qt = pltpu.roll(qt_ref[...], shift, 1) * qscale
    kt = pltpu.roll(kt_ref[...], shift, 1)
    gn = gn_ref[...]
    sig_o = _sigmoid(o_ref[...])
    for j in range(TB):
        qc = qt[:, j:j + 1]
        kc = kt[:, j:j + 1]
        c0 = c0_ref[j]
        vj = v[j:j + 1, :]
        qc0 = jnp.sum(qc * c0, axis=0, keepdims=True)
        hrow = (we[j:j + 1, :] * qc0 + s[j:j + 1, :] * vj) / dd[j:j + 1, :]
        hm_ref[j:j + 1, :] = _rms(hrow, gn) * sig_o[j:j + 1, :]
        c_new = we[j:j + 1, :] * c0 + (wi[j:j + 1, :] * kc) * vj
        if layer == 0:
            c_ref[0, j] = c_new
            for d in range(1, DEPTH):
                c_ref[d, j] = jnp.zeros_like(c_new)
        else:
            c_ref[j] = c_new


def _mlstm_decode(layer, zs, qt, kt, gates_s, bias, m0p, gn, state_c, state_n, c_all):
    qb, vb = M_DQK, M_DV
    if layer == 0:
        c_spec = pl.BlockSpec((DEPTH, TB, None, M_DQK, M_DV), lambda i, h: (0, i, h, 0, 0))
        extra_in, extra_specs, aliases = (), [], {}
    else:
        c_spec = pl.BlockSpec((None, TB, None, M_DQK, M_DV), lambda i, h: (layer, i, h, 0, 0))
        extra_in, extra_specs, aliases = (c_all,), [pl.BlockSpec(memory_space=pl.ANY)], {12: 1}
    return pl.pallas_call(
        functools.partial(_mlstm_decode_kernel, layer=layer),
        grid=(MS // TB, M_HEADS),
        input_output_aliases=aliases,
        in_specs=[
            pl.BlockSpec((TB, qb), lambda i, h: (i, C_QM // qb + h)),
            pl.BlockSpec((TB, qb), lambda i, h: (i, C_KM // qb + h)),
            pl.BlockSpec((TB, vb), lambda i, h: (i, C_VM // vb + h)),
            pl.BlockSpec((TB, vb), lambda i, h: (i, C_OM // vb + h)),
            pl.BlockSpec((None, M_DQK, MS), lambda i, h: (h, 0, 0)),
            pl.BlockSpec((None, M_DQK, MS), lambda i, h: (h, 0, 0)),
            pl.BlockSpec((TB, LANES), lambda i, h: (i, 0)),
            pl.BlockSpec((1, LANES), lambda i, h: (0, 0)),
            pl.BlockSpec((TB, LANES), lambda i, h: (i, 0)),
            pl.BlockSpec((1, vb), lambda i, h: (0, h)),
            pl.BlockSpec((None, TB, None, M_DQK, M_DV), lambda i, h: (layer, i, h, 0, 0)),
            pl.BlockSpec((None, TB, M_DQK), lambda i, h: (layer, i, h)),
        ] + extra_specs,
        out_specs=[
            pl.BlockSpec((TB, vb), lambda i, h: (i, h)),
            c_spec,
            pl.BlockSpec((TB, M_DQK), lambda i, h: (i, h)),
            pl.BlockSpec((TB, LANES), lambda i, h: (i, 0)),
        ],
        out_shape=[
            jax.ShapeDtypeStruct((MS, M_HEADS * M_DV), F32),
            jax.ShapeDtypeStruct((DEPTH, MS, M_HEADS, M_DQK, M_DV), F32),
            jax.ShapeDtypeStruct((MS, M_HEADS * M_DQK), F32),
            jax.ShapeDtypeStruct((MS, LANES), F32),
        ],
        compiler_params=_cparams(("parallel", "arbitrary")),
        name="mlstm_decode",
    )(zs, zs, zs, zs, qt, kt, gates_s, bias, m0p, gn, state_c, state_n, *extra_in)


NB = SEQ // ATT_BLOCK
KVW = KV_HEADS * HEAD_DIM


def _swa_prompt_kernel(sink_ref, q_ref, kc_ref, kp_ref, vc_ref, vp_ref, o_ref):
    nb = pl.program_id(1)
    cols = GROUP * ATT_BLOCK
    sidx = lax.broadcasted_iota(jnp.int32, (2 * ATT_BLOCK, cols), 0)
    t = lax.broadcasted_iota(jnp.int32, (2 * ATT_BLOCK, cols), 1) & (ATT_BLOCK - 1)
    rel = t + ATT_BLOCK - sidx
    visible = jnp.logical_and(jnp.logical_and(rel >= 0, rel <= WINDOW),
                              jnp.logical_or(sidx >= ATT_BLOCK, nb > 0))
    bias = jnp.where(visible, 0.0, -jnp.inf)
    q = q_ref[...] * (HEAD_DIM ** -0.5)
    kk = (jnp.concatenate([kp_ref[...], kc_ref[...]], axis=0) * LOG2_E).astype(BF16)
    vv_t = jnp.concatenate([vp_ref[...], vc_ref[...]], axis=0).T.astype(BF16)
    for g in range(KV_HEADS):
        qg = jnp.concatenate(
            [q[:, (g * GROUP + hh) * HEAD_DIM:(g * GROUP + hh + 1) * HEAD_DIM] for hh in range(GROUP)], axis=0)
        kg = kk[:, g * HEAD_DIM:(g + 1) * HEAD_DIM]
        vg_t = vv_t[g * HEAD_DIM:(g + 1) * HEAD_DIM, :]
        sink = jnp.concatenate(
            [jnp.full((1, ATT_BLOCK), sink_ref[g * GROUP + hh] * LOG2_E, F32) for hh in range(GROUP)], axis=1)
        s = lax.dot_general(kg, qg, (((1,), (1,)), ((), ())), preferred_element_type=F32) + bias
        mx = jnp.maximum(jnp.max(s, axis=0, keepdims=True), sink)
        p = jnp.exp2(s - mx)
        denom = jnp.sum(p, axis=0, keepdims=True) + jnp.exp2(sink - mx)
        o_t = jnp.dot(vg_t, p.astype(BF16), preferred_element_type=F32) / denom
        og = jnp.concatenate([o_t[:, hh * ATT_BLOCK:(hh + 1) * ATT_BLOCK].T for hh in range(GROUP)], axis=1)
        o_ref[:, g * GROUP * HEAD_DIM:(g + 1) * GROUP * HEAD_DIM] = og.astype(BF16)


def _swa_prompt(z, kv, sinks):
    qw = N_HEADS * HEAD_DIM
    return pl.pallas_call(
        _swa_prompt_kernel,
        grid=(BATCH, NB),
        in_specs=[
            pl.BlockSpec(memory_space=pltpu.SMEM),
            pl.BlockSpec((ATT_BLOCK, qw), lambda b, n: (b * NB + n, C_QA // qw)),
            pl.BlockSpec((ATT_BLOCK, KVW), lambda b, n: (b * NB + n, 0)),
            pl.BlockSpec((ATT_BLOCK, KVW), lambda b, n: (b * NB + jnp.maximum(n - 1, 0), 0)),
            pl.BlockSpec((ATT_BLOCK, KVW), lambda b, n: (b * NB + n, 1)),
            pl.BlockSpec((ATT_BLOCK, KVW), lambda b, n: (b * NB + jnp.maximum(n - 1, 0), 1)),
        ],
        out_specs=pl.BlockSpec((ATT_BLOCK, qw), lambda b, n: (b * NB + n, 0)),
        out_shape=jax.ShapeDtypeStruct((MP, qw), BF16),
        compiler_params=_cparams(("parallel", "arbitrary")),
        name="swa_prompt",
    )(sinks, z, kv, kv, kv, kv)


def _swa_decode_kernel(q_ref, kn_ref, vn_ref, knf_ref, vnf_ref, kb_ref, vb_ref, sink_ref, *rest, layer):
    if layer == 0:
        o_ref, kc_ref, vc_ref = rest
    else:
        _, _, o_ref, kc_ref, vc_ref = rest

    def put(ref, j, rows, val):
        if layer == 0:
            ref[0, j, rows, :] = val
        else:
            ref[j, rows, :] = val

    for j in range(TB):
        put(kc_ref, j, slice(0, WINDOW - 1), kb_ref[j, 1:WINDOW, :])
        put(kc_ref, j, slice(WINDOW - 1, WINDOW), knf_ref[j:j + 1, :])
        put(vc_ref, j, slice(0, WINDOW - 1), vb_ref[j, 1:WINDOW, :])
        put(vc_ref, j, slice(WINDOW - 1, WINDOW), vnf_ref[j:j + 1, :])
    if layer == 0:
        for d in range(1, DEPTH):
            kc_ref[d] = jnp.zeros(kc_ref.shape[1:], F32)
            vc_ref[d] = jnp.zeros(vc_ref.shape[1:], F32)

    rows, cols = TB * GROUP, TB * WINDOW
    own = (lax.broadcasted_iota(jnp.int32, (rows, cols), 0) // GROUP
           == lax.broadcasted_iota(jnp.int32, (rows, cols), 1) // WINDOW)
    bias = jnp.where(own, 0.0, -jnp.inf)
    scale = HEAD_DIM ** -0.5
    kstack = kb_ref[...].reshape(cols, KVW).astype(BF16)
    vstack = vb_ref[...].reshape(cols, KVW).astype(BF16)
    for g in range(KV_HEADS):
        qg = q_ref[:, g * GROUP:(g + 1) * GROUP, :].reshape(rows, HEAD_DIM) * scale
        kn = jnp.broadcast_to(kn_ref[:, g:g + 1, :], (TB, GROUP, HEAD_DIM)).reshape(rows, HEAD_DIM)
        vn = jnp.broadcast_to(vn_ref[:, g:g + 1, :], (TB, GROUP, HEAD_DIM)).reshape(rows, HEAD_DIM)
        sink = jnp.tile(sink_ref[g * GROUP:(g + 1) * GROUP, 0:1], (TB, 1))
        s_c = lax.dot_general(qg.astype(BF16), kstack[:, g * HEAD_DIM:(g + 1) * HEAD_DIM],
                              (((1,), (1,)), ((), ())), preferred_element_type=F32) + bias
        s_n = jnp.sum(qg * kn, axis=1, keepdims=True)
        mx = jnp.maximum(jnp.maximum(jnp.max(s_c, axis=1, keepdims=True), s_n), sink)
        p_c = jnp.exp(s_c - mx)
        p_n = jnp.exp(s_n - mx)
        denom = jnp.sum(p_c, axis=1, keepdims=True) + p_n + jnp.exp(sink - mx)
        o = jnp.dot(p_c.astype(BF16), vstack[:, g * HEAD_DIM:(g + 1) * HEAD_DIM], preferred_element_type=F32)
        o = (o + p_n * vn) / denom
        o_ref[:, g * GROUP:(g + 1) * GROUP, :] = o.reshape(TB, GROUP, HEAD_DIM)


def _swa_decode(layer, q3, kn3, vn3, knf, vnf, kbuf, vbuf, sinks_b, k_all, v_all):
    if layer == 0:
        cache_spec = pl.BlockSpec((DEPTH, TB, WINDOW, KVW), lambda i: (0, i, 0, 0))
        extra_in, extra_specs, aliases = (), [], {}
    else:
        cache_spec = pl.BlockSpec((None, TB, WINDOW, KVW), lambda i: (layer, i, 0, 0))
        extra_in = (k_all, v_all)
        extra_specs = [pl.BlockSpec(memory_space=pl.ANY), pl.BlockSpec(memory_space=pl.ANY)]
        aliases = {8: 1, 9: 2}
    return pl.pallas_call(
        functools.partial(_swa_decode_kernel, layer=layer),
        grid=(MS // TB,),
        input_output_aliases=aliases,
        in_specs=[
            pl.BlockSpec((TB, N_HEADS, HEAD_DIM), lambda i: (i, 0, 0)),
            pl.BlockSpec((TB, KV_HEADS, HEAD_DIM), lambda i: (i, 0, 0)),
            pl.BlockSpec((TB, KV_HEADS, HEAD_DIM), lambda i: (i, 0, 0)),
            pl.BlockSpec((TB, KVW), lambda i: (i, 0)),
            pl.BlockSpec((TB, KVW), lambda i: (i, 0)),
            pl.BlockSpec((None, TB, WINDOW, KVW), lambda i: (layer, i, 0, 0)),
            pl.BlockSpec((None, TB, WINDOW, KVW), lambda i: (layer, i, 0, 0)),
            pl.BlockSpec((N_HEADS, LANES), lambda i: (0, 0)),
        ] + extra_specs,
        out_specs=[
            pl.BlockSpec((TB, N_HEADS, HEAD_DIM), lambda i: (i, 0, 0)),
            cache_spec,
            cache_spec,
        ],
        out_shape=[
            jax.ShapeDtypeStruct((MS, N_HEADS, HEAD_DIM), F32),
            jax.ShapeDtypeStruct((DEPTH, MS, WINDOW, KVW), F32),
            jax.ShapeDtypeStruct((DEPTH, MS, WINDOW, KVW), F32),
        ],
        compiler_params=_cparams(("parallel",)),
        name="swa_decode",
    )(q3, kn3, vn3, knf, vnf, kbuf, vbuf, sinks_b, *extra_in)


def _merge_kernel(hm_ref, ha_ref, wbm_ref, wba_ref, gm_ref, ga_ref, o_ref):
    a = jnp.dot(hm_ref[...], wbm_ref[...], preferred_element_type=F32)
    b = jnp.dot(ha_ref[...], wba_ref[...], preferred_element_type=F32)
    o_ref[...] = (_sigmoid(gm_ref[...].astype(F32)) * a + _sigmoid(ga_ref[...].astype(F32)) * b).astype(BF16)


def _merge(hm, ha, wbm, wba, z):
    return pl.pallas_call(
        _merge_kernel,
        grid=(M // TM, D_MODEL // TN),
        in_specs=[
            pl.BlockSpec((TM, D_MODEL), lambda i, j: (i, 0)),
            pl.BlockSpec((TM, D_MODEL), lambda i, j: (i, 0)),
            pl.BlockSpec((D_MODEL, TN), lambda i, j: (0, j)),
            pl.BlockSpec((D_MODEL, TN), lambda i, j: (0, j)),
            pl.BlockSpec((TM, TN), lambda i, j: (i, C_GM // TN + j)),
            pl.BlockSpec((TM, TN), lambda i, j: (i, C_GA // TN + j)),
        ],
        out_specs=pl.BlockSpec((TM, TN), lambda i, j: (i, j)),
        out_shape=jax.ShapeDtypeStruct((M, D_MODEL), BF16),
        compiler_params=_cparams(("parallel", "arbitrary")),
        name="merge",
    )(hm, ha, wbm, wba, z, z)


def _mm_res_kernel(a_ref, w_ref, r_ref, o_ref):
    o_ref[...] = r_ref[...] + jnp.dot(a_ref[...], w_ref[...], preferred_element_type=F32)


def _mm_res(a, w, res, tm, tn, name):
    kdim = a.shape[1]
    n = w.shape[1]
    return pl.pallas_call(
        _mm_res_kernel,
        grid=(M // tm, n // tn),
        in_specs=[
            pl.BlockSpec((tm, kdim), lambda i, j: (i, 0)),
            pl.BlockSpec((kdim, tn), lambda i, j: (0, j)),
            pl.BlockSpec((tm, tn), lambda i, j: (i, j)),
        ],
        out_specs=pl.BlockSpec((tm, tn), lambda i, j: (i, j)),
        out_shape=jax.ShapeDtypeStruct((M, n), F32),
        compiler_params=_cparams(("parallel", "arbitrary")),
        name=name,
    )(a, w, res)


def _ffn_up_kernel(x_ref, g_ref, wg_ref, wu_ref, act_ref, h_scr):
    @pl.when(pl.program_id(1) == 0)
    def _():
        h_scr[...] = _rms(x_ref[...], g_ref[...]).astype(BF16)

    h = h_scr[...]
    a = jnp.dot(h, wg_ref[...], preferred_element_type=F32)
    u = jnp.dot(h, wu_ref[...], preferred_element_type=F32)
    act_ref[...] = ((a * _sigmoid(a)) * u).astype(BF16)


def _ffn_up(x, g, wg, wu):
    return pl.pallas_call(
        _ffn_up_kernel,
        grid=(M // TM, D_FF // TN),
        in_specs=[
            pl.BlockSpec((TM, D_MODEL), lambda i, j: (i, 0)),
            pl.BlockSpec((1, D_MODEL), lambda i, j: (0, 0)),
            pl.BlockSpec((D_MODEL, TN), lambda i, j: (0, j)),
            pl.BlockSpec((D_MODEL, TN), lambda i, j: (0, j)),
        ],
        out_specs=pl.BlockSpec((TM, TN), lambda i, j: (i, j)),
        out_shape=jax.ShapeDtypeStruct((M, D_FF), BF16),
        scratch_shapes=[pltpu.VMEM((TM, D_MODEL), BF16)],
        compiler_params=_cparams(("parallel", "arbitrary")),
        name="ffn_up",
    )(x, g, wg, wu)


def _router_kernel(x_ref, g_ref, wr_ref, ids_ref, gates_ref, cnt_ref, run_scr):
    @pl.when(pl.program_id(0) == 0)
    def _():
        run_scr[...] = jnp.zeros_like(run_scr)

    h = _rms(x_ref[...], g_ref[...])
    logits = jnp.dot(h, wr_ref[...], preferred_element_type=F32, precision=lax.Precision.HIGHEST)
    lane = lax.broadcasted_iota(jnp.int32, logits.shape, 1)
    lg = jnp.where(lane < N_EXPERTS, logits, -jnp.inf)
    v1 = jnp.max(lg, axis=1, keepdims=True)
    i1 = jnp.min(jnp.where(lg == v1, lane, LANES), axis=1, keepdims=True)
    lg2 = jnp.where(lane == i1, -jnp.inf, lg)
    v2 = jnp.max(lg2, axis=1, keepdims=True)
    i2 = jnp.min(jnp.where(lg2 == v2, lane, LANES), axis=1, keepdims=True)
    e2 = jnp.exp(v2 - v1)
    g1 = 1.0 / (1.0 + e2)
    g2 = e2 / (1.0 + e2)
    gates_ref[...] = jnp.where(lane == 0, g1, jnp.where(lane == 1, g2, 0.0))

    picks = jnp.where(jnp.logical_or(lane == i1, lane == i2), 1.0, 0.0)
    tm = picks.shape[0]
    earlier = lax.broadcasted_iota(jnp.int32, (tm, tm), 1) < lax.broadcasted_iota(jnp.int32, (tm, tm), 0)
    before = jnp.dot(jnp.where(earlier, 1.0, 0.0).astype(BF16), picks.astype(BF16),
                     preferred_element_type=F32) + run_scr[...]
    r1 = jnp.sum(jnp.where(lane == i1, before, 0.0), axis=1, keepdims=True).astype(jnp.int32)
    r2 = jnp.sum(jnp.where(lane == i2, before, 0.0), axis=1, keepdims=True).astype(jnp.int32)
    ids_ref[...] = jnp.where(lane == 0, i1, jnp.where(lane == 1, i2, jnp.where(lane == 2, r1, jnp.where(lane == 3, r2, 0))))
    total = run_scr[...] + jnp.sum(picks, axis=0, keepdims=True)
    run_scr[...] = total
    cnt_ref[...] = total


def _router(x, g, wr_pad):
    return pl.pallas_call(
        _router_kernel,
        grid=(M // TM,),
        in_specs=[
            pl.BlockSpec((TM, D_MODEL), lambda i: (i, 0)),
            pl.BlockSpec((1, D_MODEL), lambda i: (0, 0)),
            pl.BlockSpec((D_MODEL, LANES), lambda i: (0, 0)),
        ],
        out_specs=[
            pl.BlockSpec((TM, LANES), lambda i: (i, 0)),
            pl.BlockSpec((TM, LANES), lambda i: (i, 0)),
            pl.BlockSpec((1, LANES), lambda i: (0, 0)),
        ],
        out_shape=[
            jax.ShapeDtypeStruct((M, LANES), jnp.int32),
            jax.ShapeDtypeStruct((M, LANES), F32),
            jax.ShapeDtypeStruct((1, LANES), F32),
        ],
        scratch_shapes=[pltpu.VMEM((1, LANES), F32)],
        compiler_params=_cparams(("arbitrary",)),
        name="router",
    )(x, g, wr_pad)


def _row_copy(src_hbm, row, buf, r, sem):
    return pltpu.make_async_copy(src_hbm.at[pl.ds(row, 1), :], buf.at[pl.ds(r, 1), :], sem)


def _slot_gather_kernel(tok_ref, nu_ref, x_hbm, g_ref, xs_ref, buf, sem):
    m = pl.program_id(0)
    n_used = nu_ref[0]
    slot = lax.rem(m, 2)

    def request(tile, into):
        def issue(grp, carry):
            for u in range(8):
                r = grp * 8 + u
                _row_copy(x_hbm, tok_ref[tile * MOE_UP_TM + r], buf.at[into], r, sem.at[into]).start(priority=u % 2)
            return carry
        lax.fori_loop(0, MOE_UP_TM // 8, issue, 0)

    @pl.when(jnp.logical_and(m == 0, n_used > 0))
    def _():
        request(0, 0)

    @pl.when(m + 1 < n_used)
    def _():
        request(m + 1, 1 - slot)

    @pl.when(m < n_used)
    def _():
        def wait(r, carry):
            _row_copy(x_hbm, 0, buf.at[slot], r, sem.at[slot]).wait()
            return carry
        lax.fori_loop(0, MOE_UP_TM, wait, 0, unroll=8)
        xs_ref[...] = _rms(buf[slot], g_ref[...]).astype(BF16)

    @pl.when(m >= n_used)
    def _():
        xs_ref[...] = jnp.zeros_like(xs_ref)


def _slot_gather(slot_token, n_used, x, g):
    return pl.pallas_call(
        _slot_gather_kernel,
        grid_spec=pltpu.PrefetchScalarGridSpec(
            num_scalar_prefetch=2,
            grid=(N_SLOTS // MOE_UP_TM,),
            in_specs=[pl.BlockSpec(memory_space=pl.ANY),
                      pl.BlockSpec((1, D_MODEL), lambda m, tok, nu: (0, 0))],
            out_specs=pl.BlockSpec((MOE_UP_TM, D_MODEL), lambda m, tok, nu: (m, 0)),
            scratch_shapes=[pltpu.VMEM((2, MOE_UP_TM, D_MODEL), F32), pltpu.SemaphoreType.DMA((2,))],
        ),
        out_shape=jax.ShapeDtypeStruct((N_SLOTS, D_MODEL), BF16),
        compiler_params=_cparams(("arbitrary",)),
        name="slot_gather",
    )(slot_token, n_used, x, g)


def _combine_kernel(p1_ref, p2_ref, x_ref, y_hbm, gate_ref, g_ref, o_ref, buf1, buf2, sem, *, final_norm):
    rows = buf1.shape[0]
    base = pl.program_id(0) * rows

    def issue(r, carry):
        _row_copy(y_hbm, p1_ref[base + r], buf1, r, sem).start()
        _row_copy(y_hbm, p2_ref[base + r], buf2, r, sem).start()
        return carry
    lax.fori_loop(0, rows, issue, 0, unroll=8)

    def wait(r, carry):
        _row_copy(y_hbm, 0, buf1, r, sem).wait()
        _row_copy(y_hbm, 0, buf2, r, sem).wait()
        return carry
    lax.fori_loop(0, rows, wait, 0, unroll=8)
    gates = gate_ref[...]
    out = x_ref[...] + (gates[:, 0:1] * buf1[...] + gates[:, 1:2] * buf2[...])
    o_ref[...] = _rms(out, g_ref[...]) if final_norm else out


def _combine(pos1, pos2, x, y, gates, g_final, row0=0, n_rows=M, rows=COMBINE_ROWS):
    assert row0 % rows == 0 and n_rows % rows == 0
    final_norm = g_final is not None
    g = g_final if final_norm else jnp.ones((1, D_MODEL), F32)
    first = row0 // rows
    return pl.pallas_call(
        functools.partial(_combine_kernel, final_norm=final_norm),
        grid_spec=pltpu.PrefetchScalarGridSpec(
            num_scalar_prefetch=2,
            grid=(n_rows // rows,),
            in_specs=[pl.BlockSpec((rows, D_MODEL), lambda i, p1, p2: (first + i, 0)),
                      pl.BlockSpec(memory_space=pl.ANY),
                      pl.BlockSpec((rows, LANES), lambda i, p1, p2: (first + i, 0)),
                      pl.BlockSpec((1, D_MODEL), lambda i, p1, p2: (0, 0))],
            out_specs=pl.BlockSpec((rows, D_MODEL), lambda i, p1, p2: (i, 0)),
            scratch_shapes=[pltpu.VMEM((rows, D_MODEL), F32), pltpu.VMEM((rows, D_MODEL), F32),
                            pltpu.SemaphoreType.DMA(())],
        ),
        out_shape=jax.ShapeDtypeStruct((n_rows, D_MODEL), F32),
        compiler_params=_cparams(("arbitrary",)),
        name="combine",
    )(pos1[row0:row0 + n_rows], pos2[row0:row0 + n_rows], x, y, gates, g)


def _moe_up_kernel(te_ref, nu_ref, x_ref, wg_ref, wu_ref, wdf_ref, act_ref, wdb_ref, wg_scr, wu_scr):
    m = pl.program_id(1)
    for cidx in range(D_MODEL // MOE_TN):
        wdb_ref[cidx] = wdf_ref[:, cidx * MOE_TN:(cidx + 1) * MOE_TN].astype(BF16)

    @pl.when(m < nu_ref[0])
    def _():
        prev = te_ref[jnp.maximum(m - 1, 0)]

        @pl.when(jnp.logical_or(m == 0, te_ref[m] != prev))
        def _():
            wg_scr[...] = wg_ref[...].astype(BF16)
            wu_scr[...] = wu_ref[...].astype(BF16)

        x = x_ref[...]
        a = jnp.dot(x, wg_scr[...], preferred_element_type=F32)
        u = jnp.dot(x, wu_scr[...], preferred_element_type=F32)
        act_ref[...] = ((a * _sigmoid(a)) * u).astype(BF16)

    @pl.when(m >= nu_ref[0])
    def _():
        act_ref[...] = jnp.zeros_like(act_ref)


def _side_chunk(total_rows, n_steps):
    chunk = BF16_SUBLANES
    while total_rows % chunk or total_rows // chunk > n_steps:
        chunk += BF16_SUBLANES
    return chunk


def _moe_up(tile_expert, n_used, xs, wg, wu, wd):
    n_j, n_m = E_FF // MOE_TF, N_SLOTS // MOE_UP_TM
    wd_rows = N_EXPERTS * E_FF
    chunk = _side_chunk(wd_rows, n_j * n_m)
    last_chunk = wd_rows // chunk - 1
    side = lambda j, m, te, nu: (jnp.minimum(j * n_m + m, last_chunk), 0)
    return pl.pallas_call(
        _moe_up_kernel,
        grid_spec=pltpu.PrefetchScalarGridSpec(
            num_scalar_prefetch=2,
            grid=(n_j, n_m),
            in_specs=[
                pl.BlockSpec((MOE_UP_TM, D_MODEL), lambda j, m, te, nu: (m, 0)),
                pl.BlockSpec((None, D_MODEL, MOE_TF), lambda j, m, te, nu: (te[m], 0, j)),
                pl.BlockSpec((None, D_MODEL, MOE_TF), lambda j, m, te, nu: (te[m], 0, j)),
                pl.BlockSpec((chunk, D_MODEL), side),
            ],
            out_specs=[pl.BlockSpec((None, MOE_UP_TM, MOE_TF), lambda j, m, te, nu: (j, m, 0)),
                       pl.BlockSpec((D_MODEL // MOE_TN, chunk, MOE_TN),
                                    lambda j, m, te, nu: (0, jnp.minimum(j * n_m + m, last_chunk), 0))],
            scratch_shapes=[pltpu.VMEM((D_MODEL, MOE_TF), BF16), pltpu.VMEM((D_MODEL, MOE_TF), BF16)],
        ),
        out_shape=[jax.ShapeDtypeStruct((n_j, N_SLOTS, MOE_TF), BF16),
                   jax.ShapeDtypeStruct((D_MODEL // MOE_TN, wd_rows, MOE_TN), BF16)],
        compiler_params=_cparams(("arbitrary", "arbitrary")),
        name="moe_up",
    )(tile_expert, n_used, xs, wg, wu, wd.reshape(wd_rows, D_MODEL))


def _moe_down_kernel(te_ref, nu_ref, a_ref, wd_ref, y_ref):
    m = pl.program_id(1)

    @pl.when(m < nu_ref[0])
    def _():
        acc = jnp.dot(a_ref[0], wd_ref[0:MOE_TF, :], preferred_element_type=F32)
        for f in range(1, E_FF // MOE_TF):
            acc = acc + jnp.dot(a_ref[f], wd_ref[f * MOE_TF:(f + 1) * MOE_TF, :], preferred_element_type=F32)
        y_ref[...] = acc

    @pl.when(m >= nu_ref[0])
    def _():
        y_ref[...] = jnp.zeros_like(y_ref)


def _moe_down(tile_expert, n_used, act, wd_tiles):
    return pl.pallas_call(
        _moe_down_kernel,
        grid_spec=pltpu.PrefetchScalarGridSpec(
            num_scalar_prefetch=2,
            grid=(D_MODEL // MOE_TN, N_SLOTS // MOE_DN_TM),
            in_specs=[
                pl.BlockSpec((E_FF // MOE_TF, MOE_DN_TM, MOE_TF), lambda j, m, te, nu: (0, m, 0)),
                pl.BlockSpec((None, E_FF, MOE_TN), lambda j, m, te, nu: (j, te[m], 0)),
            ],
            out_specs=pl.BlockSpec((MOE_DN_TM, MOE_TN), lambda j, m, te, nu: (m, j)),
        ),
        out_shape=jax.ShapeDtypeStruct((N_SLOTS, D_MODEL), F32),
        compiler_params=_cparams(("arbitrary", "arbitrary")),
        name="moe_down",
    )(tile_expert, n_used, act, wd_tiles)


def _final_norm_kernel(x_ref, g_ref, o_ref):
    o_ref[...] = _rms(x_ref[...], g_ref[...])


def _final_norm(x, g):
    spec = pl.BlockSpec((COMBINE_ROWS, D_MODEL), lambda i: (i, 0))
    return pl.pallas_call(
        _final_norm_kernel,
        grid=(M // COMBINE_ROWS,),
        in_specs=[spec, pl.BlockSpec((1, D_MODEL), lambda i: (0, 0))],
        out_specs=spec,
        out_shape=jax.ShapeDtypeStruct((M, D_MODEL), F32),
        compiler_params=_cparams(("parallel",)),
        name="final_norm",
    )(x, g)


def _tile_map(ends, tm):
    n_tiles = N_SLOTS // tm
    n_used = (ends[-1] // tm).astype(jnp.int32)
    tile_start = jnp.arange(n_tiles, dtype=jnp.int32) * tm
    tile_expert = jnp.sum((tile_start[:, None] >= ends[None, :]).astype(jnp.int32), axis=1)
    last = jnp.sum((jnp.maximum(n_used - 1, 0) * tm >= ends).astype(jnp.int32))
    tile_expert = jnp.where(jnp.arange(n_tiles) < n_used, tile_expert, last)
    return jnp.minimum(tile_expert, N_EXPERTS - 1).astype(jnp.int32), n_used.reshape(1)


def _route(ids, counts):
    counts = counts[0, :N_EXPERTS].astype(jnp.int32)
    padded = ((counts + MOE_PAD - 1) // MOE_PAD) * MOE_PAD
    ends = jnp.cumsum(padded)
    starts = ends - padded
    pos1 = starts[ids[:, 0]] + ids[:, 2]
    pos2 = starts[ids[:, 1]] + ids[:, 3]
    tok = jnp.arange(M, dtype=jnp.int32)
    slot_token = jnp.zeros((N_SLOTS,), jnp.int32).at[jnp.concatenate([pos1, pos2])].set(jnp.concatenate([tok, tok]))
    return slot_token, _tile_map(ends, MOE_UP_TM), _tile_map(ends, MOE_DN_TM), pos1, pos2


def _rope_tables():
    half = ROT_DIM // 2
    pos = jnp.concatenate([jnp.tile(jnp.arange(SEQ, dtype=jnp.int32), BATCH),
                           jnp.full((MS,), PAST_LEN, jnp.int32)])
    inv = jnp.power(ROPE_THETA, -jnp.arange(half, dtype=F32) / half)
    ang = pos.astype(F32)[:, None] * inv[None, :]
    cos, sin = jnp.cos(ang), jnp.sin(ang)
    pad = HEAD_DIM - ROT_DIM
    one = jnp.ones((M, pad), F32)
    zero = jnp.zeros((M, pad), F32)
    zh = jnp.zeros((M, half), F32)
    cs = jnp.concatenate([cos, cos, one], axis=1)
    sa = jnp.concatenate([-sin, zh, zero], axis=1)
    sb = jnp.concatenate([zh, sin, zero], axis=1)
    rep = LANES // HEAD_DIM
    return jnp.tile(cs, (1, rep)), jnp.tile(sa, (1, rep)), jnp.tile(sb, (1, rep))


def _pad_lanes(a):
    return jnp.pad(a, ((0, 0), (0, LANES - a.shape[1])))


def kernel(x_prompt, x_sample, state_mlstm_C, state_mlstm_n, state_mlstm_m, cache_swa_k, cache_swa_v, norm_mix_g, w_in, b_igate, b_fgate, mlstm_norm_g, attn_sinks, w_branch_m, w_branch_a, w_out, norm_ffn_g, w_gate_dense, w_up_dense, w_down_dense, w_router, w_gate_moe, w_up_moe, w_down_moe, norm_final_g):
    x = jnp.concatenate([x_prompt.reshape(MP, D_MODEL), x_sample.reshape(MS, D_MODEL)], axis=0)
    cs, sa, sb = _rope_tables()
    state_n = state_mlstm_n.reshape(DEPTH, MS, M_HEADS * M_DQK)
    kbuf = cache_swa_k.reshape(DEPTH, MS, WINDOW, KVW)
    vbuf = cache_swa_v.reshape(DEPTH, MS, WINDOW, KVW)
    outs = {name: [] for name in ("Cp", "np", "mp", "kp", "vp", "ns", "ms")}
    y_final = None
    c_all = k_all = v_all = None

    w_tiles, w_gates = _repack_w_in(w_in)

    for l in range(DEPTH):
        bias = _pad_lanes(jnp.concatenate([b_igate[l], b_fgate[l]])[None, :])
        gn = mlstm_norm_g[l][None, :]
        sinks_b = jnp.broadcast_to(attn_sinks[l][:, None], (N_HEADS, LANES))

        z, kv, gates = _inproj(l, x, norm_mix_g[l][None, :], w_tiles, w_gates, cs, sa, sb)

        hm_p, c_p, n_p, m_p = _mlstm_prompt(z, gates, bias, gn)
        ha_p = _swa_prompt(z, kv, attn_sinks[l])
        kv_p = jnp.stack([kv[(b + 1) * SEQ - WINDOW:(b + 1) * SEQ] for b in range(BATCH)])
        kv_p = kv_p.reshape(BATCH, WINDOW, 2, KV_HEADS, HEAD_DIM)
        outs["Cp"].append(c_p)
        outs["np"].append(n_p.reshape(BATCH, M_HEADS, M_DQK))
        outs["mp"].append(m_p[:, :, 0, 0])
        outs["kp"].append(kv_p[:, :, 0])
        outs["vp"].append(kv_p[:, :, 1])

        zs = z[MP:].astype(F32)
        kv_s = kv[MP:]
        qt = zs[:, C_QM:C_KM].T.reshape(M_HEADS, M_DQK, MS)
        kt = zs[:, C_KM:C_VM].T.reshape(M_HEADS, M_DQK, MS)
        m0p = _pad_lanes(state_mlstm_m[l])
        hm_s, c_all, n_s, m_s = _mlstm_decode(l, zs, qt, kt, gates[MP:], bias, m0p, gn, state_mlstm_C, state_n, c_all)
        kn = kv_s[:, :KVW]
        vn = kv_s[:, KVW:]
        ha_s, k_all, v_all = _swa_decode(
            l, zs[:, C_QA:C_KA].reshape(MS, N_HEADS, HEAD_DIM), kn.reshape(MS, KV_HEADS, HEAD_DIM),
            vn.reshape(MS, KV_HEADS, HEAD_DIM), kn, vn, kbuf, vbuf, sinks_b, k_all, v_all)
        outs["ns"].append(n_s.reshape(MS, M_HEADS, M_DQK))
        outs["ms"].append(m_s[:, :M_HEADS])

        hm = jnp.concatenate([hm_p, hm_s.astype(BF16)], axis=0)
        ha = jnp.concatenate([ha_p, ha_s.reshape(MS, N_HEADS * HEAD_DIM).astype(BF16)], axis=0)
        merged = _merge(hm, ha, w_branch_m[l].astype(BF16), w_branch_a[l].astype(BF16), z)
        x = _mm_res(merged, w_out[l].astype(BF16), x, OUT_TM, D_MODEL, "out_proj")

        if l % 2 == 0:
            jd = l // 2
            act = _ffn_up(x, norm_ffn_g[l][None, :], w_gate_dense[jd].astype(BF16), w_up_dense[jd].astype(BF16))
            x = _mm_res(act, w_down_dense[jd].astype(BF16), x, TM, TN, "ffn_down")
        else:
            jm = l // 2
            g_ffn = norm_ffn_g[l][None, :]
            ids, gts, counts = _router(x, g_ffn, _pad_lanes(w_router[jm]))
            slot_token, (te_up, nu_up), (te_dn, nu_dn), pos1, pos2 = _route(ids, counts)
            xs = _slot_gather(slot_token, nu_up, x, g_ffn)
            act, wd_bf16 = _moe_up(te_up, nu_up, xs, w_gate_moe[jm], w_up_moe[jm], w_down_moe[jm])
            y = _moe_down(te_dn, nu_dn, act, wd_bf16)
            if l == DEPTH - 1:
                g_fin = norm_final_g[None, :]
                y_final = (_combine(pos1, pos2, x, y, gts, g_fin, 0, MP, COMBINE_PROMPT_ROWS),
                           _combine(pos1, pos2, x, y, gts, g_fin, MP, MS, MS))
            else:
                x = _combine(pos1, pos2, x, y, gts, None)

    if y_final is None:
        y_all = _final_norm(x, norm_final_g[None, :])
        y_final = (y_all[:MP], y_all[MP:])
    y_prompt = y_final[0].reshape(BATCH, SEQ, D_MODEL)
    y_sample = y_final[1].reshape(MS, 1, D_MODEL)
    st = lambda name: jnp.stack(outs[name])
    return (y_prompt, y_sample, st("Cp"), st("np"), st("mp"), st("kp"), st("vp"),
            c_all, st("ns"), st("ms"),
            k_all.reshape(DEPTH, MS, WINDOW, KV_HEADS, HEAD_DIM), v_all.reshape(DEPTH, MS, WINDOW, KV_HEADS, HEAD_DIM))
```
